```python
import math
import jax, jax.numpy as jnp
from jax import lax
import numpy as np

D_MODEL = 2048
BATCH = 2
SEQ = 4096
DEPTH = 2
DEC_BATCH = 128
DEC_SEQ = 1
PAST_LEN = 8192
PAGE_SIZE = 128

N_A_LAYERS = DEPTH // 2
N_B_LAYERS = DEPTH - N_A_LAYERS
N_DENSE = (DEPTH + 1) // 2
N_MOE = DEPTH // 2

CHUNK = 128
A_GROUPS = 16
A_WIDTH = 2 * D_MODEL
A_GROUP_DIM = A_WIDTH // A_GROUPS

N_HEADS = 16
Q_LORA = 512
KV_LORA = 512
NOPE_DIM = 128
ROPE_DIM = 64
V_DIM = 128
ROPE_BASE = 10000.0
Q_BLOCK = 128

D_FF = 5632
N_EXPERTS = 8
TOP_K = 2
D_EXPERT = 7168

ALPHA = (2.0 * DEPTH) ** 0.25
BETA = (8.0 * DEPTH) ** -0.25
LN_EPS = 1e-5
RMS_EPS = 1e-6

kernel_name = "yoco_gmlp_mla_deepnorm_adaln_step"


def _layernorm(x, g, b):
    xf = x.astype(jnp.float32)
    mu = jnp.mean(xf, -1, keepdims=True)
    var = jnp.mean(jnp.square(xf - mu), -1, keepdims=True)
    return ((xf - mu) * lax.rsqrt(var + LN_EPS) * g + b).astype(x.dtype)


def _rmsnorm(x, g):
    xf = x.astype(jnp.float32)
    return (xf * lax.rsqrt(jnp.mean(jnp.square(xf), -1, keepdims=True) + RMS_EPS) * g).astype(x.dtype)


def _rope_tables(pos):
    half = ROPE_DIM // 2
    inv = ROPE_BASE ** (-jnp.arange(half, dtype=jnp.float32) / half)
    ang = pos.astype(jnp.float32)[:, None] * inv[None, :]
    return jnp.cos(ang), jnp.sin(ang)


def _rope(x, cos, sin):
    x1, x2 = jnp.split(x.astype(jnp.float32), 2, axis=-1)
    return jnp.concatenate([x1 * cos - x2 * sin, x1 * sin + x2 * cos], -1).astype(x.dtype)


def _adaln(c, w, b):
    mod = jax.nn.silu(c) @ w + b
    sh_m, sc_m, g_m, sh_f, sc_f, g_f = jnp.split(mod[:, None, :], 6, axis=-1)
    return sh_m, sc_m, g_m, sh_f, sc_f, g_f


def _chunk_mlp_project(h, w_in, v_g, v_b):
    z = jax.nn.gelu(h @ w_in)
    u, v = jnp.split(z, 2, axis=-1)
    return u, _layernorm(v, v_g, v_b)


def _spatial_gate_prompt(v, w_s, b_s):
    B, S, _ = v.shape
    ws = jnp.where(jnp.tril(jnp.ones((CHUNK, CHUNK), dtype=bool)), w_s, 0.0)
    vc = v.reshape(B, S // CHUNK, CHUNK, A_GROUPS, A_GROUP_DIM)
    s = jnp.einsum("gts,bnsgd->bntgd", ws, vc) + b_s.T[:, :, None]
    return s.reshape(B, S, A_WIDTH)


def _spatial_gate_sample(v, w_s, b_s):
    Bd, T, _ = v.shape
    ws = jnp.where(jnp.tril(jnp.ones((T, T), dtype=bool)), w_s[:, :T, :T], 0.0)
    vg = v.reshape(Bd, T, A_GROUPS, A_GROUP_DIM)
    s = jnp.einsum("gts,bsgd->btgd", ws, vg) + b_s[:, :T].T[:, :, None]
    return s.reshape(Bd, T, A_WIDTH)


def _mla_kv(x, w_dkv, kv_g, w_kr, cos, sin):
    ckv = _rmsnorm(x @ w_dkv, kv_g)
    kpe = _rope(x @ w_kr, cos, sin)
    return ckv, kpe


def _mla_query(h, w_dq, q_g, w_uq, w_qr, cos, sin):
    cq = _rmsnorm(h @ w_dq, q_g)
    q_nope = jnp.einsum("btc,chd->bthd", cq, w_uq)
    q_pe = _rope(jnp.einsum("btc,chr->bthr", cq, w_qr), cos[:, None, :], sin[:, None, :])
    return q_nope, q_pe


def _mla_attend_prompt(q_nope, q_pe, ckv, kpe, w_uk, w_uv):
    B, S = q_nope.shape[:2]
    scale = (NOPE_DIM + ROPE_DIM) ** -0.5
    k_nope = jnp.einsum("bsc,chd->bshd", ckv, w_uk)
    v = jnp.einsum("bsc,chd->bshd", ckv, w_uv)
    nb = S // Q_BLOCK
    qn = q_nope.reshape(B, nb, Q_BLOCK, N_HEADS, NOPE_DIM).transpose(1, 0, 2, 3, 4)
    qp = q_pe.reshape(B, nb, Q_BLOCK, N_HEADS, ROPE_DIM).transpose(1, 0, 2, 3, 4)
    kpos = jnp.arange(S)

    def block(args):
        qn_b, qp_b, i = args
        s = (jnp.einsum("bqhd,bkhd->bhqk", qn_b, k_nope)
             + jnp.einsum("bqhr,bkr->bhqk", qp_b, kpe)).astype(jnp.float32) * scale
        qpos = i * Q_BLOCK + jnp.arange(Q_BLOCK)
        s = jnp.where(kpos[None, :] <= qpos[:, None], s, -jnp.inf)
        p = jax.nn.softmax(s, axis=-1).astype(v.dtype)
        return jnp.einsum("bhqk,bkhd->bqhd", p, v)

    o = lax.map(block, (qn, qp, jnp.arange(nb)))
    return o.transpose(1, 0, 2, 3, 4).reshape(B, S, N_HEADS * V_DIM)


def _mla_attend_sample(q_nope, q_pe, ckv_new, kpe_new, cache_ckv, cache_kpe, page_table, w_uk, w_uv):
    Bd, T = q_nope.shape[:2]
    scale = (NOPE_DIM + ROPE_DIM) ** -0.5
    q_lat = jnp.einsum("bthd,chd->bthc", q_nope, w_uk)

    def scores(ck, kp):
        return (jnp.einsum("bthc,bpc->bhtp", q_lat, ck)
                + jnp.einsum("bthr,bpr->bhtp", q_pe, kp)).astype(jnp.float32) * scale

    def update(carry, s, ck):
        m, l, acc = carry
        m_new = jnp.maximum(m, jnp.max(s, -1))
        corr = jnp.exp(m - m_new)
        p = jnp.exp(s - m_new[..., None])
        l = l * corr + jnp.sum(p, -1)
        acc = acc * corr[..., None] + jnp.einsum("bhtp,bpc->bhtc", p, ck.astype(jnp.float32))
        return (m_new, l, acc)

    def page_step(carry, phys):
        ck = cache_ckv[phys]
        kp = cache_kpe[phys]
        return update(carry, scores(ck, kp), ck), None

    init = (jnp.full((Bd, N_HEADS, T), -jnp.inf, jnp.float32),
            jnp.zeros((Bd, N_HEADS, T), jnp.float32),
            jnp.zeros((Bd, N_HEADS, T, KV_LORA), jnp.float32))
    carry, _ = lax.scan(page_step, init, page_table.T)
    s_new = scores(ckv_new, kpe_new)
    s_new = jnp.where(jnp.tril(jnp.ones((T, T), dtype=bool)), s_new, -jnp.inf)
    _, l, acc = update(carry, s_new, ckv_new)
    o_lat = (acc / l[..., None]).astype(ckv_new.dtype)
    o = jnp.einsum("bhtc,chd->bthd", o_lat, w_uv)
    return o.reshape(Bd, T, N_HEADS * V_DIM)


def _swiglu(h, w_gate, w_up, w_down):
    return (jax.nn.silu(h @ w_gate) * (h @ w_up)) @ w_down


def _moe(h, w_router, b_router, w_gate, w_up, w_down):
    logits = (h @ w_router).astype(jnp.float32) + b_router
    probs = jax.nn.softmax(logits, axis=-1)
    top_p, top_i = lax.top_k(probs, TOP_K)
    top_p = top_p / jnp.sum(top_p, -1, keepdims=True)
    gates = jnp.sum(jax.nn.one_hot(top_i, N_EXPERTS, dtype=jnp.float32) * top_p[..., None], axis=-2)
    gates = gates.astype(h.dtype)
    out = jnp.zeros_like(h)
    for e in range(N_EXPERTS):
        out = out + gates[..., e:e + 1] * _swiglu(h, w_gate[e], w_up[e], w_down[e])
    return out


def _trunk(x, c, is_prompt, cache_ckv, cache_kpe, page_table, p):
    T = x.shape[1]
    pos = jnp.arange(T, dtype=jnp.int32) + (0 if is_prompt else PAST_LEN)
    cos, sin = _rope_tables(pos)
    ckv = None
    kpe = None
    a_states = []
    for l in range(DEPTH):
        sh_m, sc_m, g_m, sh_f, sc_f, g_f = _adaln(c, p["w_ada"][l], p["b_ada"][l])
        h = x * (1 + sc_m) + sh_m
        if l < N_A_LAYERS:
            u, v = _chunk_mlp_project(h, p["a_w_in"][l], p["a_v_g"][l], p["a_v_b"][l])
            if is_prompt:
                s = _spatial_gate_prompt(v, p["a_w_s"][l], p["a_b_s"][l])
            else:
                s = _spatial_gate_sample(v, p["a_w_s"][l], p["a_b_s"][l])
                a_states.append(v)
            mix = (u * s) @ p["a_w_out"][l]
        else:
            j = l - N_A_LAYERS
            q_nope, q_pe = _mla_query(h, p["b_w_dq"][j], p["b_q_g"][j], p["b_w_uq"][j], p["b_w_qr"][j], cos, sin)
            if is_prompt:
                o = _mla_attend_prompt(q_nope, q_pe, ckv, kpe, p["kv_w_uk"], p["kv_w_uv"])
            else:
                o = _mla_attend_sample(q_nope, q_pe, ckv, kpe, cache_ckv, cache_kpe, page_table,
                                       p["kv_w_uk"], p["kv_w_uv"])
            mix = o @ p["b_w_o"][j]
        x = _layernorm(ALPHA * x + g_m * mix, p["ln_g"][l, 0], p["ln_b"][l, 0])
        h = x * (1 + sc_f) + sh_f
        if l % 2 == 0:
            k = l // 2
            f = _swiglu(h, p["ffn_w_gate"][k], p["ffn_w_up"][k], p["ffn_w_down"][k])
        else:
            k = l // 2
            f = _moe(h, p["moe_w_router"][k], p["moe_b_router"][k], p["moe_w_gate"][k],
                     p["moe_w_up"][k], p["moe_w_down"][k])
        x = _layernorm(ALPHA * x + g_f * f, p["ln_g"][l, 1], p["ln_b"][l, 1])
        if l == N_A_LAYERS - 1:
            ckv, kpe = _mla_kv(x, p["kv_w_dkv"], p["kv_g"], p["kv_w_kr"], cos, sin)
    return x, ckv, kpe, a_states


def setup_inputs(seed: int = 0) -> dict:
    key = jax.random.key(seed)
    keys = iter(jax.random.split(key, 64))

    def nrm(shape, scale):
        return jax.random.normal(next(keys), shape, jnp.float32) * scale

    n_pages = PAST_LEN // PAGE_SIZE
    n_pool = (DEC_BATCH * n_pages * 5) // 4
    D = D_MODEL
    inp = {}
    inp["x_prompt"] = nrm((BATCH, SEQ, D), 1.0)
    inp["x_sample"] = nrm((DEC_BATCH, DEC_SEQ, D), 1.0)
    inp["cache_ckv"] = nrm((n_pool, PAGE_SIZE, KV_LORA), 1.0)
    inp["cache_kpe"] = nrm((n_pool, PAGE_SIZE, ROPE_DIM), 1.0)
    perm = jax.random.permutation(next(keys), n_pool)
    inp["page_table"] = perm[:DEC_BATCH * n_pages].reshape(DEC_BATCH, n_pages).astype(jnp.int32)
    inp["c_prompt"] = nrm((BATCH, D), 1.0)
    inp["c_sample"] = nrm((DEC_BATCH, D), 1.0)
    inp["w_ada"] = nrm((DEPTH, D, 6 * D), 0.5 * D ** -0.5)
    inp["b_ada"] = nrm((DEPTH, 6 * D), 0.02)
    inp["ln_g"] = 1.0 + nrm((DEPTH, 2, D), 0.02)
    inp["ln_b"] = nrm((DEPTH, 2, D), 0.02)
    inp["a_w_in"] = nrm((N_A_LAYERS, D, 2 * A_WIDTH), D ** -0.5)
    inp["a_v_g"] = 1.0 + nrm((N_A_LAYERS, A_WIDTH), 0.02)
    inp["a_v_b"] = nrm((N_A_LAYERS, A_WIDTH), 0.02)
    inp["a_w_s"] = nrm((N_A_LAYERS, A_GROUPS, CHUNK, CHUNK), CHUNK ** -0.5)
    inp["a_b_s"] = 1.0 + nrm((N_A_LAYERS, A_GROUPS, CHUNK), 0.02)
    inp["a_w_out"] = nrm((N_A_LAYERS, A_WIDTH, D), BETA * A_WIDTH ** -0.5)
    inp["b_w_dq"] = nrm((N_B_LAYERS, D, Q_LORA), D ** -0.5)
    inp["b_q_g"] = 1.0 + nrm((N_B_LAYERS, Q_LORA), 0.02)
    inp["b_w_uq"] = nrm((N_B_LAYERS, Q_LORA, N_HEADS, NOPE_DIM), Q_LORA ** -0.5)
    inp["b_w_qr"] = nrm((N_B_LAYERS, Q_LORA, N_HEADS, ROPE_DIM), Q_LORA ** -0.5)
    inp["b_w_o"] = nrm((N_B_LAYERS, N_HEADS * V_DIM, D), BETA * (N_HEADS * V_DIM) ** -0.5)
    inp["kv_w_dkv"] = nrm((D, KV_LORA), D ** -0.5)
    inp["kv_g"] = 1.0 + nrm((KV_LORA,), 0.02)
    inp["kv_w_kr"] = nrm((D, ROPE_DIM), D ** -0.5)
    inp["kv_w_uk"] = nrm((KV_LORA, N_HEADS, NOPE_DIM), KV_LORA ** -0.5)
    inp["kv_w_uv"] = nrm((KV_LORA, N_HEADS, V_DIM), BETA * KV_LORA ** -0.5)
    inp["ffn_w_gate"] = nrm((N_DENSE, D, D_FF), D ** -0.5)
    inp["ffn_w_up"] = nrm((N_DENSE, D, D_FF), D ** -0.5)
    inp["ffn_w_down"] = nrm((N_DENSE, D_FF, D), BETA * D_FF ** -0.5)
    inp["moe_w_router"] = nrm((N_MOE, D, N_EXPERTS), D ** -0.5)
    inp["moe_b_router"] = nrm((N_MOE, N_EXPERTS), 0.01)
    inp["moe_w_gate"] = nrm((N_MOE, N_EXPERTS, D, D_EXPERT), D ** -0.5)
    inp["moe_w_up"] = nrm((N_MOE, N_EXPERTS, D, D_EXPERT), D ** -0.5)
    inp["moe_w_down"] = nrm((N_MOE, N_EXPERTS, D_EXPERT, D), BETA * D_EXPERT ** -0.5)
    return inp


def reference(x_prompt, x_sample, cache_ckv, cache_kpe, page_table, c_prompt, c_sample,
              w_ada, b_ada, ln_g, ln_b,
              a_w_in, a_v_g, a_v_b, a_w_s, a_b_s, a_w_out,
              b_w_dq, b_q_g, b_w_uq, b_w_qr, b_w_o,
              kv_w_dkv, kv_g, kv_w_kr, kv_w_uk, kv_w_uv,
              ffn_w_gate, ffn_w_up, ffn_w_down,
              moe_w_router, moe_b_router, moe_w_gate, moe_w_up, moe_w_down):
    p = dict(w_ada=w_ada, b_ada=b_ada, ln_g=ln_g, ln_b=ln_b,
             a_w_in=a_w_in, a_v_g=a_v_g, a_v_b=a_v_b, a_w_s=a_w_s, a_b_s=a_b_s, a_w_out=a_w_out,
             b_w_dq=b_w_dq, b_q_g=b_q_g, b_w_uq=b_w_uq, b_w_qr=b_w_qr, b_w_o=b_w_o,
             kv_w_dkv=kv_w_dkv, kv_g=kv_g, kv_w_kr=kv_w_kr, kv_w_uk=kv_w_uk, kv_w_uv=kv_w_uv,
             ffn_w_gate=ffn_w_gate, ffn_w_up=ffn_w_up, ffn_w_down=ffn_w_down,
             moe_w_router=moe_w_router, moe_b_router=moe_b_router, moe_w_gate=moe_w_gate,
             moe_w_up=moe_w_up, moe_w_down=moe_w_down)
    y_prompt, ckv_prompt, kpe_prompt, _ = _trunk(x_prompt, c_prompt, True, cache_ckv, cache_kpe, page_table, p)
    y_sample, ckv_sample, kpe_sample, a_states = _trunk(x_sample, c_sample, False, cache_ckv, cache_kpe,
                                                        page_table, p)
    v_sample = jnp.stack(a_states)
    return (y_prompt, y_sample, ckv_prompt, kpe_prompt, ckv_sample, kpe_sample, v_sample)
```

```python
import functools
import math

import jax
import jax.numpy as jnp
from jax import lax
from jax.experimental import pallas as pl
from jax.experimental.pallas import tpu as pltpu

BF = jnp.bfloat16
F32 = jnp.float32

ROW_TILE = 128
LN_EPS = 1e-5
RMS_EPS = 1e-6
ROPE_BASE = 10000.0
NEG_INF = float("-inf")
VMEM_LIMIT = 56 * 1024 * 1024


def _params(n_axes, vmem=VMEM_LIMIT):
    return pltpu.CompilerParams(dimension_semantics=("arbitrary",) * n_axes, vmem_limit_bytes=vmem)


def _gelu_tanh(x):
    return 0.5 * x * (1.0 + jnp.tanh(math.sqrt(2.0 / math.pi) * (x + 0.044715 * x * x * x)))


def _silu(x):
    return x * (1.0 / (1.0 + jnp.exp(-x)))


def _layernorm_rows(x, g, b):
    mu = jnp.mean(x, axis=-1, keepdims=True)
    xc = x - mu
    var = jnp.mean(xc * xc, axis=-1, keepdims=True)
    return xc * lax.rsqrt(var + LN_EPS) * g + b


def _mm_kernel(*refs, act, has_bias, has_add):
    x_ref, w_ref = refs[0], refs[1]
    k = 2
    bias_ref = add_ref = None
    if has_bias:
        bias_ref = refs[k]; k += 1
    if has_add:
        add_ref = refs[k]; k += 1
    o_ref, wbf_ref = refs[k], refs[k + 1]

    @pl.when(pl.program_id(1) == 0)
    def _():
        wbf_ref[...] = w_ref[...].astype(BF)

    x = x_ref[...]
    if act == "silu_in":
        x = _silu(x.astype(F32))
    acc = jnp.dot(x.astype(BF), wbf_ref[...], preferred_element_type=F32)
    if has_bias:
        acc = acc + bias_ref[...]
    if act == "gelu":
        acc = _gelu_tanh(acc)
    if has_add:
        acc = acc + add_ref[...]
    o_ref[...] = acc.astype(o_ref.dtype)


def _matmul(x, w, *, bm, bn, out_dtype, n_off=0, n_out=None, act=None, bias=None, add=None,
            lead=None, name="mm"):
    M, K = x.shape
    N = w.shape[-1] if n_out is None else n_out
    bn = min(bn, N)
    assert M % bm == 0 and N % bn == 0 and n_off % bn == 0
    offb = n_off // bn
    if lead is None:
        w_spec = pl.BlockSpec((K, bn), lambda j, i: (0, j + offb))
    else:
        lead_idx = tuple(lead)
        w_spec = pl.BlockSpec((None,) * len(lead_idx) + (K, bn), lambda j, i: lead_idx + (0, j + offb))
    in_specs = [pl.BlockSpec((bm, K), lambda j, i: (i, 0)), w_spec]
    args = [x, w]
    if bias is not None:
        in_specs.append(pl.BlockSpec((1, bn), lambda j, i: (0, j + offb)))
        args.append(bias)
    if add is not None:
        in_specs.append(pl.BlockSpec((bm, bn), lambda j, i: (i, j)))
        args.append(add)
    return pl.pallas_call(
        functools.partial(_mm_kernel, act=act, has_bias=bias is not None, has_add=add is not None),
        out_shape=jax.ShapeDtypeStruct((M, N), out_dtype),
        grid=(N // bn, M // bm),
        in_specs=in_specs,
        out_specs=pl.BlockSpec((bm, bn), lambda j, i: (i, j)),
        scratch_shapes=[pltpu.VMEM((K, bn), BF)],
        compiler_params=_params(2),
        name=name,
    )(*args)


def _swiglu_up_kernel(*refs, gate_col):
    if gate_col is None:
        x_ref, wg_ref, wu_ref, o_ref, wgb_ref, wub_ref = refs
        gates_ref = None
    else:
        x_ref, wg_ref, wu_ref, gates_ref, o_ref, wgb_ref, wub_ref = refs

    @pl.when(pl.program_id(1) == 0)
    def _():
        wgb_ref[...] = wg_ref[...].astype(BF)
        wub_ref[...] = wu_ref[...].astype(BF)

    x = x_ref[...].astype(BF)
    g = jnp.dot(x, wgb_ref[...], preferred_element_type=F32)
    u = jnp.dot(x, wub_ref[...], preferred_element_type=F32)
    a = _silu(g) * u
    if gates_ref is not None:
        a = a * gates_ref[:, gate_col:gate_col + 1]
    o_ref[...] = a.astype(o_ref.dtype)


def _swiglu_up(x, wg, wu, *, bm, bn, lead, gates=None, gate_col=None, name="swiglu_up"):
    M, K = x.shape
    N = wg.shape[-1]
    bn = min(bn, N)
    assert M % bm == 0 and N % bn == 0
    lead_idx = tuple(lead)
    w_spec = pl.BlockSpec((None,) * len(lead_idx) + (K, bn), lambda j, i: lead_idx + (0, j))
    in_specs = [pl.BlockSpec((bm, K), lambda j, i: (i, 0)), w_spec, w_spec]
    args = [x, wg, wu]
    if gates is not None:
        in_specs.append(pl.BlockSpec((bm, gates.shape[1]), lambda j, i: (i, 0)))
        args.append(gates)
    return pl.pallas_call(
        functools.partial(_swiglu_up_kernel, gate_col=gate_col if gates is not None else None),
        out_shape=jax.ShapeDtypeStruct((M, N), BF),
        grid=(N // bn, M // bm),
        in_specs=in_specs,
        out_specs=pl.BlockSpec((bm, bn), lambda j, i: (i, j)),
        scratch_shapes=[pltpu.VMEM((K, bn), BF), pltpu.VMEM((K, bn), BF)],
        compiler_params=_params(2),
        name=name,
    )(*args)


def _mod_spec(tiles_per_batch, D, chunk):
    return pl.BlockSpec((ROW_TILE, D), lambda i: (i // tiles_per_batch, chunk))


def _modulate_kernel(x_ref, sc_ref, sh_ref, h_ref):
    h_ref[...] = (x_ref[...] * (1.0 + sc_ref[...]) + sh_ref[...]).astype(h_ref.dtype)


def _modulate(x, mod, sc_chunk, sh_chunk, tiles_per_batch):
    T, D = x.shape
    row = pl.BlockSpec((ROW_TILE, D), lambda i: (i, 0))
    return pl.pallas_call(
        _modulate_kernel,
        out_shape=jax.ShapeDtypeStruct((T, D), BF),
        grid=(T // ROW_TILE,),
        in_specs=[row, _mod_spec(tiles_per_batch, D, sc_chunk), _mod_spec(tiles_per_batch, D, sh_chunk)],
        out_specs=row,
        compiler_params=_params(1),
        name="modulate",
    )(x, mod, mod)


def _postln_kernel(*refs, alpha, modulate_next, emit_bf):
    x_ref, mix_ref, g_ref, lng_ref, lnb_ref = refs[:5]
    k = 5
    if modulate_next:
        sc_ref, sh_ref = refs[k], refs[k + 1]
        k += 2
    xo_ref = refs[k]; k += 1
    xn = _layernorm_rows(alpha * x_ref[...] + g_ref[...] * mix_ref[...], lng_ref[...], lnb_ref[...])
    xo_ref[...] = xn
    if modulate_next:
        refs[k][...] = (xn * (1.0 + sc_ref[...]) + sh_ref[...]).astype(BF)
        k += 1
    if emit_bf:
        refs[k][...] = xn.astype(BF)


def _postln(x, mix, mod_g, g_chunk, ln_g, ln_b, tiles_per_batch, *, alpha, mod_next=None,
            sc_chunk=None, sh_chunk=None, emit_bf=False):
    T, D = x.shape
    row = pl.BlockSpec((ROW_TILE, D), lambda i: (i, 0))
    vec = pl.BlockSpec((1, D), lambda i: (0, 0))
    in_specs = [row, row, _mod_spec(tiles_per_batch, D, g_chunk), vec, vec]
    args = [x, mix, mod_g, ln_g.reshape(1, D), ln_b.reshape(1, D)]
    out_shape = [jax.ShapeDtypeStruct((T, D), F32)]
    out_specs = [row]
    if mod_next is not None:
        in_specs += [_mod_spec(tiles_per_batch, D, sc_chunk), _mod_spec(tiles_per_batch, D, sh_chunk)]
        args += [mod_next, mod_next]
        out_shape.append(jax.ShapeDtypeStruct((T, D), BF))
        out_specs.append(row)
    if emit_bf:
        out_shape.append(jax.ShapeDtypeStruct((T, D), BF))
        out_specs.append(row)
    return pl.pallas_call(
        functools.partial(_postln_kernel, alpha=alpha, modulate_next=mod_next is not None, emit_bf=emit_bf),
        out_shape=out_shape,
        grid=(T // ROW_TILE,),
        in_specs=in_specs,
        out_specs=out_specs,
        compiler_params=_params(1),
        name="postln",
    )(*args)


def _spatial_gate_kernel(vpre_ref, u_ref, ws_ref, b_ref, vg_ref, vb_ref, mix_ref, vout_ref, *, groups):
    v = _layernorm_rows(vpre_ref[...], vg_ref[...], vb_ref[...])

    @pl.when(pl.program_id(0) == pl.num_programs(0) - 1)
    def _():
        vout_ref[...] = v

    vb16 = v.astype(BF)
    gd = v.shape[1] // groups
    for g in range(groups):
        cols = slice(g * gd, (g + 1) * gd)
        s = jnp.dot(ws_ref[g], vb16[:, cols], preferred_element_type=F32) + b_ref[:, cols]
        mix_ref[:, cols] = (u_ref[:, cols].astype(F32) * s).astype(BF)


def _spatial_gate(vpre, u, ws_tiles, b_tiles, v_g, v_b, n_prompt_tiles):
    T, A = vpre.shape
    G = ws_tiles.shape[1]
    row = pl.BlockSpec((ROW_TILE, A), lambda i: (i, 0))
    vec = pl.BlockSpec((1, A), lambda i: (0, 0))
    return pl.pallas_call(
        functools.partial(_spatial_gate_kernel, groups=G),
        out_shape=[jax.ShapeDtypeStruct((T, A), BF), jax.ShapeDtypeStruct((ROW_TILE, A), F32)],
        grid=(T // ROW_TILE,),
        in_specs=[row, row,
                  pl.BlockSpec((None, G, ROW_TILE, ROW_TILE), lambda i: (i // n_prompt_tiles, 0, 0, 0)),
                  pl.BlockSpec((None, ROW_TILE, A), lambda i: (i // n_prompt_tiles, 0, 0)),
                  vec, vec],
        out_specs=[row, pl.BlockSpec((ROW_TILE, A), lambda i: (0, 0))],
        compiler_params=_params(1),
        name="spatial_gate",
    )(vpre, u, ws_tiles, b_tiles, v_g.reshape(1, A), v_b.reshape(1, A))


def _kv_kernel(x_ref, wkv_ref, kvg_ref, cos_ref, sin_ref, wuk_ref, wuv_ref,
               ckv_ref, kpe_ref, kfull_ref, v_ref, *, c_dim, r_dim, heads, nope):
    y = jnp.dot(x_ref[...], wkv_ref[...], preferred_element_type=F32)
    c = y[:, :c_dim]
    ckv = c * lax.rsqrt(jnp.mean(c * c, axis=-1, keepdims=True) + RMS_EPS) * kvg_ref[...]
    ckv_ref[...] = ckv
    kpe = y[:, c_dim:c_dim + r_dim] * cos_ref[...] + y[:, c_dim + 128:c_dim + 128 + r_dim] * sin_ref[...]
    kpe_ref[...] = kpe
    cb = ckv.astype(BF)
    kn = jnp.dot(cb, wuk_ref[...], preferred_element_type=F32)
    v_ref[...] = jnp.dot(cb, wuv_ref[...], preferred_element_type=F32).astype(BF)
    kpe_pad = jnp.concatenate([kpe, jnp.zeros((kpe.shape[0], 128 - r_dim), F32)], axis=-1).astype(BF)
    for h in range(heads):
        kfull_ref[:, h * 256:h * 256 + nope] = kn[:, h * nope:(h + 1) * nope].astype(BF)
        kfull_ref[:, h * 256 + nope:(h + 1) * 256] = kpe_pad


def _kv_side(xb, wkv, kv_g, cos_k, sin_k, wuk, wuv, *, bm, c_dim, r_dim, heads, nope):
    T, D = xb.shape
    assert nope == 128 and r_dim <= 128
    row = lambda w: pl.BlockSpec((bm, w), lambda i: (i, 0))
    full = lambda a: pl.BlockSpec(a.shape, lambda i: (0,) * a.ndim)
    return pl.pallas_call(
        functools.partial(_kv_kernel, c_dim=c_dim, r_dim=r_dim, heads=heads, nope=nope),
        out_shape=[jax.ShapeDtypeStruct((T, c_dim), F32), jax.ShapeDtypeStruct((T, r_dim), F32),
                   jax.ShapeDtypeStruct((T, heads * 256), BF), jax.ShapeDtypeStruct((T, wuv.shape[1]), BF)],
        grid=(T // bm,),
        in_specs=[row(D), full(wkv), full(kv_g), row(r_dim), row(r_dim), full(wuk), full(wuv)],
        out_specs=[row(c_dim), row(r_dim), row(heads * 256), row(wuv.shape[1])],
        compiler_params=_params(1),
        name="kv_side",
    )(xb, wkv, kv_g, cos_k, sin_k, wuk, wuv)


def _q_kernel(h_ref, wdq_ref, qg_ref, wuq_ref, wpe_ref, wrot_ref, cos_ref, sin_ref, q_ref,
              *, heads, nope, scale):
    c = jnp.dot(h_ref[...], wdq_ref[...], preferred_element_type=F32)
    cq = (c * lax.rsqrt(jnp.mean(c * c, axis=-1, keepdims=True) + RMS_EPS) * qg_ref[...]).astype(BF)
    qn = jnp.dot(cq, wuq_ref[...], preferred_element_type=F32) * scale
    qp = jnp.dot(cq, wpe_ref[...], preferred_element_type=F32)
    qr = jnp.dot(cq, wrot_ref[...], preferred_element_type=F32)
    cos = cos_ref[...]
    sin = sin_ref[...]
    for h in range(heads):
        q_ref[:, h * 256:h * 256 + nope] = qn[:, h * nope:(h + 1) * nope].astype(BF)
        pe = qp[:, h * 128:(h + 1) * 128] * cos + qr[:, h * 128:(h + 1) * 128] * sin
        q_ref[:, h * 256 + nope:(h + 1) * 256] = pe.astype(BF)


def _q_side(hb, wdq, q_g, wuq, wpe, wrot, cos_q, sin_q, *, bm, heads, nope, scale):
    T, D = hb.shape
    assert nope == 128
    row = lambda w: pl.BlockSpec((bm, w), lambda i: (i, 0))
    full = lambda a: pl.BlockSpec(a.shape, lambda i: (0,) * a.ndim)
    return pl.pallas_call(
        functools.partial(_q_kernel, heads=heads, nope=nope, scale=scale),
        out_shape=jax.ShapeDtypeStruct((T, heads * 256), BF),
        grid=(T // bm,),
        in_specs=[row(D), full(wdq), full(q_g), full(wuq), full(wpe), full(wrot), row(128), row(128)],
        out_specs=row(heads * 256),
        compiler_params=_params(1),
        name="q_side",
    )(hb, wdq, q_g, wuq, wpe, wrot, cos_q, sin_q)


def _flash_kernel(qi_ref, ki_ref, q_ref, k_ref, v_ref, o_ref, m_ref, l_ref, acc_ref):
    step = pl.program_id(2)
    qi = qi_ref[step]
    ki = ki_ref[step]

    @pl.when(ki == 0)
    def _():
        m_ref[...] = jnp.full(m_ref.shape, NEG_INF, F32)
        l_ref[...] = jnp.zeros(l_ref.shape, F32)
        acc_ref[...] = jnp.zeros(acc_ref.shape, F32)

    def update(masked):
        s = lax.dot_general(q_ref[...], k_ref[...], (((1,), (1,)), ((), ())), preferred_element_type=F32)
        if masked:
            row = lax.broadcasted_iota(jnp.int32, s.shape, 0)
            col = lax.broadcasted_iota(jnp.int32, s.shape, 1)
            s = jnp.where(col <= row, s, NEG_INF)
        m_prev = m_ref[...]
        m_new = jnp.maximum(m_prev, jnp.max(s, axis=-1, keepdims=True))
        corr = jnp.exp(m_prev - m_new)
        p = jnp.exp(s - m_new)
        l_ref[...] = corr * l_ref[...] + jnp.sum(p, axis=-1, keepdims=True)
        acc_ref[...] = corr * acc_ref[...] + jnp.dot(p.astype(BF), v_ref[...], preferred_element_type=F32)
        m_ref[...] = m_new

    @pl.when(ki < qi)
    def _():
        update(False)

    @pl.when(ki == qi)
    def _():
        update(True)
        o_ref[...] = (acc_ref[...] / l_ref[...]).astype(o_ref.dtype)


def _flash_attention(q, kfull, v, *, batch, seq, heads, vdim, blk, total_rows):
    nb = seq // blk
    pairs = [(a, b) for a in range(nb) for b in range(a + 1)]
    qi_map = jnp.asarray([p[0] for p in pairs], jnp.int32)
    ki_map = jnp.asarray([p[1] for p in pairs], jnp.int32)
    grid_spec = pltpu.PrefetchScalarGridSpec(
        num_scalar_prefetch=2,
        grid=(batch, heads, len(pairs)),
        in_specs=[
            pl.BlockSpec((blk, 256), lambda b, h, s, qi, ki: (b * nb + qi[s], h)),
            pl.BlockSpec((blk, 256), lambda b, h, s, qi, ki: (b * nb + ki[s], h)),
            pl.BlockSpec((blk, vdim), lambda b, h, s, qi, ki: (b * nb + ki[s], h)),
        ],
        out_specs=pl.BlockSpec((blk, vdim), lambda b, h, s, qi, ki: (b * nb + qi[s], h)),
        scratch_shapes=[pltpu.VMEM((blk, 1), F32), pltpu.VMEM((blk, 1), F32), pltpu.VMEM((blk, vdim), F32)],
    )
    return pl.pallas_call(
        _flash_kernel,
        out_shape=jax.ShapeDtypeStruct((total_rows, heads * vdim), BF),
        grid_spec=grid_spec,
        compiler_params=_params(3),
        name="flash_prompt",
    )(qi_map, ki_map, q, kfull, v)


def _absorb_kernel(q_ref, wuk_ref, o_ref):
    o_ref[...] = lax.dot_general(q_ref[...], wuk_ref[...], (((1,), (1,)), ((), ())),
                                 preferred_element_type=F32).astype(o_ref.dtype)


def _absorb_query(q, wuk, *, heads, nope, c_dim, row_block, rows):
    return pl.pallas_call(
        _absorb_kernel,
        out_shape=jax.ShapeDtypeStruct((rows, heads * c_dim), BF),
        grid=(heads,),
        in_specs=[pl.BlockSpec((rows, nope), lambda h: (row_block, 2 * h)),
                  pl.BlockSpec((c_dim, nope), lambda h: (0, h))],
        out_specs=pl.BlockSpec((rows, c_dim), lambda h: (0, h)),
        compiler_params=_params(1),
        name="absorb_query",
    )(q, wuk)


def _decode_kernel(pt_ref, qlat_ref, qpe_ref, cnew_ref, knew_ref, *rest, pages):
    ck_refs = rest[:pages]
    kp_refs = rest[pages:2 * pages]
    o_ref, m_ref, l_ref, acc_ref = rest[2 * pages:]
    j = pl.program_id(1)

    @pl.when(j == 0)
    def _():
        m_ref[...] = jnp.full(m_ref.shape, NEG_INF, F32)
        l_ref[...] = jnp.zeros(l_ref.shape, F32)
        acc_ref[...] = jnp.zeros(acc_ref.shape, F32)

    ql = qlat_ref[0]
    qp = qpe_ref[0]
    contract_last = (((1,), (1,)), ((), ()))
    cks = []
    scores = []
    for i in range(pages):
        ck = ck_refs[i][...].astype(BF)
        kp = kp_refs[i][...].astype(BF)
        cks.append(ck)
        scores.append(lax.dot_general(ql, ck, contract_last, preferred_element_type=F32)
                      + lax.dot_general(qp, kp, contract_last, preferred_element_type=F32))
    s = jnp.concatenate(scores, axis=-1)
    m_prev = m_ref[...]
    m_new = jnp.maximum(m_prev, jnp.max(s, axis=-1, keepdims=True))
    corr = jnp.exp(m_prev - m_new)
    p = jnp.exp(s - m_new)
    l_new = corr * l_ref[...] + jnp.sum(p, axis=-1, keepdims=True)
    pb = p.astype(BF)
    rows = cks[0].shape[0]
    pv = jnp.dot(pb[:, :rows], cks[0], preferred_element_type=F32)
    for i in range(1, pages):
        pv = pv + jnp.dot(pb[:, i * rows:(i + 1) * rows], cks[i], preferred_element_type=F32)
    acc_new = corr * acc_ref[...] + pv
    m_ref[...] = m_new
    l_ref[...] = l_new
    acc_ref[...] = acc_new

    @pl.when(j == pl.num_programs(1) - 1)
    def _():
        cn = cnew_ref[0].astype(BF).astype(F32)
        kn = knew_ref[0].astype(BF).astype(F32)
        s_self = (jnp.sum(ql.astype(F32) * cn, axis=-1, keepdims=True)
                  + jnp.sum(qp.astype(F32) * kn, axis=-1, keepdims=True))
        m_fin = jnp.maximum(m_new, s_self)
        c2 = jnp.exp(m_new - m_fin)
        p_self = jnp.exp(s_self - m_fin)
        l_fin = c2 * l_new + p_self
        acc_fin = c2 * acc_new + p_self.astype(BF).astype(F32) * cn
        o_ref[0] = (acc_fin / l_fin).astype(o_ref.dtype)


def _decode_attention(page_table, q_lat, q_pe, ckv_new, kpe_new, cache_ckv, cache_kpe, *, pages):
    Bd, H, C = q_lat.shape
    R = q_pe.shape[2]
    n_pages = page_table.shape[1]
    page_rows = cache_ckv.shape[1]
    assert n_pages % pages == 0
    per_b = lambda w: pl.BlockSpec((1, H, w), lambda b, j, pt: (b, 0, 0))
    new_row = lambda w: pl.BlockSpec((1, 1, w), lambda b, j, pt: (b, 0, 0))

    def page_spec(width, i):
        return pl.BlockSpec((None, page_rows, width), lambda b, j, pt: (pt[b * n_pages + j * pages + i], 0, 0))

    grid_spec = pltpu.PrefetchScalarGridSpec(
        num_scalar_prefetch=1,
        grid=(Bd, n_pages // pages),
        in_specs=[per_b(C), per_b(R), new_row(C), new_row(R)]
        + [page_spec(C, i) for i in range(pages)] + [page_spec(R, i) for i in range(pages)],
        out_specs=pl.BlockSpec((1, H, C), lambda b, j, pt: (b, 0, 0)),
        scratch_shapes=[pltpu.VMEM((H, 1), F32), pltpu.VMEM((H, 1), F32), pltpu.VMEM((H, C), F32)],
    )
    return pl.pallas_call(
        functools.partial(_decode_kernel, pages=pages),
        out_shape=jax.ShapeDtypeStruct((Bd, H, C), BF),
        grid_spec=grid_spec,
        compiler_params=_params(2),
        name="decode_attention",
    )(page_table.reshape(-1), q_lat, q_pe, ckv_new, kpe_new,
      *([cache_ckv] * pages), *([cache_kpe] * pages))


def _value_up_kernel(o_hbm_ref, olat_ref, wuv_ref, o_ref):
    del o_hbm_ref
    o_ref[...] = jnp.dot(olat_ref[...], wuv_ref[...], preferred_element_type=F32).astype(o_ref.dtype)


def _value_up_into(o_all, o_lat, wuv, *, heads, c_dim, vdim, row_block, rows):
    return pl.pallas_call(
        _value_up_kernel,
        out_shape=jax.ShapeDtypeStruct(o_all.shape, o_all.dtype),
        grid=(heads,),
        in_specs=[pl.BlockSpec(memory_space=pl.ANY),
                  pl.BlockSpec((rows, c_dim), lambda h: (0, h)),
                  pl.BlockSpec((c_dim, vdim), lambda h: (0, h))],
        out_specs=pl.BlockSpec((rows, vdim), lambda h: (row_block, h)),
        input_output_aliases={0: 0},
        compiler_params=_params(1),
        name="value_up",
    )(o_all, o_lat, wuv)


def _router_kernel(x_ref, sc_ref, sh_ref, wr_ref, br_ref, g_ref, *, n_exp):
    h = x_ref[...] * (1.0 + sc_ref[...]) + sh_ref[...]
    logits = jnp.dot(h, wr_ref[...], preferred_element_type=F32, precision=lax.Precision.HIGHEST) + br_ref[...]
    lane = lax.broadcasted_iota(jnp.int32, logits.shape, 1)
    valid = lane < n_exp
    logits = jnp.where(valid, logits, NEG_INF)
    mx = jnp.max(logits, axis=-1, keepdims=True)
    e = jnp.exp(logits - mx)
    probs = e / jnp.sum(e, axis=-1, keepdims=True)
    big = jnp.int32(1 << 30)
    p1 = jnp.max(probs, axis=-1, keepdims=True)
    i1 = jnp.min(jnp.where((probs == p1) & valid, lane, big), axis=-1, keepdims=True)
    rest = jnp.where((lane == i1) | ~valid, -1.0, probs)
    p2 = jnp.max(rest, axis=-1, keepdims=True)
    i2 = jnp.min(jnp.where(rest == p2, lane, big), axis=-1, keepdims=True)
    denom = p1 + p2
    g_ref[...] = jnp.where(lane == i1, p1 / denom, jnp.where(lane == i2, p2 / denom, 0.0))


def _router(x, mod, sc_chunk, sh_chunk, wr_pad, br_pad, tiles_per_batch, n_exp):
    T, D = x.shape
    row = pl.BlockSpec((ROW_TILE, D), lambda i: (i, 0))
    return pl.pallas_call(
        functools.partial(_router_kernel, n_exp=n_exp),
        out_shape=jax.ShapeDtypeStruct((T, 128), F32),
        grid=(T // ROW_TILE,),
        in_specs=[row, _mod_spec(tiles_per_batch, D, sc_chunk), _mod_spec(tiles_per_batch, D, sh_chunk),
                  pl.BlockSpec((D, 128), lambda i: (0, 0)), pl.BlockSpec((1, 128), lambda i: (0, 0))],
        out_specs=pl.BlockSpec((ROW_TILE, 128), lambda i: (i, 0)),
        compiler_params=_params(1),
        name="router",
    )(x, mod, mod, wr_pad, br_pad)


def _pick_bm(T, target):
    best = ROW_TILE
    for m in range(ROW_TILE, target + 1, ROW_TILE):
        if T % m == 0:
            best = m
    return best


def kernel(x_prompt, x_sample, cache_ckv, cache_kpe, page_table, c_prompt, c_sample, w_ada, b_ada, ln_g, ln_b, a_w_in, a_v_g, a_v_b, a_w_s, a_b_s, a_w_out, b_w_dq, b_q_g, b_w_uq, b_w_qr, b_w_o, kv_w_dkv, kv_g, kv_w_kr, kv_w_uk, kv_w_uv, ffn_w_gate, ffn_w_up, ffn_w_down, moe_w_router, moe_b_router, moe_w_gate, moe_w_up, moe_w_down):
    B, S, D = x_prompt.shape
    Bd, Td, _ = x_sample.shape
    depth = w_ada.shape[0]
    n_a = a_w_in.shape[0]
    assert depth == 2 and n_a == 1 and Td == 1 and Bd == ROW_TILE and S % ROW_TILE == 0
    A = a_w_in.shape[2] // 2
    G = a_w_s.shape[1]
    H, nope = b_w_uq.shape[2], b_w_uq.shape[3]
    R = b_w_qr.shape[3]
    C = kv_w_dkv.shape[1]
    vdim = kv_w_uv.shape[2]
    E = moe_w_router.shape[2]
    n_pages, page_rows = page_table.shape[1], cache_ckv.shape[1]
    past_len = n_pages * page_rows
    alpha = (2.0 * depth) ** 0.25
    scale = (nope + R) ** -0.5
    Tp = B * S
    T = Tp + Bd
    tiles_per_batch = S // ROW_TILE
    n_prompt_tiles = Tp // ROW_TILE
    bm = _pick_bm(T, 640)

    x = jnp.concatenate([x_prompt.reshape(Tp, D), x_sample.reshape(Bd, D)], axis=0)

    c_tiles = jnp.concatenate([jnp.repeat(c_prompt, ROW_TILE, axis=0), c_sample], axis=0)
    mods = [_matmul(c_tiles, w_ada, bm=c_tiles.shape[0], bn=512, out_dtype=F32, act="silu_in",
                    bias=b_ada[l].reshape(1, -1), lead=(l,), name="adaln") for l in range(depth)]
    SH_M, SC_M, G_M, SH_F, SC_F, G_F = range(6)

    half = R // 2
    inv = ROPE_BASE ** (-jnp.arange(half, dtype=F32) / half)
    pos = jnp.concatenate([jnp.tile(jnp.arange(S, dtype=jnp.int32), B),
                           jnp.full((Bd,), past_len, jnp.int32)]).astype(F32)
    ang = pos[:, None] * inv[None, :]
    cos2 = jnp.concatenate([jnp.cos(ang), jnp.cos(ang)], axis=-1)
    sin2 = jnp.concatenate([jnp.sin(ang), jnp.sin(ang)], axis=-1)
    lane_pad = jnp.zeros((T, 128 - R), F32)
    cos_q = jnp.concatenate([cos2 * scale, lane_pad], axis=-1)
    sin_q = jnp.concatenate([sin2 * scale, lane_pad], axis=-1)

    h = _modulate(x, mods[0], SC_M, SH_M, tiles_per_batch)
    u = _matmul(h, a_w_in, bm=bm, bn=512, out_dtype=BF, n_off=0, n_out=A, act="gelu", lead=(0,), name="a_in_u")
    vpre = _matmul(h, a_w_in, bm=bm, bn=512, out_dtype=F32, n_off=A, n_out=A, act="gelu", lead=(0,), name="a_in_v")
    tril = jnp.tril(jnp.ones((ROW_TILE, ROW_TILE), bool))
    ws_prompt = jnp.where(tril, a_w_s[0], 0.0)
    ws_sample = a_w_s[0][:, :1, :1] * jnp.eye(ROW_TILE, dtype=F32)
    ws_tiles = jnp.stack([ws_prompt, ws_sample]).astype(BF)
    gd = A // G
    b_prompt = jnp.repeat(a_b_s[0].T, gd, axis=1)
    b_sample = jnp.broadcast_to(b_prompt[:1], b_prompt.shape)
    b_tiles = jnp.stack([b_prompt, b_sample])
    gated, v_sample = _spatial_gate(vpre, u, ws_tiles, b_tiles, a_v_g[0], a_v_b[0], n_prompt_tiles)
    mix = _matmul(gated, a_w_out, bm=bm, bn=512, out_dtype=F32, lead=(0,), name="a_out")
    x, h = _postln(x, mix, mods[0], G_M, ln_g[0, 0], ln_b[0, 0], tiles_per_batch, alpha=alpha,
                   mod_next=mods[0], sc_chunk=SC_F, sh_chunk=SH_F)
    act = _swiglu_up(h, ffn_w_gate, ffn_w_up, bm=bm, bn=512, lead=(0,), name="ffn_up")
    f = _matmul(act, ffn_w_down, bm=bm, bn=256, out_dtype=F32, lead=(0,), name="ffn_down")
    x, h, xb = _postln(x, f, mods[0], G_F, ln_g[0, 1], ln_b[0, 1], tiles_per_batch, alpha=alpha,
                       mod_next=mods[1], sc_chunk=SC_M, sh_chunk=SH_M, emit_bf=True)

    w1, w2 = kv_w_kr[:, :half], kv_w_kr[:, half:]
    zpad = jnp.zeros((D, 128 - R), F32)
    wkv = jnp.concatenate([kv_w_dkv, kv_w_kr, zpad, -w2, w1, zpad], axis=1).astype(BF)
    wuk2 = kv_w_uk.reshape(C, H * nope)
    wuv2 = kv_w_uv.reshape(C, H * vdim)
    ckv, kpe, kfull, vfull = _kv_side(xb, wkv, kv_g.reshape(1, C), cos2, sin2, wuk2.astype(BF), wuv2.astype(BF),
                                      bm=bm, c_dim=C, r_dim=R, heads=H, nope=nope)

    wqr = b_w_qr[0]
    qpad = jnp.zeros(wqr.shape[:2] + (128 - R,), F32)
    wpe = jnp.concatenate([wqr, qpad], axis=-1).reshape(wqr.shape[0], H * 128)
    wrot = jnp.concatenate([-wqr[..., half:], wqr[..., :half], qpad], axis=-1).reshape(wqr.shape[0], H * 128)
    q = _q_side(h, b_w_dq[0].astype(BF), b_q_g[0].reshape(1, -1), b_w_uq[0].reshape(-1, H * nope).astype(BF),
                wpe.astype(BF), wrot.astype(BF), cos_q, sin_q, bm=bm, heads=H, nope=nope, scale=scale)
    o_all = _flash_attention(q, kfull, vfull, batch=B, seq=S, heads=H, vdim=vdim, blk=min(512, S), total_rows=T)

    q_lat = _absorb_query(q, wuk2.astype(BF), heads=H, nope=nope, c_dim=C,
                          row_block=Tp // Bd, rows=Bd).reshape(Bd, H, C)
    q_pe_s = q[Tp:].reshape(Bd, H, 256)[:, :, nope:nope + R]
    o_lat = _decode_attention(page_table, q_lat, q_pe_s, ckv[Tp:].reshape(Bd, 1, C), kpe[Tp:].reshape(Bd, 1, R),
                              cache_ckv, cache_kpe, pages=min(16, n_pages))
    o_all = _value_up_into(o_all, o_lat.reshape(Bd, H * C), wuv2.astype(BF), heads=H, c_dim=C, vdim=vdim,
                           row_block=Tp // Bd, rows=Bd)
    mix = _matmul(o_all, b_w_o, bm=bm, bn=512, out_dtype=F32, lead=(0,), name="attn_out")
    x = _postln(x, mix, mods[1], G_M, ln_g[1, 0], ln_b[1, 0], tiles_per_batch, alpha=alpha)[0]

    wr_pad = jnp.concatenate([moe_w_router[0], jnp.zeros((D, 128 - E), F32)], axis=1)
    br_pad = jnp.concatenate([moe_b_router[0], jnp.zeros((128 - E,), F32)]).reshape(1, 128)
    gates = _router(x, mods[1], SC_F, SH_F, wr_pad, br_pad, tiles_per_batch, E)
    h = _modulate(x, mods[1], SC_F, SH_F, tiles_per_batch)
    f = None
    for e in range(E):
        act = _swiglu_up(h, moe_w_gate, moe_w_up, bm=bm, bn=512, lead=(0, e), gates=gates, gate_col=e,
                         name="moe_up")
        f = _matmul(act, moe_w_down, bm=bm, bn=256, out_dtype=F32, lead=(0, e), add=f, name="moe_down")
    x = _postln(x, f, mods[1], G_F, ln_g[1, 1], ln_b[1, 1], tiles_per_batch, alpha=alpha)[0]

    y_prompt = x[:Tp].reshape(B, S, D)
    y_sample = x[Tp:].reshape(Bd, Td, D)
    return (y_prompt, y_sample,
            ckv[:Tp].reshape(B, S, C), kpe[:Tp].reshape(B, S, R),
            ckv[Tp:].reshape(Bd, Td, C), kpe[Tp:].reshape(Bd, Td, R),
            v_sample.reshape(n_a, Bd, Td, A))
```

```python
import functools
import math

import jax
import jax.numpy as jnp
from jax import lax
from jax.experimental import pallas as pl
from jax.experimental.pallas import tpu as pltpu

BF = jnp.bfloat16
F32 = jnp.float32

ROW_TILE = 128
LN_EPS = 1e-5
RMS_EPS = 1e-6
ROPE_BASE = 10000.0
NEG_INF = float("-inf")
VMEM_LIMIT = 56 * 1024 * 1024


def _params(n_axes, vmem=VMEM_LIMIT):
    return pltpu.CompilerParams(dimension_semantics=("arbitrary",) * n_axes, vmem_limit_bytes=vmem)


def _gelu_tanh(x):
    return 0.5 * x * (1.0 + jnp.tanh(math.sqrt(2.0 / math.pi) * (x + 0.044715 * x * x * x)))


def _silu(x):
    return x * (1.0 / (1.0 + jnp.exp(-x)))


def _layernorm_rows(x, g, b):
    mu = jnp.mean(x, axis=-1, keepdims=True)
    xc = x - mu
    var = jnp.mean(xc * xc, axis=-1, keepdims=True)
    return xc * lax.rsqrt(var + LN_EPS) * g + b


def _mm_kernel(*refs, act, has_bias, has_add):
    x_ref, w_ref = refs[0], refs[1]
    k = 2
    bias_ref = add_ref = None
    if has_bias:
        bias_ref = refs[k]; k += 1
    if has_add:
        add_ref = refs[k]; k += 1
    o_ref, wbf_ref = refs[k], refs[k + 1]

    @pl.when(pl.program_id(1) == 0)
    def _():
        wbf_ref[...] = w_ref[...].astype(BF)

    x = x_ref[...]
    if act == "silu_in":
        x = _silu(x.astype(F32))
    acc = jnp.dot(x.astype(BF), wbf_ref[...], preferred_element_type=F32)
    if has_bias:
        acc = acc + bias_ref[...]
    if act == "gelu":
        acc = _gelu_tanh(acc)
    if has_add:
        acc = acc + add_ref[...]
    o_ref[...] = acc.astype(o_ref.dtype)


def _matmul(x, w, *, bm, bn, out_dtype, n_off=0, n_out=None, act=None, bias=None, add=None,
            lead=None, name="mm"):
    M, K = x.shape
    N = w.shape[-1] if n_out is None else n_out
    bn = min(bn, N)
    assert M % bm == 0 and N % bn == 0 and n_off % bn == 0
    offb = n_off // bn
    if lead is None:
        w_spec = pl.BlockSpec((K, bn), lambda j, i: (0, j + offb))
    else:
        lead_idx = tuple(lead)
        w_spec = pl.BlockSpec((None,) * len(lead_idx) + (K, bn), lambda j, i: lead_idx + (0, j + offb))
    in_specs = [pl.BlockSpec((bm, K), lambda j, i: (i, 0)), w_spec]
    args = [x, w]
    if bias is not None:
        in_specs.append(pl.BlockSpec((1, bn), lambda j, i: (0, j + offb)))
        args.append(bias)
    if add is not None:
        in_specs.append(pl.BlockSpec((bm, bn), lambda j, i: (i, j)))
        args.append(add)
    return pl.pallas_call(
        functools.partial(_mm_kernel, act=act, has_bias=bias is not None, has_add=add is not None),
        out_shape=jax.ShapeDtypeStruct((M, N), out_dtype),
        grid=(N // bn, M // bm),
        in_specs=in_specs,
        out_specs=pl.BlockSpec((bm, bn), lambda j, i: (i, j)),
        scratch_shapes=[pltpu.VMEM((K, bn), BF)],
        compiler_params=_params(2),
        name=name,
    )(*args)


def _swiglu_up_kernel(x_ref, wg_ref, wu_ref, o_ref, wgb_ref, wub_ref):
    @pl.when(pl.program_id(1) == 0)
    def _():
        wgb_ref[...] = wg_ref[...].astype(BF)
        wub_ref[...] = wu_ref[...].astype(BF)

    x = x_ref[...].astype(BF)
    g = jnp.dot(x, wgb_ref[...], preferred_element_type=F32)
    u = jnp.dot(x, wub_ref[...], preferred_element_type=F32)
    o_ref[...] = (_silu(g) * u).astype(o_ref.dtype)


def _swiglu_up(x, wg, wu, *, bm, bn, lead, name="swiglu_up"):
    M, K = x.shape
    N = wg.shape[-1]
    bn = min(bn, N)
    assert M % bm == 0 and N % bn == 0
    lead_idx = tuple(lead)
    w_spec = pl.BlockSpec((None,) * len(lead_idx) + (K, bn), lambda j, i: lead_idx + (0, j))
    return pl.pallas_call(
        _swiglu_up_kernel,
        out_shape=jax.ShapeDtypeStruct((M, N), BF),
        grid=(N // bn, M // bm),
        in_specs=[pl.BlockSpec((bm, K), lambda j, i: (i, 0)), w_spec, w_spec],
        out_specs=pl.BlockSpec((bm, bn), lambda j, i: (i, j)),
        scratch_shapes=[pltpu.VMEM((K, bn), BF), pltpu.VMEM((K, bn), BF)],
        compiler_params=_params(2),
        name=name,
    )(x, wg, wu)


def _mod_spec(tiles_per_batch, D, chunk):
    return pl.BlockSpec((ROW_TILE, D), lambda i: (i // tiles_per_batch, chunk))


def _modulate_kernel(x_ref, sc_ref, sh_ref, h_ref):
    h_ref[...] = (x_ref[...] * (1.0 + sc_ref[...]) + sh_ref[...]).astype(h_ref.dtype)


def _modulate(x, mod, sc_chunk, sh_chunk, tiles_per_batch):
    T, D = x.shape
    row = pl.BlockSpec((ROW_TILE, D), lambda i: (i, 0))
    return pl.pallas_call(
        _modulate_kernel,
        out_shape=jax.ShapeDtypeStruct((T, D), BF),
        grid=(T // ROW_TILE,),
        in_specs=[row, _mod_spec(tiles_per_batch, D, sc_chunk), _mod_spec(tiles_per_batch, D, sh_chunk)],
        out_specs=row,
        compiler_params=_params(1),
        name="modulate",
    )(x, mod, mod)


def _postln_kernel(*refs, alpha, modulate_next, emit_bf):
    x_ref, mix_ref, g_ref, lng_ref, lnb_ref = refs[:5]
    k = 5
    if modulate_next:
        sc_ref, sh_ref = refs[k], refs[k + 1]
        k += 2
    xo_ref = refs[k]; k += 1
    xn = _layernorm_rows(alpha * x_ref[...] + g_ref[...] * mix_ref[...], lng_ref[...], lnb_ref[...])
    xo_ref[...] = xn
    if modulate_next:
        refs[k][...] = (xn * (1.0 + sc_ref[...]) + sh_ref[...]).astype(BF)
        k += 1
    if emit_bf:
        refs[k][...] = xn.astype(BF)


def _postln(x, mix, mod_g, g_chunk, ln_g, ln_b, tiles_per_batch, *, alpha, mod_next=None,
            sc_chunk=None, sh_chunk=None, emit_bf=False):
    T, D = x.shape
    row = pl.BlockSpec((ROW_TILE, D), lambda i: (i, 0))
    vec = pl.BlockSpec((1, D), lambda i: (0, 0))
    in_specs = [row, row, _mod_spec(tiles_per_batch, D, g_chunk), vec, vec]
    args = [x, mix, mod_g, ln_g.reshape(1, D), ln_b.reshape(1, D)]
    out_shape = [jax.ShapeDtypeStruct((T, D), F32)]
    out_specs = [row]
    if mod_next is not None:
        in_specs += [_mod_spec(tiles_per_batch, D, sc_chunk), _mod_spec(tiles_per_batch, D, sh_chunk)]
        args += [mod_next, mod_next]
        out_shape.append(jax.ShapeDtypeStruct((T, D), BF))
        out_specs.append(row)
    if emit_bf:
        out_shape.append(jax.ShapeDtypeStruct((T, D), BF))
        out_specs.append(row)
    return pl.pallas_call(
        functools.partial(_postln_kernel, alpha=alpha, modulate_next=mod_next is not None, emit_bf=emit_bf),
        out_shape=out_shape,
        grid=(T // ROW_TILE,),
        in_specs=in_specs,
        out_specs=out_specs,
        compiler_params=_params(1),
        name="postln",
    )(*args)


def _spatial_gate_kernel(vpre_ref, u_ref, ws_ref, b_ref, vg_ref, vb_ref, mix_ref, vout_ref, *, groups):
    v = _layernorm_rows(vpre_ref[...], vg_ref[...], vb_ref[...])

    @pl.when(pl.program_id(0) == pl.num_programs(0) - 1)
    def _():
        vout_ref[...] = v

    vb16 = v.astype(BF)
    gd = v.shape[1] // groups
    for g in range(groups):
        cols = slice(g * gd, (g + 1) * gd)
        s = jnp.dot(ws_ref[g], vb16[:, cols], preferred_element_type=F32) + b_ref[:, cols]
        mix_ref[:, cols] = (u_ref[:, cols].astype(F32) * s).astype(BF)


def _spatial_gate(vpre, u, ws_tiles, b_tiles, v_g, v_b, n_prompt_tiles):
    T, A = vpre.shape
    G = ws_tiles.shape[1]
    row = pl.BlockSpec((ROW_TILE, A), lambda i: (i, 0))
    vec = pl.BlockSpec((1, A), lambda i: (0, 0))
    return pl.pallas_call(
        functools.partial(_spatial_gate_kernel, groups=G),
        out_shape=[jax.ShapeDtypeStruct((T, A), BF), jax.ShapeDtypeStruct((ROW_TILE, A), F32)],
        grid=(T // ROW_TILE,),
        in_specs=[row, row,
                  pl.BlockSpec((None, G, ROW_TILE, ROW_TILE), lambda i: (i // n_prompt_tiles, 0, 0, 0)),
                  pl.BlockSpec((None, ROW_TILE, A), lambda i: (i // n_prompt_tiles, 0, 0)),
                  vec, vec],
        out_specs=[row, pl.BlockSpec((ROW_TILE, A), lambda i: (0, 0))],
        compiler_params=_params(1),
        name="spatial_gate",
    )(vpre, u, ws_tiles, b_tiles, v_g.reshape(1, A), v_b.reshape(1, A))


def _kv_kernel(x_ref, wkv_ref, kvg_ref, cos_ref, sin_ref, wuk_ref, wuv_ref,
               ckv_ref, kpe_ref, kfull_ref, v_ref, *, c_dim, r_dim, heads, nope):
    y = jnp.dot(x_ref[...], wkv_ref[...], preferred_element_type=F32)
    c = y[:, :c_dim]
    ckv = c * lax.rsqrt(jnp.mean(c * c, axis=-1, keepdims=True) + RMS_EPS) * kvg_ref[...]
    ckv_ref[...] = ckv
    kpe = y[:, c_dim:c_dim + r_dim] * cos_ref[...] + y[:, c_dim + 128:c_dim + 128 + r_dim] * sin_ref[...]
    kpe_ref[...] = kpe
    cb = ckv.astype(BF)
    kn = jnp.dot(cb, wuk_ref[...], preferred_element_type=F32)
    v_ref[...] = jnp.dot(cb, wuv_ref[...], preferred_element_type=F32).astype(BF)
    kpe_pad = jnp.concatenate([kpe, jnp.zeros((kpe.shape[0], 128 - r_dim), F32)], axis=-1).astype(BF)
    for h in range(heads):
        kfull_ref[:, h * 256:h * 256 + nope] = kn[:, h * nope:(h + 1) * nope].astype(BF)
        kfull_ref[:, h * 256 + nope:(h + 1) * 256] = kpe_pad


def _kv_side(xb, wkv, kv_g, cos_k, sin_k, wuk, wuv, *, bm, c_dim, r_dim, heads, nope):
    T, D = xb.shape
    assert nope == 128 and r_dim <= 128
    row = lambda w: pl.BlockSpec((bm, w), lambda i: (i, 0))
    full = lambda a: pl.BlockSpec(a.shape, lambda i: (0,) * a.ndim)
    return pl.pallas_call(
        functools.partial(_kv_kernel, c_dim=c_dim, r_dim=r_dim, heads=heads, nope=nope),
        out_shape=[jax.ShapeDtypeStruct((T, c_dim), F32), jax.ShapeDtypeStruct((T, r_dim), F32),
                   jax.ShapeDtypeStruct((T, heads * 256), BF), jax.ShapeDtypeStruct((T, wuv.shape[1]), BF)],
        grid=(T // bm,),
        in_specs=[row(D), full(wkv), full(kv_g), row(r_dim), row(r_dim), full(wuk), full(wuv)],
        out_specs=[row(c_dim), row(r_dim), row(heads * 256), row(wuv.shape[1])],
        compiler_params=_params(1),
        name="kv_side",
    )(xb, wkv, kv_g, cos_k, sin_k, wuk, wuv)


def _q_kernel(h_ref, wdq_ref, qg_ref, wuq_ref, wpe_ref, wrot_ref, cos_ref, sin_ref, q_ref,
              *, heads, nope, scale):
    c = jnp.dot(h_ref[...], wdq_ref[...], preferred_element_type=F32)
    cq = (c * lax.rsqrt(jnp.mean(c * c, axis=-1, keepdims=True) + RMS_EPS) * qg_ref[...]).astype(BF)
    qn = jnp.dot(cq, wuq_ref[...], preferred_element_type=F32) * scale
    qp = jnp.dot(cq, wpe_ref[...], preferred_element_type=F32)
    qr = jnp.dot(cq, wrot_ref[...], preferred_element_type=F32)
    cos = cos_ref[...]
    sin = sin_ref[...]
    for h in range(heads):
        q_ref[:, h * 256:h * 256 + nope] = qn[:, h * nope:(h + 1) * nope].astype(BF)
        pe = qp[:, h * 128:(h + 1) * 128] * cos + qr[:, h * 128:(h + 1) * 128] * sin
        q_ref[:, h * 256 + nope:(h + 1) * 256] = pe.astype(BF)


def _q_side(hb, wdq, q_g, wuq, wpe, wrot, cos_q, sin_q, *, bm, heads, nope, scale):
    T, D = hb.shape
    assert nope == 128
    row = lambda w: pl.BlockSpec((bm, w), lambda i: (i, 0))
    full = lambda a: pl.BlockSpec(a.shape, lambda i: (0,) * a.ndim)
    return pl.pallas_call(
        functools.partial(_q_kernel, heads=heads, nope=nope, scale=scale),
        out_shape=jax.ShapeDtypeStruct((T, heads * 256), BF),
        grid=(T // bm,),
        in_specs=[row(D), full(wdq), full(q_g), full(wuq), full(wpe), full(wrot), row(128), row(128)],
        out_specs=row(heads * 256),
        compiler_params=_params(1),
        name="q_side",
    )(hb, wdq, q_g, wuq, wpe, wrot, cos_q, sin_q)


def _flash_kernel(qi_ref, ki_ref, q_ref, k_ref, v_ref, o_init_ref, o_ref, m_ref, l_ref, acc_ref, *, hb, vdim):
    del o_init_ref
    step = pl.program_id(2)
    qi = qi_ref[step]
    ki = ki_ref[step]

    @pl.when(ki == 0)
    def _():
        m_ref[...] = jnp.full(m_ref.shape, NEG_INF, F32)
        l_ref[...] = jnp.zeros(l_ref.shape, F32)
        acc_ref[...] = jnp.zeros(acc_ref.shape, F32)

    def update(masked):
        for h in range(hb):
            qc = slice(h * 256, (h + 1) * 256)
            vc = slice(h * vdim, (h + 1) * vdim)
            s = lax.dot_general(q_ref[:, qc], k_ref[:, qc], (((1,), (1,)), ((), ())),
                                preferred_element_type=F32)
            if masked:
                row = lax.broadcasted_iota(jnp.int32, s.shape, 0)
                col = lax.broadcasted_iota(jnp.int32, s.shape, 1)
                s = jnp.where(col <= row, s, NEG_INF)
            m_prev = m_ref[h]
            m_new = jnp.maximum(m_prev, jnp.max(s, axis=-1, keepdims=True))
            corr = jnp.exp(m_prev - m_new)
            p = jnp.exp(s - m_new)
            l_ref[h] = corr * l_ref[h] + jnp.sum(p, axis=-1, keepdims=True)
            acc_ref[:, vc] = corr * acc_ref[:, vc] + jnp.dot(p.astype(BF), v_ref[:, vc],
                                                             preferred_element_type=F32)
            m_ref[h] = m_new

    @pl.when(ki < qi)
    def _():
        update(False)

    @pl.when(ki == qi)
    def _():
        update(True)
        for h in range(hb):
            vc = slice(h * vdim, (h + 1) * vdim)
            o_ref[:, vc] = (acc_ref[:, vc] / l_ref[h]).astype(o_ref.dtype)


def _flash_attention(q, kfull, v, *, batch, seq, heads, vdim, blk, hb, total_rows):
    assert heads % hb == 0
    nb = seq // blk
    pairs = [(a, b) for a in range(nb) for b in range(a + 1)]
    qi_map = jnp.asarray([p[0] for p in pairs], jnp.int32)
    ki_map = jnp.asarray([p[1] for p in pairs], jnp.int32)
    grid_spec = pltpu.PrefetchScalarGridSpec(
        num_scalar_prefetch=2,
        grid=(batch, heads // hb, len(pairs)),
        in_specs=[
            pl.BlockSpec((blk, hb * 256), lambda b, h, s, qi, ki: (b * nb + qi[s], h)),
            pl.BlockSpec((blk, hb * 256), lambda b, h, s, qi, ki: (b * nb + ki[s], h)),
            pl.BlockSpec((blk, hb * vdim), lambda b, h, s, qi, ki: (b * nb + ki[s], h)),
            pl.BlockSpec(memory_space=pl.ANY),
        ],
        out_specs=pl.BlockSpec((blk, hb * vdim), lambda b, h, s, qi, ki: (b * nb + qi[s], h)),
        scratch_shapes=[pltpu.VMEM((hb, blk, 1), F32), pltpu.VMEM((hb, blk, 1), F32),
                        pltpu.VMEM((blk, hb * vdim), F32)],
    )
    return pl.pallas_call(
        functools.partial(_flash_kernel, hb=hb, vdim=vdim),
        out_shape=jax.ShapeDtypeStruct((total_rows, heads * vdim), BF),
        grid_spec=grid_spec,
        input_output_aliases={5: 0},
        compiler_params=_params(3),
        name="flash_prompt",
    )(qi_map, ki_map, q, kfull, v, jnp.zeros((total_rows, heads * vdim), BF))


def _absorb_kernel(q_ref, wuk_ref, o_ref):
    o_ref[...] = lax.dot_general(q_ref[...], wuk_ref[...], (((1,), (1,)), ((), ())),
                                 preferred_element_type=F32).astype(o_ref.dtype)


def _absorb_query(q, wuk, *, heads, nope, c_dim, row_block, rows):
    return pl.pallas_call(
        _absorb_kernel,
        out_shape=jax.ShapeDtypeStruct((rows, heads * c_dim), BF),
        grid=(heads,),
        in_specs=[pl.BlockSpec((rows, nope), lambda h: (row_block, 2 * h)),
                  pl.BlockSpec((c_dim, nope), lambda h: (0, h))],
        out_specs=pl.BlockSpec((rows, c_dim), lambda h: (0, h)),
        compiler_params=_params(1),
        name="absorb_query",
    )(q, wuk)


def _decode_kernel(pt_ref, qlat_ref, qpe_ref, cnew_ref, knew_ref, *rest, pages):
    ck_refs = rest[:pages]
    kp_refs = rest[pages:2 * pages]
    o_ref, m_ref, l_ref, acc_ref = rest[2 * pages:]
    j = pl.program_id(1)

    @pl.when(j == 0)
    def _():
        m_ref[...] = jnp.full(m_ref.shape, NEG_INF, F32)
        l_ref[...] = jnp.zeros(l_ref.shape, F32)
        acc_ref[...] = jnp.zeros(acc_ref.shape, F32)

    ql = qlat_ref[0]
    qp = qpe_ref[0]
    contract_last = (((1,), (1,)), ((), ()))
    cks = []
    scores = []
    for i in range(pages):
        ck = ck_refs[i][...].astype(BF)
        kp = kp_refs[i][...].astype(BF)
        cks.append(ck)
        scores.append(lax.dot_general(ql, ck, contract_last, preferred_element_type=F32)
                      + jnp.dot(qp, kp, preferred_element_type=F32))
    s = jnp.concatenate(scores, axis=-1)
    m_prev = m_ref[...]
    m_new = jnp.maximum(m_prev, jnp.max(s, axis=-1, keepdims=True))
    corr = jnp.exp(m_prev - m_new)
    p = jnp.exp(s - m_new)
    l_new = corr * l_ref[...] + jnp.sum(p, axis=-1, keepdims=True)
    pb = p.astype(BF)
    rows = cks[0].shape[0]
    pv = jnp.dot(pb[:, :rows], cks[0], preferred_element_type=F32)
    for i in range(1, pages):
        pv = pv + jnp.dot(pb[:, i * rows:(i + 1) * rows], cks[i], preferred_element_type=F32)
    acc_new = corr * acc_ref[...] + pv
    m_ref[...] = m_new
    l_ref[...] = l_new
    acc_ref[...] = acc_new

    @pl.when(j == pl.num_programs(1) - 1)
    def _():
        cn = cnew_ref[0].astype(BF).astype(F32)
        kn = knew_ref[0].astype(BF).astype(F32)
        s_self = (jnp.sum(ql.astype(F32) * cn, axis=-1, keepdims=True)
                  + jnp.sum(qp.astype(F32) * kn, axis=-1, keepdims=True))
        m_fin = jnp.maximum(m_new, s_self)
        c2 = jnp.exp(m_new - m_fin)
        p_self = jnp.exp(s_self - m_fin)
        l_fin = c2 * l_new + p_self
        acc_fin = c2 * acc_new + p_self.astype(BF).astype(F32) * cn
        o_ref[0] = (acc_fin / l_fin).astype(o_ref.dtype)


def _decode_attention(page_table, q_lat, q_pe, ckv_new, kpe_new, cache_ckv, cache_kpe_t, *, pages):
    Bd, H, C = q_lat.shape
    R = q_pe.shape[2]
    n_pages = page_table.shape[1]
    page_rows = cache_ckv.shape[1]
    assert n_pages % pages == 0
    per_b = lambda w: pl.BlockSpec((1, H, w), lambda b, j, pt: (b, 0, 0))
    new_row = lambda w: pl.BlockSpec((1, 1, w), lambda b, j, pt: (b, 0, 0))

    def page_spec(shape, i):
        return pl.BlockSpec((None,) + shape, lambda b, j, pt: (pt[b * n_pages + j * pages + i], 0, 0))

    grid_spec = pltpu.PrefetchScalarGridSpec(
        num_scalar_prefetch=1,
        grid=(Bd, n_pages // pages),
        in_specs=[per_b(C), per_b(R), new_row(C), new_row(R)]
        + [page_spec((page_rows, C), i) for i in range(pages)]
        + [page_spec((R, page_rows), i) for i in range(pages)],
        out_specs=pl.BlockSpec((1, H, C), lambda b, j, pt: (b, 0, 0)),
        scratch_shapes=[pltpu.VMEM((H, 1), F32), pltpu.VMEM((H, 1), F32), pltpu.VMEM((H, C), F32)],
    )
    return pl.pallas_call(
        functools.partial(_decode_kernel, pages=pages),
        out_shape=jax.ShapeDtypeStruct((Bd, H, C), BF),
        grid_spec=grid_spec,
        compiler_params=_params(2),
        name="decode_attention",
    )(page_table.reshape(-1), q_lat, q_pe, ckv_new, kpe_new,
      *([cache_ckv] * pages), *([cache_kpe_t] * pages))


def _value_up_kernel(o_hbm_ref, olat_ref, wuv_ref, o_ref):
    del o_hbm_ref
    o_ref[...] = jnp.dot(olat_ref[...], wuv_ref[...], preferred_element_type=F32).astype(o_ref.dtype)


def _value_up_into(o_all, o_lat, wuv, *, heads, c_dim, vdim, row_block, rows):
    return pl.pallas_call(
        _value_up_kernel,
        out_shape=jax.ShapeDtypeStruct(o_all.shape, o_all.dtype),
        grid=(heads,),
        in_specs=[pl.BlockSpec(memory_space=pl.ANY),
                  pl.BlockSpec((rows, c_dim), lambda h: (0, h)),
                  pl.BlockSpec((c_dim, vdim), lambda h: (0, h))],
        out_specs=pl.BlockSpec((rows, vdim), lambda h: (row_block, h)),
        input_output_aliases={0: 0},
        compiler_params=_params(1),
        name="value_up",
    )(o_all, o_lat, wuv)


def _router_kernel(x_ref, sc_ref, sh_ref, wr_ref, br_ref, w_ref, i_ref, h_ref, *, n_exp):
    h = x_ref[...] * (1.0 + sc_ref[...]) + sh_ref[...]
    logits = jnp.dot(h, wr_ref[...], preferred_element_type=F32, precision=lax.Precision.HIGHEST) + br_ref[...]
    lane = lax.broadcasted_iota(jnp.int32, logits.shape, 1)
    valid = lane < n_exp
    logits = jnp.where(valid, logits, NEG_INF)
    mx = jnp.max(logits, axis=-1, keepdims=True)
    e = jnp.exp(logits - mx)
    probs = e / jnp.sum(e, axis=-1, keepdims=True)
    big = jnp.int32(1 << 30)
    p1 = jnp.max(probs, axis=-1, keepdims=True)
    i1 = jnp.min(jnp.where((probs == p1) & valid, lane, big), axis=-1, keepdims=True)
    rest = jnp.where((lane == i1) | ~valid, -1.0, probs)
    p2 = jnp.max(rest, axis=-1, keepdims=True)
    i2 = jnp.min(jnp.where(rest == p2, lane, big), axis=-1, keepdims=True)
    denom = p1 + p2
    w_ref[...] = jnp.where(lane == 0, p1 / denom, jnp.where(lane == 1, p2 / denom, 0.0))
    i_ref[...] = jnp.where(lane == 0, i1, jnp.where(lane == 1, i2, 0))
    h_ref[...] = h


def _router(x, mod, sc_chunk, sh_chunk, wr_pad, br_pad, tiles_per_batch, n_exp):
    T, D = x.shape
    row = pl.BlockSpec((ROW_TILE, D), lambda i: (i, 0))
    sel = pl.BlockSpec((ROW_TILE, 128), lambda i: (i, 0))
    return pl.pallas_call(
        functools.partial(_router_kernel, n_exp=n_exp),
        out_shape=[jax.ShapeDtypeStruct((T, 128), F32), jax.ShapeDtypeStruct((T, 128), jnp.int32),
                   jax.ShapeDtypeStruct((T, D), F32)],
        grid=(T // ROW_TILE,),
        in_specs=[row, _mod_spec(tiles_per_batch, D, sc_chunk), _mod_spec(tiles_per_batch, D, sh_chunk),
                  pl.BlockSpec((D, 128), lambda i: (0, 0)), pl.BlockSpec((1, 128), lambda i: (0, 0))],
        out_specs=[sel, sel, row],
        compiler_params=_params(1),
        name="router",
    )(x, mod, mod, wr_pad, br_pad)


def _route_tables(idx2, n_exp, bm, n_blocks):
    T = idx2.shape[0]
    e_flat = idx2.reshape(-1)
    onehot = (e_flat[:, None] == jnp.arange(n_exp, dtype=jnp.int32)[None, :]).astype(jnp.int32)
    rank = jnp.sum((jnp.cumsum(onehot, axis=0) - onehot) * onehot, axis=1)
    counts = jnp.sum(onehot, axis=0)
    blocks_per = (counts + bm - 1) // bm
    blocks_end = jnp.cumsum(blocks_per)
    dest = (blocks_end - blocks_per)[e_flat] * bm + rank
    n_used = blocks_end[-1]
    b = jnp.minimum(jnp.arange(n_blocks, dtype=jnp.int32), n_used - 1)
    block_expert = jnp.sum((b[:, None] >= blocks_end[None, :]).astype(jnp.int32), axis=1)
    row_token = jnp.zeros((n_blocks * bm,), jnp.int32).at[dest].set(jnp.arange(2 * T, dtype=jnp.int32) // 2)
    return row_token, dest.astype(jnp.int32), block_expert.astype(jnp.int32), n_used.reshape(1).astype(jnp.int32)


def _row_copy(src_hbm, row, dst, sem):
    return pltpu.make_async_copy(src_hbm.at[pl.ds(row, 1)], dst, sem)


def _dispatch_kernel(tok_ref, h_hbm, o_ref, buf, sem, *, rows):
    i = pl.program_id(0)
    slot = i % 2

    def issue(step, s):
        def body(r, c):
            _row_copy(h_hbm, tok_ref[step * rows + r], buf.at[s, pl.ds(r, 1)], sem.at[s]).start()
            return c
        lax.fori_loop(0, rows, body, 0, unroll=8)

    @pl.when(i == 0)
    def _():
        issue(0, 0)

    @pl.when(i + 1 < pl.num_programs(0))
    def _():
        issue(i + 1, 1 - slot)

    def wait_body(r, c):
        _row_copy(h_hbm, 0, buf.at[slot, pl.ds(r, 1)], sem.at[slot]).wait()
        return c
    lax.fori_loop(0, rows, wait_body, 0, unroll=8)
    o_ref[...] = buf[slot].astype(o_ref.dtype)


def _dispatch(h, row_token, *, rows=128):
    T, D = h.shape
    Rn = row_token.shape[0]
    assert Rn % rows == 0
    grid_spec = pltpu.PrefetchScalarGridSpec(
        num_scalar_prefetch=1,
        grid=(Rn // rows,),
        in_specs=[pl.BlockSpec(memory_space=pl.ANY)],
        out_specs=pl.BlockSpec((rows, D), lambda i, tok: (i, 0)),
        scratch_shapes=[pltpu.VMEM((2, rows, D), h.dtype), pltpu.SemaphoreType.DMA((2,))],
    )
    return pl.pallas_call(
        functools.partial(_dispatch_kernel, rows=rows),
        out_shape=jax.ShapeDtypeStruct((Rn, D), BF),
        grid_spec=grid_spec,
        compiler_params=_params(1),
        name="moe_dispatch",
    )(row_token, h)


def _combine_postln_kernel(dest_ref, x_ref, g_ref, lng_ref, lnb_ref, w_ref, y_hbm, xo_ref, buf, sem, *, alpha, rows):
    i = pl.program_id(0)
    slot = i % 2

    def issue(step, s):
        def body(r, c):
            a = (step * rows + r) * 2
            _row_copy(y_hbm, dest_ref[a], buf.at[s, 0, pl.ds(r, 1)], sem.at[s]).start()
            _row_copy(y_hbm, dest_ref[a + 1], buf.at[s, 1, pl.ds(r, 1)], sem.at[s]).start()
            return c
        lax.fori_loop(0, rows, body, 0, unroll=8)

    @pl.when(i == 0)
    def _():
        issue(0, 0)

    @pl.when(i + 1 < pl.num_programs(0))
    def _():
        issue(i + 1, 1 - slot)

    def wait_body(r, c):
        _row_copy(y_hbm, 0, buf.at[slot, 0, pl.ds(r, 1)], sem.at[slot]).wait()
        _row_copy(y_hbm, 0, buf.at[slot, 1, pl.ds(r, 1)], sem.at[slot]).wait()
        return c
    lax.fori_loop(0, rows, wait_body, 0, unroll=8)
    f = w_ref[:, 0:1] * buf[slot, 0] + w_ref[:, 1:2] * buf[slot, 1]
    xo_ref[...] = _layernorm_rows(alpha * x_ref[...] + g_ref[...] * f, lng_ref[...], lnb_ref[...])


def _combine_postln(x, y, dest, wts, mod_g, g_chunk, ln_g, ln_b, tiles_per_batch, *, alpha):
    T, D = x.shape
    rows = ROW_TILE
    row = pl.BlockSpec((rows, D), lambda i, d: (i, 0))
    vec = pl.BlockSpec((1, D), lambda i, d: (0, 0))
    grid_spec = pltpu.PrefetchScalarGridSpec(
        num_scalar_prefetch=1,
        grid=(T // rows,),
        in_specs=[row, pl.BlockSpec((rows, D), lambda i, d: (i // tiles_per_batch, g_chunk)), vec, vec,
                  pl.BlockSpec((rows, 128), lambda i, d: (i, 0)), pl.BlockSpec(memory_space=pl.ANY)],
        out_specs=row,
        scratch_shapes=[pltpu.VMEM((2, 2, rows, D), F32), pltpu.SemaphoreType.DMA((2,))],
    )
    return pl.pallas_call(
        functools.partial(_combine_postln_kernel, alpha=alpha, rows=rows),
        out_shape=jax.ShapeDtypeStruct((T, D), F32),
        grid_spec=grid_spec,
        compiler_params=_params(1),
        name="moe_combine_postln",
    )(dest, x, mod_g, ln_g.reshape(1, D), ln_b.reshape(1, D), wts, y)


def _expert_changed(be_ref, i):
    return (i == 0) | (be_ref[i] != be_ref[jnp.maximum(i - 1, 0)])


def _moe_up_kernel(be_ref, nu_ref, x_ref, wg_ref, wu_ref, o_ref, wgb_ref, wub_ref):
    i = pl.program_id(1)

    @pl.when(_expert_changed(be_ref, i))
    def _():
        wgb_ref[...] = wg_ref[...].astype(BF)
        wub_ref[...] = wu_ref[...].astype(BF)

    @pl.when(i < nu_ref[0])
    def _():
        x = x_ref[...]
        g = jnp.dot(x, wgb_ref[...], preferred_element_type=F32)
        u = jnp.dot(x, wub_ref[...], preferred_element_type=F32)
        o_ref[...] = (_silu(g) * u).astype(o_ref.dtype)

    @pl.when(i >= nu_ref[0])
    def _():
        o_ref[...] = jnp.zeros(o_ref.shape, o_ref.dtype)


def _moe_up(xs, wg, wu, block_expert, n_used, *, bm, bn):
    Rn, K = xs.shape
    N = wg.shape[-1]
    bn = min(bn, N)
    assert Rn % bm == 0 and N % bn == 0
    w_spec = pl.BlockSpec((None, None, K, bn), lambda j, i, be, nu: (0, be[i], 0, j))
    grid_spec = pltpu.PrefetchScalarGridSpec(
        num_scalar_prefetch=2,
        grid=(N // bn, Rn // bm),
        in_specs=[pl.BlockSpec((bm, K), lambda j, i, be, nu: (i, 0)), w_spec, w_spec],
        out_specs=pl.BlockSpec((bm, bn), lambda j, i, be, nu: (i, j)),
        scratch_shapes=[pltpu.VMEM((K, bn), BF), pltpu.VMEM((K, bn), BF)],
    )
    return pl.pallas_call(
        _moe_up_kernel,
        out_shape=jax.ShapeDtypeStruct((Rn, N), BF),
        grid_spec=grid_spec,
        compiler_params=_params(2),
        name="moe_up",
    )(block_expert, n_used, xs, wg, wu)


def _moe_down_kernel(be_ref, nu_ref, x_ref, w_ref, o_ref, wb_ref):
    i = pl.program_id(1)

    @pl.when(_expert_changed(be_ref, i))
    def _():
        wb_ref[...] = w_ref[...].astype(BF)

    @pl.when(i < nu_ref[0])
    def _():
        o_ref[...] = jnp.dot(x_ref[...], wb_ref[...], preferred_element_type=F32)

    @pl.when(i >= nu_ref[0])
    def _():
        o_ref[...] = jnp.zeros(o_ref.shape, o_ref.dtype)


def _moe_down(act, wd, block_expert, n_used, *, bm, bn):
    Rn, K = act.shape
    N = wd.shape[-1]
    bn = min(bn, N)
    assert Rn % bm == 0 and N % bn == 0
    grid_spec = pltpu.PrefetchScalarGridSpec(
        num_scalar_prefetch=2,
        grid=(N // bn, Rn // bm),
        in_specs=[pl.BlockSpec((bm, K), lambda j, i, be, nu: (i, 0)),
                  pl.BlockSpec((None, None, K, bn), lambda j, i, be, nu: (0, be[i], 0, j))],
        out_specs=pl.BlockSpec((bm, bn), lambda j, i, be, nu: (i, j)),
        scratch_shapes=[pltpu.VMEM((K, bn), BF)],
    )
    return pl.pallas_call(
        _moe_down_kernel,
        out_shape=jax.ShapeDtypeStruct((Rn, N), F32),
        grid_spec=grid_spec,
        compiler_params=_params(2),
        name="moe_down",
    )(block_expert, n_used, act, wd)


def _pick_bm(T, target):
    best = ROW_TILE
    for m in range(ROW_TILE, target + 1, ROW_TILE):
        if T % m == 0:
            best = m
    return best


def kernel(x_prompt, x_sample, cache_ckv, cache_kpe, page_table, c_prompt, c_sample, w_ada, b_ada, ln_g, ln_b, a_w_in, a_v_g, a_v_b, a_w_s, a_b_s, a_w_out, b_w_dq, b_q_g, b_w_uq, b_w_qr, b_w_o, kv_w_dkv, kv_g, kv_w_kr, kv_w_uk, kv_w_uv, ffn_w_gate, ffn_w_up, ffn_w_down, moe_w_router, moe_b_router, moe_w_gate, moe_w_up, moe_w_down):
    B, S, D = x_prompt.shape
    Bd, Td, _ = x_sample.shape
    depth = w_ada.shape[0]
    n_a = a_w_in.shape[0]
    assert depth == 2 and n_a == 1 and Td == 1 and Bd == ROW_TILE and S % ROW_TILE == 0
    A = a_w_in.shape[2] // 2
    G = a_w_s.shape[1]
    H, nope = b_w_uq.shape[2], b_w_uq.shape[3]
    R = b_w_qr.shape[3]
    C = kv_w_dkv.shape[1]
    vdim = kv_w_uv.shape[2]
    E = moe_w_router.shape[2]
    n_pages, page_rows = page_table.shape[1], cache_ckv.shape[1]
    past_len = n_pages * page_rows
    alpha = (2.0 * depth) ** 0.25
    scale = (nope + R) ** -0.5
    Tp = B * S
    T = Tp + Bd
    tiles_per_batch = S // ROW_TILE
    n_prompt_tiles = Tp // ROW_TILE
    bm = _pick_bm(T, 640)

    x = jnp.concatenate([x_prompt.reshape(Tp, D), x_sample.reshape(Bd, D)], axis=0)

    c_tiles = jnp.concatenate([jnp.repeat(c_prompt, ROW_TILE, axis=0), c_sample], axis=0)
    mods = [_matmul(c_tiles, w_ada, bm=c_tiles.shape[0], bn=512, out_dtype=F32, act="silu_in",
                    bias=b_ada[l].reshape(1, -1), lead=(l,), name="adaln") for l in range(depth)]
    SH_M, SC_M, G_M, SH_F, SC_F, G_F = range(6)

    half = R // 2
    inv = ROPE_BASE ** (-jnp.arange(half, dtype=F32) / half)
    pos = jnp.concatenate([jnp.tile(jnp.arange(S, dtype=jnp.int32), B),
                           jnp.full((Bd,), past_len, jnp.int32)]).astype(F32)
    ang = pos[:, None] * inv[None, :]
    cos2 = jnp.concatenate([jnp.cos(ang), jnp.cos(ang)], axis=-1)
    sin2 = jnp.concatenate([jnp.sin(ang), jnp.sin(ang)], axis=-1)
    lane_pad = jnp.zeros((T, 128 - R), F32)
    cos_q = jnp.concatenate([cos2 * scale, lane_pad], axis=-1)
    sin_q = jnp.concatenate([sin2 * scale, lane_pad], axis=-1)

    h = _modulate(x, mods[0], SC_M, SH_M, tiles_per_batch)
    u = _matmul(h, a_w_in, bm=bm, bn=512, out_dtype=BF, n_off=0, n_out=A, act="gelu", lead=(0,), name="a_in_u")
    vpre = _matmul(h, a_w_in, bm=bm, bn=512, out_dtype=F32, n_off=A, n_out=A, act="gelu", lead=(0,), name="a_in_v")
    tril = jnp.tril(jnp.ones((ROW_TILE, ROW_TILE), bool))
    ws_prompt = jnp.where(tril, a_w_s[0], 0.0)
    ws_sample = a_w_s[0][:, :1, :1] * jnp.eye(ROW_TILE, dtype=F32)
    ws_tiles = jnp.stack([ws_prompt, ws_sample]).astype(BF)
    gd = A // G
    b_prompt = jnp.repeat(a_b_s[0].T, gd, axis=1)
    b_sample = jnp.broadcast_to(b_prompt[:1], b_prompt.shape)
    b_tiles = jnp.stack([b_prompt, b_sample])
    gated, v_sample = _spatial_gate(vpre, u, ws_tiles, b_tiles, a_v_g[0], a_v_b[0], n_prompt_tiles)
    mix = _matmul(gated, a_w_out, bm=bm, bn=512, out_dtype=F32, lead=(0,), name="a_out")
    x, h = _postln(x, mix, mods[0], G_M, ln_g[0, 0], ln_b[0, 0], tiles_per_batch, alpha=alpha,
                   mod_next=mods[0], sc_chunk=SC_F, sh_chunk=SH_F)
    act = _swiglu_up(h, ffn_w_gate, ffn_w_up, bm=bm, bn=512, lead=(0,), name="ffn_up")
    f = _matmul(act, ffn_w_down, bm=bm, bn=512, out_dtype=F32, lead=(0,), name="ffn_down")
    x, h, xb = _postln(x, f, mods[0], G_F, ln_g[0, 1], ln_b[0, 1], tiles_per_batch, alpha=alpha,
                       mod_next=mods[1], sc_chunk=SC_M, sh_chunk=SH_M, emit_bf=True)

    w1, w2 = kv_w_kr[:, :half], kv_w_kr[:, half:]
    zpad = jnp.zeros((D, 128 - R), F32)
    wkv = jnp.concatenate([kv_w_dkv, kv_w_kr, zpad, -w2, w1, zpad], axis=1).astype(BF)
    wuk2 = kv_w_uk.reshape(C, H * nope)
    wuv2 = kv_w_uv.reshape(C, H * vdim)
    ckv, kpe, kfull, vfull = _kv_side(xb, wkv, kv_g.reshape(1, C), cos2, sin2, wuk2.astype(BF), wuv2.astype(BF),
                                      bm=bm, c_dim=C, r_dim=R, heads=H, nope=nope)

    wqr = b_w_qr[0]
    qpad = jnp.zeros(wqr.shape[:2] + (128 - R,), F32)
    wpe = jnp.concatenate([wqr, qpad], axis=-1).reshape(wqr.shape[0], H * 128)
    wrot = jnp.concatenate([-wqr[..., half:], wqr[..., :half], qpad], axis=-1).reshape(wqr.shape[0], H * 128)
    q = _q_side(h, b_w_dq[0].astype(BF), b_q_g[0].reshape(1, -1), b_w_uq[0].reshape(-1, H * nope).astype(BF),
                wpe.astype(BF), wrot.astype(BF), cos_q, sin_q, bm=bm, heads=H, nope=nope, scale=scale)
    o_all = _flash_attention(q, kfull, vfull, batch=B, seq=S, heads=H, vdim=vdim, blk=min(512, S), hb=min(4, H),
                             total_rows=T)

    q_lat = _absorb_query(q, wuk2.astype(BF), heads=H, nope=nope, c_dim=C,
                          row_block=Tp // Bd, rows=Bd).reshape(Bd, H, C)
    q_pe_s = q[Tp:].reshape(Bd, H, 256)[:, :, nope:nope + R]
    o_lat = _decode_attention(page_table, q_lat, q_pe_s, ckv[Tp:].reshape(Bd, 1, C), kpe[Tp:].reshape(Bd, 1, R),
                              cache_ckv, jnp.swapaxes(cache_kpe, 1, 2), pages=min(16, n_pages))
    o_all = _value_up_into(o_all, o_lat.reshape(Bd, H * C), wuv2.astype(BF), heads=H, c_dim=C, vdim=vdim,
                           row_block=Tp // Bd, rows=Bd)
    mix = _matmul(o_all, b_w_o, bm=bm, bn=512, out_dtype=F32, lead=(0,), name="attn_out")
    x = _postln(x, mix, mods[1], G_M, ln_g[1, 0], ln_b[1, 0], tiles_per_batch, alpha=alpha)[0]

    wr_pad = jnp.concatenate([moe_w_router[0], jnp.zeros((D, 128 - E), F32)], axis=1)
    br_pad = jnp.concatenate([moe_b_router[0], jnp.zeros((128 - E,), F32)]).reshape(1, 128)
    wts, idx, hf = _router(x, mods[1], SC_F, SH_F, wr_pad, br_pad, tiles_per_batch, E)
    bme = 256
    n_blocks = pl.cdiv(2 * T + E * (bme - 1), bme)
    row_token, dest, block_expert, n_used = _route_tables(idx[:, :2], E, bme, n_blocks)
    hs = _dispatch(hf, row_token)
    act = _moe_up(hs, moe_w_gate, moe_w_up, block_expert, n_used, bm=bme, bn=512)
    ys = _moe_down(act, moe_w_down, block_expert, n_used, bm=bme, bn=512)
    x = _combine_postln(x, ys, dest, wts, mods[1], G_F, ln_g[1, 1], ln_b[1, 1], tiles_per_batch, alpha=alpha)

    y_prompt = x[:Tp].reshape(B, S, D)
    y_sample = x[Tp:].reshape(Bd, Td, D)
    return (y_prompt, y_sample,
            ckv[:Tp].reshape(B, S, C), kpe[:Tp].reshape(B, S, R),
            ckv[Tp:].reshape(Bd, Td, C), kpe[Tp:].reshape(Bd, Td, R),
            v_sample.reshape(n_a, Bd, Td, A))
```

```python
import functools
import math

import jax
import jax.numpy as jnp
from jax import lax
from jax.experimental import pallas as pl
from jax.experimental.pallas import tpu as pltpu

BF = jnp.bfloat16
F32 = jnp.float32

ROW_TILE = 128
LN_EPS = 1e-5
RMS_EPS = 1e-6
ROPE_BASE = 10000.0
NEG_INF = float("-inf")
VMEM_LIMIT = 56 * 1024 * 1024


def _params(n_axes, vmem=VMEM_LIMIT):
    return pltpu.CompilerParams(dimension_semantics=("arbitrary",) * n_axes, vmem_limit_bytes=vmem)


def _gelu_tanh(x):
    return 0.5 * x * (1.0 + jnp.tanh(math.sqrt(2.0 / math.pi) * (x + 0.044715 * x * x * x)))


def _silu(x):
    return x * (1.0 / (1.0 + jnp.exp(-x)))


def _layernorm_rows(x, g, b):
    mu = jnp.mean(x, axis=-1, keepdims=True)
    xc = x - mu
    var = jnp.mean(xc * xc, axis=-1, keepdims=True)
    return xc * lax.rsqrt(var + LN_EPS) * g + b


def _mm_kernel(*refs, act, has_bias, has_add):
    x_ref, w_ref = refs[0], refs[1]
    k = 2
    bias_ref = add_ref = None
    if has_bias:
        bias_ref = refs[k]; k += 1
    if has_add:
        add_ref = refs[k]; k += 1
    o_ref, wbf_ref = refs[k], refs[k + 1]

    @pl.when(pl.program_id(1) == 0)
    def _():
        wbf_ref[...] = w_ref[...].astype(BF)

    x = x_ref[...]
    if act == "silu_in":
        x = _silu(x.astype(F32))
    acc = jnp.dot(x.astype(BF), wbf_ref[...], preferred_element_type=F32)
    if has_bias:
        acc = acc + bias_ref[...]
    if act == "gelu":
        acc = _gelu_tanh(acc)
    if has_add:
        acc = acc + add_ref[...]
    o_ref[...] = acc.astype(o_ref.dtype)


def _matmul(x, w, *, bm, bn, out_dtype, n_off=0, n_out=None, act=None, bias=None, add=None,
            lead=None, name="mm"):
    M, K = x.shape
    N = w.shape[-1] if n_out is None else n_out
    bn = min(bn, N)
    assert M % bm == 0 and N % bn == 0 and n_off % bn == 0
    offb = n_off // bn
    if lead is None:
        w_spec = pl.BlockSpec((K, bn), lambda j, i: (0, j + offb))
    else:
        lead_idx = tuple(lead)
        w_spec = pl.BlockSpec((None,) * len(lead_idx) + (K, bn), lambda j, i: lead_idx + (0, j + offb))
    in_specs = [pl.BlockSpec((bm, K), lambda j, i: (i, 0)), w_spec]
    args = [x, w]
    if bias is not None:
        in_specs.append(pl.BlockSpec((1, bn), lambda j, i: (0, j + offb)))
        args.append(bias)
    if add is not None:
        in_specs.append(pl.BlockSpec((bm, bn), lambda j, i: (i, j)))
        args.append(add)
    return pl.pallas_call(
        functools.partial(_mm_kernel, act=act, has_bias=bias is not None, has_add=add is not None),
        out_shape=jax.ShapeDtypeStruct((M, N), out_dtype),
        grid=(N // bn, M // bm),
        in_specs=in_specs,
        out_specs=pl.BlockSpec((bm, bn), lambda j, i: (i, j)),
        scratch_shapes=[pltpu.VMEM((K, bn), BF)],
        compiler_params=_params(2),
        name=name,
    )(*args)


def _swiglu_up_kernel(x_ref, wg_ref, wu_ref, o_ref, wgb_ref, wub_ref):
    @pl.when(pl.program_id(1) == 0)
    def _():
        wgb_ref[...] = wg_ref[...].astype(BF)
        wub_ref[...] = wu_ref[...].astype(BF)

    x = x_ref[...].astype(BF)
    g = jnp.dot(x, wgb_ref[...], preferred_element_type=F32)
    u = jnp.dot(x, wub_ref[...], preferred_element_type=F32)
    o_ref[...] = (_silu(g) * u).astype(o_ref.dtype)


def _swiglu_up(x, wg, wu, *, bm, bn, lead, name="swiglu_up"):
    M, K = x.shape
    N = wg.shape[-1]
    bn = min(bn, N)
    assert M % bm == 0 and N % bn == 0
    lead_idx = tuple(lead)
    w_spec = pl.BlockSpec((None,) * len(lead_idx) + (K, bn), lambda j, i: lead_idx + (0, j))
    return pl.pallas_call(
        _swiglu_up_kernel,
        out_shape=jax.ShapeDtypeStruct((M, N), BF),
        grid=(N // bn, M // bm),
        in_specs=[pl.BlockSpec((bm, K), lambda j, i: (i, 0)), w_spec, w_spec],
        out_specs=pl.BlockSpec((bm, bn), lambda j, i: (i, j)),
        scratch_shapes=[pltpu.VMEM((K, bn), BF), pltpu.VMEM((K, bn), BF)],
        compiler_params=_params(2),
        name=name,
    )(x, wg, wu)


def _mod_spec(tiles_per_batch, D, chunk):
    return pl.BlockSpec((ROW_TILE, D), lambda i: (i // tiles_per_batch, chunk))


def _modulate_kernel(x_ref, sc_ref, sh_ref, h_ref):
    h_ref[...] = (x_ref[...] * (1.0 + sc_ref[...]) + sh_ref[...]).astype(h_ref.dtype)


def _modulate(x, mod, sc_chunk, sh_chunk, tiles_per_batch):
    T, D = x.shape
    row = pl.BlockSpec((ROW_TILE, D), lambda i: (i, 0))
    return pl.pallas_call(
        _modulate_kernel,
        out_shape=jax.ShapeDtypeStruct((T, D), BF),
        grid=(T // ROW_TILE,),
        in_specs=[row, _mod_spec(tiles_per_batch, D, sc_chunk), _mod_spec(tiles_per_batch, D, sh_chunk)],
        out_specs=row,
        compiler_params=_params(1),
        name="modulate",
    )(x, mod, mod)


def _postln_kernel(*refs, alpha, modulate_next, emit_bf):
    x_ref, mix_ref, g_ref, lng_ref, lnb_ref = refs[:5]
    k = 5
    if modulate_next:
        sc_ref, sh_ref = refs[k], refs[k + 1]
        k += 2
    xo_ref = refs[k]; k += 1
    xn = _layernorm_rows(alpha * x_ref[...] + g_ref[...] * mix_ref[...], lng_ref[...], lnb_ref[...])
    xo_ref[...] = xn
    if modulate_next:
        refs[k][...] = (xn * (1.0 + sc_ref[...]) + sh_ref[...]).astype(BF)
        k += 1
    if emit_bf:
        refs[k][...] = xn.astype(BF)


def _postln(x, mix, mod_g, g_chunk, ln_g, ln_b, tiles_per_batch, *, alpha, mod_next=None,
            sc_chunk=None, sh_chunk=None, emit_bf=False):
    T, D = x.shape
    row = pl.BlockSpec((ROW_TILE, D), lambda i: (i, 0))
    vec = pl.BlockSpec((1, D), lambda i: (0, 0))
    in_specs = [row, row, _mod_spec(tiles_per_batch, D, g_chunk), vec, vec]
    args = [x, mix, mod_g, ln_g.reshape(1, D), ln_b.reshape(1, D)]
    out_shape = [jax.ShapeDtypeStruct((T, D), F32)]
    out_specs = [row]
    if mod_next is not None:
        in_specs += [_mod_spec(tiles_per_batch, D, sc_chunk), _mod_spec(tiles_per_batch, D, sh_chunk)]
        args += [mod_next, mod_next]
        out_shape.append(jax.ShapeDtypeStruct((T, D), BF))
        out_specs.append(row)
    if emit_bf:
        out_shape.append(jax.ShapeDtypeStruct((T, D), BF))
        out_specs.append(row)
    return pl.pallas_call(
        functools.partial(_postln_kernel, alpha=alpha, modulate_next=mod_next is not None, emit_bf=emit_bf),
        out_shape=out_shape,
        grid=(T // ROW_TILE,),
        in_specs=in_specs,
        out_specs=out_specs,
        compiler_params=_params(1),
        name="postln",
    )(*args)


def _spatial_gate_kernel(vpre_ref, u_ref, ws_ref, b_ref, vg_ref, vb_ref, mix_ref, vout_ref, *, groups):
    v = _layernorm_rows(vpre_ref[...], vg_ref[...], vb_ref[...])

    @pl.when(pl.program_id(0) == pl.num_programs(0) - 1)
    def _():
        vout_ref[...] = v

    vb16 = v.astype(BF)
    gd = v.shape[1] // groups
    for g in range(groups):
        cols = slice(g * gd, (g + 1) * gd)
        s = jnp.dot(ws_ref[g], vb16[:, cols], preferred_element_type=F32) + b_ref[:, cols]
        mix_ref[:, cols] = (u_ref[:, cols].astype(F32) * s).astype(BF)


def _spatial_gate(vpre, u, ws_tiles, b_tiles, v_g, v_b, n_prompt_tiles):
    T, A = vpre.shape
    G = ws_tiles.shape[1]
    row = pl.BlockSpec((ROW_TILE, A), lambda i: (i, 0))
    vec = pl.BlockSpec((1, A), lambda i: (0, 0))
    return pl.pallas_call(
        functools.partial(_spatial_gate_kernel, groups=G),
        out_shape=[jax.ShapeDtypeStruct((T, A), BF), jax.ShapeDtypeStruct((ROW_TILE, A), F32)],
        grid=(T // ROW_TILE,),
        in_specs=[row, row,
                  pl.BlockSpec((None, G, ROW_TILE, ROW_TILE), lambda i: (i // n_prompt_tiles, 0, 0, 0)),
                  pl.BlockSpec((None, ROW_TILE, A), lambda i: (i // n_prompt_tiles, 0, 0)),
                  vec, vec],
        out_specs=[row, pl.BlockSpec((ROW_TILE, A), lambda i: (0, 0))],
        compiler_params=_params(1),
        name="spatial_gate",
    )(vpre, u, ws_tiles, b_tiles, v_g.reshape(1, A), v_b.reshape(1, A))


_CONTRACT_LAST = (((1,), (1,)), ((), ()))


def _kv_kernel(x_ref, wkv_ref, kvg_ref, cos_ref, sin_ref, wuk_ref, wuvt_ref,
               ckv_ref, kpe_ref, kfull_ref, vt_ref, *, c_dim, r_dim, heads, nope):
    y = jnp.dot(x_ref[...], wkv_ref[...], preferred_element_type=F32)
    c = y[:, :c_dim]
    ckv = c * lax.rsqrt(jnp.mean(c * c, axis=-1, keepdims=True) + RMS_EPS) * kvg_ref[...]
    ckv_ref[...] = ckv
    kpe = y[:, c_dim:c_dim + r_dim] * cos_ref[...] + y[:, c_dim + 128:c_dim + 128 + r_dim] * sin_ref[...]
    kpe_ref[...] = kpe
    cb = ckv.astype(BF)
    kn = jnp.dot(cb, wuk_ref[...], preferred_element_type=F32)
    vt_ref[...] = lax.dot_general(wuvt_ref[...], cb, _CONTRACT_LAST, preferred_element_type=F32).astype(BF)
    kpe_pad = jnp.concatenate([kpe, jnp.zeros((kpe.shape[0], 128 - r_dim), F32)], axis=-1).astype(BF)
    for h in range(heads):
        kfull_ref[:, h * 256:h * 256 + nope] = kn[:, h * nope:(h + 1) * nope].astype(BF)
        kfull_ref[:, h * 256 + nope:(h + 1) * 256] = kpe_pad


def _kv_side(xb, wkv, kv_g, cos_k, sin_k, wuk, wuv_t, *, bm, c_dim, r_dim, heads, nope):
    T, D = xb.shape
    assert nope == 128 and r_dim <= 128
    row = lambda w: pl.BlockSpec((bm, w), lambda i: (i, 0))
    full = lambda a: pl.BlockSpec(a.shape, lambda i: (0,) * a.ndim)
    hv = wuv_t.shape[0]
    return pl.pallas_call(
        functools.partial(_kv_kernel, c_dim=c_dim, r_dim=r_dim, heads=heads, nope=nope),
        out_shape=[jax.ShapeDtypeStruct((T, c_dim), F32), jax.ShapeDtypeStruct((T, r_dim), F32),
                   jax.ShapeDtypeStruct((T, heads * 256), BF), jax.ShapeDtypeStruct((hv, T), BF)],
        grid=(T // bm,),
        in_specs=[row(D), full(wkv), full(kv_g), row(r_dim), row(r_dim), full(wuk), full(wuv_t)],
        out_specs=[row(c_dim), row(r_dim), row(heads * 256), pl.BlockSpec((hv, bm), lambda i: (0, i))],
        compiler_params=_params(1),
        name="kv_side",
    )(xb, wkv, kv_g, cos_k, sin_k, wuk, wuv_t)


def _q_kernel(h_ref, wdqt_ref, qg_ref, wuqt_ref, wpet_ref, wrott_ref, cost_ref, sint_ref, qt_ref,
              *, heads, nope, scale):
    ct = lax.dot_general(wdqt_ref[...], h_ref[...], _CONTRACT_LAST, preferred_element_type=F32)
    cqt = (ct * lax.rsqrt(jnp.mean(ct * ct, axis=0, keepdims=True) + RMS_EPS) * qg_ref[...]).astype(BF)
    qn = jnp.dot(wuqt_ref[...], cqt, preferred_element_type=F32) * scale
    qp = jnp.dot(wpet_ref[...], cqt, preferred_element_type=F32)
    qr = jnp.dot(wrott_ref[...], cqt, preferred_element_type=F32)
    cos = cost_ref[...]
    sin = sint_ref[...]
    for h in range(heads):
        qt_ref[h * 256:h * 256 + nope, :] = qn[h * nope:(h + 1) * nope].astype(BF)
        pe = qp[h * 128:(h + 1) * 128] * cos + qr[h * 128:(h + 1) * 128] * sin
        qt_ref[h * 256 + nope:(h + 1) * 256, :] = pe.astype(BF)


def _q_side(hb, wdq_t, q_g_col, wuq_t, wpe_t, wrot_t, cos_t, sin_t, *, bm, heads, nope, scale):
    T, D = hb.shape
    assert nope == 128
    full = lambda a: pl.BlockSpec(a.shape, lambda i: (0,) * a.ndim)
    col = lambda r: pl.BlockSpec((r, bm), lambda i: (0, i))
    return pl.pallas_call(
        functools.partial(_q_kernel, heads=heads, nope=nope, scale=scale),
        out_shape=jax.ShapeDtypeStruct((heads * 256, T), BF),
        grid=(T // bm,),
        in_specs=[pl.BlockSpec((bm, D), lambda i: (i, 0)), full(wdq_t), full(q_g_col), full(wuq_t),
                  full(wpe_t), full(wrot_t), col(128), col(128)],
        out_specs=col(heads * 256),
        compiler_params=_params(1),
        name="q_side",
    )(hb, wdq_t, q_g_col, wuq_t, wpe_t, wrot_t, cos_t, sin_t)


def _flash_kernel(qi_ref, ki_ref, k_ref, qt_ref, vt_ref, o_init_ref, o_ref, m_ref, l_ref, acc_ref, *, hb, vdim):
    del o_init_ref
    step = pl.program_id(2)
    qi = qi_ref[step]
    ki = ki_ref[step]

    @pl.when(ki == 0)
    def _():
        m_ref[...] = jnp.full(m_ref.shape, NEG_INF, F32)
        l_ref[...] = jnp.zeros(l_ref.shape, F32)
        acc_ref[...] = jnp.zeros(acc_ref.shape, F32)

    def update(masked):
        for h in range(hb):
            qr = slice(h * 256, (h + 1) * 256)
            vr = slice(h * vdim, (h + 1) * vdim)
            st = jnp.dot(k_ref[:, qr], qt_ref[qr, :], preferred_element_type=F32)
            if masked:
                key = lax.broadcasted_iota(jnp.int32, st.shape, 0)
                qry = lax.broadcasted_iota(jnp.int32, st.shape, 1)
                st = jnp.where(key <= qry, st, NEG_INF)
            m_prev = m_ref[h]
            m_new = jnp.maximum(m_prev, jnp.max(st, axis=0, keepdims=True))
            corr = jnp.exp2(m_prev - m_new)
            pt = jnp.exp2(st - m_new)
            l_ref[h] = corr * l_ref[h] + jnp.sum(pt, axis=0, keepdims=True)
            acc_ref[vr, :] = corr * acc_ref[vr, :] + jnp.dot(vt_ref[vr, :], pt.astype(BF),
                                                             preferred_element_type=F32)
            m_ref[h] = m_new

    @pl.when(ki < qi)
    def _():
        update(False)

    @pl.when(ki == qi)
    def _():
        update(True)
        for h in range(hb):
            vr = slice(h * vdim, (h + 1) * vdim)
            o_ref[:, vr] = (acc_ref[vr, :] / l_ref[h]).T.astype(o_ref.dtype)


def _flash_attention(kfull, q_t, v_t, *, batch, seq, heads, vdim, blk, hb, total_rows):
    assert heads % hb == 0
    nb = seq // blk
    pairs = [(a, b) for a in range(nb) for b in range(a + 1)]
    qi_map = jnp.asarray([p[0] for p in pairs], jnp.int32)
    ki_map = jnp.asarray([p[1] for p in pairs], jnp.int32)
    grid_spec = pltpu.PrefetchScalarGridSpec(
        num_scalar_prefetch=2,
        grid=(batch, heads // hb, len(pairs)),
        in_specs=[
            pl.BlockSpec((blk, hb * 256), lambda b, h, s, qi, ki: (b * nb + ki[s], h)),
            pl.BlockSpec((hb * 256, blk), lambda b, h, s, qi, ki: (h, b * nb + qi[s])),
            pl.BlockSpec((hb * vdim, blk), lambda b, h, s, qi, ki: (h, b * nb + ki[s])),
            pl.BlockSpec(memory_space=pl.ANY),
        ],
        out_specs=pl.BlockSpec((blk, hb * vdim), lambda b, h, s, qi, ki: (b * nb + qi[s], h)),
        scratch_shapes=[pltpu.VMEM((hb, 1, blk), F32), pltpu.VMEM((hb, 1, blk), F32),
                        pltpu.VMEM((hb * vdim, blk), F32)],
    )
    return pl.pallas_call(
        functools.partial(_flash_kernel, hb=hb, vdim=vdim),
        out_shape=jax.ShapeDtypeStruct((total_rows, heads * vdim), BF),
        grid_spec=grid_spec,
        input_output_aliases={5: 0},
        compiler_params=_params(3),
        name="flash_prompt",
    )(qi_map, ki_map, kfull, q_t, v_t, jnp.zeros((total_rows, heads * vdim), BF))


def _absorb_kernel(wuk_ref, qt_ref, o_ref):
    o_ref[...] = jnp.dot(wuk_ref[...], qt_ref[...], preferred_element_type=F32).astype(o_ref.dtype)


def _absorb_query(q_t, wuk, *, heads, nope, c_dim, col_block, cols):
    return pl.pallas_call(
        _absorb_kernel,
        out_shape=jax.ShapeDtypeStruct((heads * c_dim, cols), BF),
        grid=(heads,),
        in_specs=[pl.BlockSpec((c_dim, nope), lambda h: (0, h)),
                  pl.BlockSpec((nope, cols), lambda h: (2 * h, col_block))],
        out_specs=pl.BlockSpec((c_dim, cols), lambda h: (h, 0)),
        compiler_params=_params(1),
        name="absorb_query",
    )(wuk, q_t)


def _page_copies(pt_ref, ck_hbm, kp_hbm, ck_buf, kp_buf, sem, step, slot, i, pages):
    page = pt_ref[step * pages + i]
    return (pltpu.make_async_copy(ck_hbm.at[page], ck_buf.at[slot, i], sem.at[0, slot]),
            pltpu.make_async_copy(kp_hbm.at[page], kp_buf.at[slot, i], sem.at[1, slot]))


def _decode_kernel(pt_ref, qlat_ref, qlt_ref, qpe_ref, cnew_ref, knew_ref, ck_hbm, kp_hbm, o_ref,
                   ck_buf, kp_buf, sem, m_ref, l_ref, acc_ref, *, pages, steps_per_seq):
    n = pl.program_id(0)
    j = n % steps_per_seq
    slot = n % 2

    def start_step(step, s):
        for i in range(pages):
            for cp in _page_copies(pt_ref, ck_hbm, kp_hbm, ck_buf, kp_buf, sem, step, s, i, pages):
                cp.start()

    @pl.when(n == 0)
    def _():
        start_step(0, 0)

    @pl.when(n + 1 < pl.num_programs(0))
    def _():
        start_step(n + 1, 1 - slot)

    @pl.when(j == 0)
    def _():
        m_ref[...] = jnp.full(m_ref.shape, NEG_INF, F32)
        l_ref[...] = jnp.zeros(l_ref.shape, F32)
        acc_ref[...] = jnp.zeros(acc_ref.shape, F32)

    for i in range(pages):
        for cp in _page_copies(pt_ref, ck_hbm, kp_hbm, ck_buf, kp_buf, sem, n, slot, i, pages):
            cp.wait()

    ql = qlat_ref[0]
    qlt = qlt_ref[0]
    qp = qpe_ref[0]
    heads = ql.shape[0]
    cks = []
    scores = []
    for i in range(pages):
        ck = ck_buf[slot, i].astype(BF)
        kp = kp_buf[slot, i].astype(BF)
        cks.append(ck)
        st = jnp.dot(ck, qlt, preferred_element_type=F32)
        st = jnp.where(lax.broadcasted_iota(jnp.int32, st.shape, 1) < heads, st, 0.0)
        scores.append(st.T[:heads] + jnp.dot(qp, kp, preferred_element_type=F32))
    s = jnp.concatenate(scores, axis=-1)
    m_prev = m_ref[...]
    m_new = jnp.maximum(m_prev, jnp.max(s, axis=-1, keepdims=True))
    corr = jnp.exp2(m_prev - m_new)
    p = jnp.exp2(s - m_new)
    l_new = corr * l_ref[...] + jnp.sum(p, axis=-1, keepdims=True)
    pb = p.astype(BF)
    rows = cks[0].shape[0]
    pv = jnp.dot(pb[:, :rows], cks[0], preferred_element_type=F32)
    for i in range(1, pages):
        pv = pv + jnp.dot(pb[:, i * rows:(i + 1) * rows], cks[i], preferred_element_type=F32)
    acc_new = corr * acc_ref[...] + pv
    m_ref[...] = m_new
    l_ref[...] = l_new
    acc_ref[...] = acc_new

    @pl.when(j == steps_per_seq - 1)
    def _():
        cn = cnew_ref[0].astype(BF).astype(F32)
        kn = knew_ref[0].astype(BF).astype(F32)
        s_self = (jnp.sum(ql.astype(F32) * cn, axis=-1, keepdims=True)
                  + jnp.sum(qp.astype(F32) * kn, axis=-1, keepdims=True))
        m_fin = jnp.maximum(m_new, s_self)
        c2 = jnp.exp2(m_new - m_fin)
        p_self = jnp.exp2(s_self - m_fin)
        l_fin = c2 * l_new + p_self
        acc_fin = c2 * acc_new + p_self.astype(BF).astype(F32) * cn
        o_ref[0] = (acc_fin / l_fin).astype(o_ref.dtype)


def _decode_attention(page_table, q_lat, q_lat_t, q_pe, ckv_new, kpe_new, cache_ckv, cache_kpe_t, *, pages):
    Bd, H, C = q_lat.shape
    R = q_pe.shape[2]
    n_pages = page_table.shape[1]
    page_rows = cache_ckv.shape[1]
    assert n_pages % pages == 0
    sps = n_pages // pages
    per_b = lambda *shape: pl.BlockSpec((1,) + shape, lambda n, pt: (n // sps, 0, 0))
    hbm = pl.BlockSpec(memory_space=pl.ANY)
    grid_spec = pltpu.PrefetchScalarGridSpec(
        num_scalar_prefetch=1,
        grid=(Bd * sps,),
        in_specs=[per_b(H, C), per_b(C, 128), per_b(H, R), per_b(1, C), per_b(1, R), hbm, hbm],
        out_specs=per_b(H, C),
        scratch_shapes=[pltpu.VMEM((2, pages, page_rows, C), cache_ckv.dtype),
                        pltpu.VMEM((2, pages, R, page_rows), cache_kpe_t.dtype),
                        pltpu.SemaphoreType.DMA((2, 2)),
                        pltpu.VMEM((H, 1), F32), pltpu.VMEM((H, 1), F32), pltpu.VMEM((H, C), F32)],
    )
    return pl.pallas_call(
        functools.partial(_decode_kernel, pages=pages, steps_per_seq=sps),
        out_shape=jax.ShapeDtypeStruct((Bd, H, C), BF),
        grid_spec=grid_spec,
        compiler_params=_params(1),
        name="decode_attention",
    )(page_table.reshape(-1), q_lat, q_lat_t, q_pe, ckv_new, kpe_new, cache_ckv, cache_kpe_t)


def _value_up_kernel(o_hbm_ref, olat_ref, wuv_ref, o_ref):
    del o_hbm_ref
    o_ref[...] = jnp.dot(olat_ref[...], wuv_ref[...], preferred_element_type=F32).astype(o_ref.dtype)


def _value_up_into(o_all, o_lat, wuv, *, heads, c_dim, vdim, row_block, rows):
    return pl.pallas_call(
        _value_up_kernel,
        out_shape=jax.ShapeDtypeStruct(o_all.shape, o_all.dtype),
        grid=(heads,),
        in_specs=[pl.BlockSpec(memory_space=pl.ANY),
                  pl.BlockSpec((rows, c_dim), lambda h: (0, h)),
                  pl.BlockSpec((c_dim, vdim), lambda h: (0, h))],
        out_specs=pl.BlockSpec((rows, vdim), lambda h: (row_block, h)),
        input_output_aliases={0: 0},
        compiler_params=_params(1),
        name="value_up",
    )(o_all, o_lat, wuv)


def _router_kernel(x_ref, sc_ref, sh_ref, wr_ref, br_ref, w_ref, i_ref, h_ref, *, n_exp):
    h = x_ref[...] * (1.0 + sc_ref[...]) + sh_ref[...]
    logits = jnp.dot(h, wr_ref[...], preferred_element_type=F32, precision=lax.Precision.HIGHEST) + br_ref[...]
    lane = lax.broadcasted_iota(jnp.int32, logits.shape, 1)
    valid = lane < n_exp
    logits = jnp.where(valid, logits, NEG_INF)
    mx = jnp.max(logits, axis=-1, keepdims=True)
    e = jnp.exp(logits - mx)
    probs = e / jnp.sum(e, axis=-1, keepdims=True)
    big = jnp.int32(1 << 30)
    p1 = jnp.max(probs, axis=-1, keepdims=True)
    i1 = jnp.min(jnp.where((probs == p1) & valid, lane, big), axis=-1, keepdims=True)
    rest = jnp.where((lane == i1) | ~valid, -1.0, probs)
    p2 = jnp.max(rest, axis=-1, keepdims=True)
    i2 = jnp.min(jnp.where(rest == p2, lane, big), axis=-1, keepdims=True)
    denom = p1 + p2
    w_ref[...] = jnp.where(lane == 0, p1 / denom, jnp.where(lane == 1, p2 / denom, 0.0))
    i_ref[...] = jnp.where(lane == 0, i1, jnp.where(lane == 1, i2, 0))
    h_ref[...] = h


def _router(x, mod, sc_chunk, sh_chunk, wr_pad, br_pad, tiles_per_batch, n_exp):
    T, D = x.shape
    row = pl.BlockSpec((ROW_TILE, D), lambda i: (i, 0))
    sel = pl.BlockSpec((ROW_TILE, 128), lambda i: (i, 0))
    return pl.pallas_call(
        functools.partial(_router_kernel, n_exp=n_exp),
        out_shape=[jax.ShapeDtypeStruct((T, 128), F32), jax.ShapeDtypeStruct((T, 128), jnp.int32),
                   jax.ShapeDtypeStruct((T, D), F32)],
        grid=(T // ROW_TILE,),
        in_specs=[row, _mod_spec(tiles_per_batch, D, sc_chunk), _mod_spec(tiles_per_batch, D, sh_chunk),
                  pl.BlockSpec((D, 128), lambda i: (0, 0)), pl.BlockSpec((1, 128), lambda i: (0, 0))],
        out_specs=[sel, sel, row],
        compiler_params=_params(1),
        name="router",
    )(x, mod, mod, wr_pad, br_pad)


def _route_tables(idx2, n_exp, bm, n_blocks):
    T = idx2.shape[0]
    e_flat = idx2.reshape(-1)
    onehot = (e_flat[:, None] == jnp.arange(n_exp, dtype=jnp.int32)[None, :]).astype(jnp.int32)
    rank = jnp.sum((jnp.cumsum(onehot, axis=0) - onehot) * onehot, axis=1)
    counts = jnp.sum(onehot, axis=0)
    blocks_per = (counts + bm - 1) // bm
    blocks_end = jnp.cumsum(blocks_per)
    dest = (blocks_end - blocks_per)[e_flat] * bm + rank
    n_used = blocks_end[-1]
    b = jnp.minimum(jnp.arange(n_blocks, dtype=jnp.int32), n_used - 1)
    block_expert = jnp.sum((b[:, None] >= blocks_end[None, :]).astype(jnp.int32), axis=1)
    row_token = jnp.zeros((n_blocks * bm,), jnp.int32).at[dest].set(jnp.arange(2 * T, dtype=jnp.int32) // 2)
    return row_token, dest.astype(jnp.int32), block_expert.astype(jnp.int32), n_used.reshape(1).astype(jnp.int32)


def _row_copy(src_hbm, row, dst, sem):
    return pltpu.make_async_copy(src_hbm.at[pl.ds(row, 1)], dst, sem)


def _dispatch_kernel(tok_ref, h_hbm, o_ref, buf, sem, *, rows):
    i = pl.program_id(0)
    slot = i % 2

    def issue(step, s):
        def body(r, c):
            _row_copy(h_hbm, tok_ref[step * rows + r], buf.at[s, pl.ds(r, 1)], sem.at[s]).start()
            return c
        lax.fori_loop(0, rows, body, 0, unroll=8)

    @pl.when(i == 0)
    def _():
        issue(0, 0)

    @pl.when(i + 1 < pl.num_programs(0))
    def _():
        issue(i + 1, 1 - slot)

    def wait_body(r, c):
        _row_copy(h_hbm, 0, buf.at[slot, pl.ds(r, 1)], sem.at[slot]).wait()
        return c
    lax.fori_loop(0, rows, wait_body, 0, unroll=8)
    o_ref[...] = buf[slot].astype(o_ref.dtype)


def _dispatch(h, row_token, *, rows=128):
    T, D = h.shape
    Rn = row_token.shape[0]
    assert Rn % rows == 0
    grid_spec = pltpu.PrefetchScalarGridSpec(
        num_scalar_prefetch=1,
        grid=(Rn // rows,),
        in_specs=[pl.BlockSpec(memory_space=pl.ANY)],
        out_specs=pl.BlockSpec((rows, D), lambda i, tok: (i, 0)),
        scratch_shapes=[pltpu.VMEM((2, rows, D), h.dtype), pltpu.SemaphoreType.DMA((2,))],
    )
    return pl.pallas_call(
        functools.partial(_dispatch_kernel, rows=rows),
        out_shape=jax.ShapeDtypeStruct((Rn, D), BF),
        grid_spec=grid_spec,
        compiler_params=_params(1),
        name="moe_dispatch",
    )(row_token, h)


def _combine_postln_kernel(dest_ref, x_ref, g_ref, lng_ref, lnb_ref, w_ref, y_hbm, xo_ref, buf, sem, *, alpha, rows):
    i = pl.program_id(0)
    slot = i % 2

    def issue(step, s):
        def body(r, c):
            a = (step * rows + r) * 2
            _row_copy(y_hbm, dest_ref[a], buf.at[s, 0, pl.ds(r, 1)], sem.at[s]).start()
            _row_copy(y_hbm, dest_ref[a + 1], buf.at[s, 1, pl.ds(r, 1)], sem.at[s]).start()
            return c
        lax.fori_loop(0, rows, body, 0, unroll=8)

    @pl.when(i == 0)
    def _():
        issue(0, 0)

    @pl.when(i + 1 < pl.num_programs(0))
    def _():
        issue(i + 1, 1 - slot)

    def wait_body(r, c):
        _row_copy(y_hbm, 0, buf.at[slot, 0, pl.ds(r, 1)], sem.at[slot]).wait()
        _row_copy(y_hbm, 0, buf.at[slot, 1, pl.ds(r, 1)], sem.at[slot]).wait()
        return c
    lax.fori_loop(0, rows, wait_body, 0, unroll=8)
    f = w_ref[:, 0:1] * buf[slot, 0] + w_ref[:, 1:2] * buf[slot, 1]
    xo_ref[...] = _layernorm_rows(alpha * x_ref[...] + g_ref[...] * f, lng_ref[...], lnb_ref[...])


def _combine_postln(x, y, dest, wts, mod_g, g_chunk, ln_g, ln_b, tiles_per_batch, *, alpha):
    T, D = x.shape
    rows = ROW_TILE
    row = pl.BlockSpec((rows, D), lambda i, d: (i, 0))
    vec = pl.BlockSpec((1, D), lambda i, d: (0, 0))
    grid_spec = pltpu.PrefetchScalarGridSpec(
        num_scalar_prefetch=1,
        grid=(T // rows,),
        in_specs=[row, pl.BlockSpec((rows, D), lambda i, d: (i // tiles_per_batch, g_chunk)), vec, vec,
                  pl.BlockSpec((rows, 128), lambda i, d: (i, 0)), pl.BlockSpec(memory_space=pl.ANY)],
        out_specs=row,
        scratch_shapes=[pltpu.VMEM((2, 2, rows, D), F32), pltpu.SemaphoreType.DMA((2,))],
    )
    return pl.pallas_call(
        functools.partial(_combine_postln_kernel, alpha=alpha, rows=rows),
        out_shape=jax.ShapeDtypeStruct((T, D), F32),
        grid_spec=grid_spec,
        compiler_params=_params(1),
        name="moe_combine_postln",
    )(dest, x, mod_g, ln_g.reshape(1, D), ln_b.reshape(1, D), wts, y)


def _expert_changed(be_ref, i):
    return (i == 0) | (be_ref[i] != be_ref[jnp.maximum(i - 1, 0)])


def _moe_up_kernel(be_ref, nu_ref, x_ref, wg_ref, wu_ref, o_ref, wgb_ref, wub_ref):
    i = pl.program_id(1)

    @pl.when(_expert_changed(be_ref, i))
    def _():
        wgb_ref[...] = wg_ref[...].astype(BF)
        wub_ref[...] = wu_ref[...].astype(BF)

    @pl.when(i < nu_ref[0])
    def _():
        x = x_ref[...]
        g = jnp.dot(x, wgb_ref[...], preferred_element_type=F32)
        u = jnp.dot(x, wub_ref[...], preferred_element_type=F32)
        o_ref[...] = (_silu(g) * u).astype(o_ref.dtype)

    @pl.when(i >= nu_ref[0])
    def _():
        o_ref[...] = jnp.zeros(o_ref.shape, o_ref.dtype)


def _moe_up(xs, wg, wu, block_expert, n_used, *, bm, bn):
    Rn, K = xs.shape
    N = wg.shape[-1]
    bn = min(bn, N)
    assert Rn % bm == 0 and N % bn == 0
    w_spec = pl.BlockSpec((None, None, K, bn), lambda j, i, be, nu: (0, be[i], 0, j))
    grid_spec = pltpu.PrefetchScalarGridSpec(
        num_scalar_prefetch=2,
        grid=(N // bn, Rn // bm),
        in_specs=[pl.BlockSpec((bm, K), lambda j, i, be, nu: (i, 0)), w_spec, w_spec],
        out_specs=pl.BlockSpec((bm, bn), lambda j, i, be, nu: (i, j)),
        scratch_shapes=[pltpu.VMEM((K, bn), BF), pltpu.VMEM((K, bn), BF)],
    )
    return pl.pallas_call(
        _moe_up_kernel,
        out_shape=jax.ShapeDtypeStruct((Rn, N), BF),
        grid_spec=grid_spec,
        compiler_params=_params(2),
        name="moe_up",
    )(block_expert, n_used, xs, wg, wu)


def _moe_down_kernel(be_ref, nu_ref, x_ref, w_ref, o_ref, wb_ref):
    i = pl.program_id(1)

    @pl.when(_expert_changed(be_ref, i))
    def _():
        wb_ref[...] = w_ref[...].astype(BF)

    @pl.when(i < nu_ref[0])
    def _():
        o_ref[...] = jnp.dot(x_ref[...], wb_ref[...], preferred_element_type=F32)

    @pl.when(i >= nu_ref[0])
    def _():
        o_ref[...] = jnp.zeros(o_ref.shape, o_ref.dtype)


def _moe_down(act, wd, block_expert, n_used, *, bm, bn):
    Rn, K = act.shape
    N = wd.shape[-1]
    bn = min(bn, N)
    assert Rn % bm == 0 and N % bn == 0
    grid_spec = pltpu.PrefetchScalarGridSpec(
        num_scalar_prefetch=2,
        grid=(N // bn, Rn // bm),
        in_specs=[pl.BlockSpec((bm, K), lambda j, i, be, nu: (i, 0)),
                  pl.BlockSpec((None, None, K, bn), lambda j, i, be, nu: (0, be[i], 0, j))],
        out_specs=pl.BlockSpec((bm, bn), lambda j, i, be, nu: (i, j)),
        scratch_shapes=[pltpu.VMEM((K, bn), BF)],
    )
    return pl.pallas_call(
        _moe_down_kernel,
        out_shape=jax.ShapeDtypeStruct((Rn, N), F32),
        grid_spec=grid_spec,
        compiler_params=_params(2),
        name="moe_down",
    )(block_expert, n_used, act, wd)


def _pick_bm(T, target):
    best = ROW_TILE
    for m in range(ROW_TILE, target + 1, ROW_TILE):
        if T % m == 0:
            best = m
    return best


def kernel(x_prompt, x_sample, cache_ckv, cache_kpe, page_table, c_prompt, c_sample, w_ada, b_ada, ln_g, ln_b, a_w_in, a_v_g, a_v_b, a_w_s, a_b_s, a_w_out, b_w_dq, b_q_g, b_w_uq, b_w_qr, b_w_o, kv_w_dkv, kv_g, kv_w_kr, kv_w_uk, kv_w_uv, ffn_w_gate, ffn_w_up, ffn_w_down, moe_w_router, moe_b_router, moe_w_gate, moe_w_up, moe_w_down):
    B, S, D = x_prompt.shape
    Bd, Td, _ = x_sample.shape
    depth = w_ada.shape[0]
    n_a = a_w_in.shape[0]
    assert depth == 2 and n_a == 1 and Td == 1 and Bd == ROW_TILE and S % ROW_TILE == 0
    A = a_w_in.shape[2] // 2
    G = a_w_s.shape[1]
    H, nope = b_w_uq.shape[2], b_w_uq.shape[3]
    R = b_w_qr.shape[3]
    C = kv_w_dkv.shape[1]
    vdim = kv_w_uv.shape[2]
    E = moe_w_router.shape[2]
    n_pages, page_rows = page_table.shape[1], cache_ckv.shape[1]
    past_len = n_pages * page_rows
    alpha = (2.0 * depth) ** 0.25
    scale = (nope + R) ** -0.5
    Tp = B * S
    T = Tp + Bd
    tiles_per_batch = S // ROW_TILE
    n_prompt_tiles = Tp // ROW_TILE
    bm = _pick_bm(T, 640)
    bm_mid = _pick_bm(T, 832)
    bm_big = _pick_bm(T, 1664)

    x = jnp.concatenate([x_prompt.reshape(Tp, D), x_sample.reshape(Bd, D)], axis=0)

    c_tiles = jnp.concatenate([jnp.repeat(c_prompt, ROW_TILE, axis=0), c_sample], axis=0)
    mods = [_matmul(c_tiles, w_ada, bm=c_tiles.shape[0], bn=512, out_dtype=F32, act="silu_in",
                    bias=b_ada[l].reshape(1, -1), lead=(l,), name="adaln") for l in range(depth)]
    SH_M, SC_M, G_M, SH_F, SC_F, G_F = range(6)

    half = R // 2
    inv = ROPE_BASE ** (-jnp.arange(half, dtype=F32) / half)
    pos = jnp.concatenate([jnp.tile(jnp.arange(S, dtype=jnp.int32), B),
                           jnp.full((Bd,), past_len, jnp.int32)]).astype(F32)
    ang = pos[:, None] * inv[None, :]
    cos2 = jnp.concatenate([jnp.cos(ang), jnp.cos(ang)], axis=-1)
    sin2 = jnp.concatenate([jnp.sin(ang), jnp.sin(ang)], axis=-1)
    qscale = scale * math.log2(math.e)
    row_pad = jnp.zeros((128 - R, T), F32)
    cos_qt = jnp.concatenate([cos2.T * qscale, row_pad], axis=0)
    sin_qt = jnp.concatenate([sin2.T * qscale, row_pad], axis=0)

    h = _modulate(x, mods[0], SC_M, SH_M, tiles_per_batch)
    u = _matmul(h, a_w_in, bm=bm_big, bn=512, out_dtype=BF, n_off=0, n_out=A, act="gelu", lead=(0,), name="a_in_u")
    vpre = _matmul(h, a_w_in, bm=bm_big, bn=512, out_dtype=F32, n_off=A, n_out=A, act="gelu", lead=(0,),
                   name="a_in_v")
    tril = jnp.tril(jnp.ones((ROW_TILE, ROW_TILE), bool))
    ws_prompt = jnp.where(tril, a_w_s[0], 0.0)
    ws_sample = a_w_s[0][:, :1, :1] * jnp.eye(ROW_TILE, dtype=F32)
    ws_tiles = jnp.stack([ws_prompt, ws_sample]).astype(BF)
    gd = A // G
    b_prompt = jnp.repeat(a_b_s[0].T, gd, axis=1)
    b_sample = jnp.broadcast_to(b_prompt[:1], b_prompt.shape)
    b_tiles = jnp.stack([b_prompt, b_sample])
    gated, v_sample = _spatial_gate(vpre, u, ws_tiles, b_tiles, a_v_g[0], a_v_b[0], n_prompt_tiles)
    mix = _matmul(gated, a_w_out, bm=bm_mid, bn=512, out_dtype=F32, lead=(0,), name="a_out")
    x, h = _postln(x, mix, mods[0], G_M, ln_g[0, 0], ln_b[0, 0], tiles_per_batch, alpha=alpha,
                   mod_next=mods[0], sc_chunk=SC_F, sh_chunk=SH_F)
    act = _swiglu_up(h, ffn_w_gate, ffn_w_up, bm=bm_big, bn=512, lead=(0,), name="ffn_up")
    f = _matmul(act, ffn_w_down, bm=bm, bn=512, out_dtype=F32, lead=(0,), name="ffn_down")
    x, h, xb = _postln(x, f, mods[0], G_F, ln_g[0, 1], ln_b[0, 1], tiles_per_batch, alpha=alpha,
                       mod_next=mods[1], sc_chunk=SC_M, sh_chunk=SH_M, emit_bf=True)

    w1, w2 = kv_w_kr[:, :half], kv_w_kr[:, half:]
    zpad = jnp.zeros((D, 128 - R), F32)
    wkv = jnp.concatenate([kv_w_dkv, kv_w_kr, zpad, -w2, w1, zpad], axis=1).astype(BF)
    wuk2 = kv_w_uk.reshape(C, H * nope)
    wuv2 = kv_w_uv.reshape(C, H * vdim)
    ckv, kpe, kfull, v_t = _kv_side(xb, wkv, kv_g.reshape(1, C), cos2, sin2, wuk2.astype(BF), wuv2.T.astype(BF),
                                    bm=bm, c_dim=C, r_dim=R, heads=H, nope=nope)

    wqr = b_w_qr[0]
    qpad = jnp.zeros(wqr.shape[:2] + (128 - R,), F32)
    wpe = jnp.concatenate([wqr, qpad], axis=-1).reshape(wqr.shape[0], H * 128)
    wrot = jnp.concatenate([-wqr[..., half:], wqr[..., :half], qpad], axis=-1).reshape(wqr.shape[0], H * 128)
    q_t = _q_side(h, b_w_dq[0].T.astype(BF), b_q_g[0].reshape(-1, 1), b_w_uq[0].reshape(-1, H * nope).T.astype(BF),
                  wpe.T.astype(BF), wrot.T.astype(BF), cos_qt, sin_qt, bm=bm, heads=H, nope=nope, scale=qscale)
    o_all = _flash_attention(kfull, q_t, v_t, batch=B, seq=S, heads=H, vdim=vdim, blk=min(512, S), hb=min(4, H),
                             total_rows=T)

    q_lat_hc = _absorb_query(q_t, wuk2.astype(BF), heads=H, nope=nope, c_dim=C,
                             col_block=Tp // Bd, cols=Bd).reshape(H, C, Bd)
    q_lat = q_lat_hc.transpose(2, 0, 1)
    q_lat_t = jnp.pad(q_lat_hc.transpose(2, 1, 0), ((0, 0), (0, 0), (0, 128 - H)))
    q_pe_s = q_t[:, Tp:].reshape(H, 256, Bd)[:, nope:nope + R, :].transpose(2, 0, 1)
    o_lat = _decode_attention(page_table, q_lat, q_lat_t, q_pe_s, ckv[Tp:].reshape(Bd, 1, C),
                              kpe[Tp:].reshape(Bd, 1, R), cache_ckv, jnp.swapaxes(cache_kpe, 1, 2),
                              pages=min(16, n_pages))
    o_all = _value_up_into(o_all, o_lat.reshape(Bd, H * C), wuv2.astype(BF), heads=H, c_dim=C, vdim=vdim,
                           row_block=Tp // Bd, rows=Bd)
    mix = _matmul(o_all, b_w_o, bm=bm_big, bn=512, out_dtype=F32, lead=(0,), name="attn_out")
    x = _postln(x, mix, mods[1], G_M, ln_g[1, 0], ln_b[1, 0], tiles_per_batch, alpha=alpha)[0]

    wr_pad = jnp.concatenate([moe_w_router[0], jnp.zeros((D, 128 - E), F32)], axis=1)
    br_pad = jnp.concatenate([moe_b_router[0], jnp.zeros((128 - E,), F32)]).reshape(1, 128)
    wts, idx, hf = _router(x, mods[1], SC_F, SH_F, wr_pad, br_pad, tiles_per_batch, E)
    bme = 384
    n_blocks = pl.cdiv(2 * T + E * (bme - 1), bme)
    row_token, dest, block_expert, n_used = _route_tables(idx[:, :2], E, bme, n_blocks)
    hs = _dispatch(hf, row_token)
    act = _moe_up(hs, moe_w_gate, moe_w_up, block_expert, n_used, bm=bme, bn=1024)
    ys = _moe_down(act, moe_w_down, block_expert, n_used, bm=bme, bn=512)
    x = _combine_postln(x, ys, dest, wts, mods[1], G_F, ln_g[1, 1], ln_b[1, 1], tiles_per_batch, alpha=alpha)

    y_prompt = x[:Tp].reshape(B, S, D)
    y_sample = x[Tp:].reshape(Bd, Td, D)
    return (y_prompt, y_sample,
            ckv[:Tp].reshape(B, S, C), kpe[:Tp].reshape(B, S, R),
            ckv[Tp:].reshape(Bd, Td, C), kpe[Tp:].reshape(Bd, Td, R),
            v_sample.reshape(n_a, Bd, Td, A))
```

```python
import functools
import math

import jax
import jax.numpy as jnp
from jax import lax
from jax.experimental import pallas as pl
from jax.experimental.pallas import tpu as pltpu

BF = jnp.bfloat16
F32 = jnp.float32

ROW_TILE = 128
LN_EPS = 1e-5
RMS_EPS = 1e-6
ROPE_BASE = 10000.0
NEG_INF = float("-inf")
VMEM_LIMIT = 56 * 1024 * 1024
DECODE_SLOTS = 3


def _params(n_axes, vmem=VMEM_LIMIT):
    return pltpu.CompilerParams(dimension_semantics=("arbitrary",) * n_axes, vmem_limit_bytes=vmem)


def _gelu_tanh(x):
    return 0.5 * x * (1.0 + jnp.tanh(math.sqrt(2.0 / math.pi) * (x + 0.044715 * x * x * x)))


def _silu(x):
    return x * (1.0 / (1.0 + jnp.exp(-x)))


def _layernorm_rows(x, g, b):
    mu = jnp.mean(x, axis=-1, keepdims=True)
    xc = x - mu
    var = jnp.mean(xc * xc, axis=-1, keepdims=True)
    return xc * lax.rsqrt(var + LN_EPS) * g + b


def _group_starts(bw_ref, i):
    return (i == 0) | (bw_ref[i] != bw_ref[jnp.maximum(i - 1, 0)])


def _wsmm_kernel(bw_ref, nxt_ref, nu_ref, x_ref, *refs, n_w, act, has_bias, bn, n_off, row_chunks):
    w_hbm = refs[:n_w]
    k = n_w
    bias_ref = None
    if has_bias:
        bias_ref = refs[k]
        k += 1
    o_ref = refs[k]
    stage = refs[k + 1:k + 1 + n_w]
    wbf = refs[k + 1 + n_w:k + 1 + 2 * n_w]
    sem, cnt = refs[k + 1 + 2 * n_w], refs[k + 2 + 2 * n_w]
    j = pl.program_id(0)
    i = pl.program_id(1)

    def weight_copies(w_idx, jj, slot):
        col = pl.multiple_of(jj * bn + n_off, 128)
        return [pltpu.make_async_copy(w_hbm[t].at[w_idx, :, pl.ds(col, bn)], stage[t].at[slot], sem.at[t, slot])
                for t in range(n_w)]

    @pl.when((j == 0) & (i == 0))
    def _():
        cnt[0] = 0
        for cp in weight_copies(bw_ref[0], 0, 0):
            cp.start()

    @pl.when(_group_starts(bw_ref, i))
    def _():
        slot = cnt[0] % 2
        for cp in weight_copies(bw_ref[i], j, slot):
            cp.wait()
        for t in range(n_w):
            wbf[t][...] = stage[t][slot].astype(BF)
        more = nxt_ref[i] >= 0
        nw = jnp.where(more, nxt_ref[i], bw_ref[0])
        nj = jnp.where(more, j, j + 1)

        @pl.when(nj < pl.num_programs(0))
        def _():
            for cp in weight_copies(nw, nj, 1 - slot):
                cp.start()

        cnt[0] = cnt[0] + 1

    @pl.when(i < nu_ref[0])
    def _():
        rows = x_ref.shape[0] // row_chunks
        for c in range(row_chunks):
            rs = slice(c * rows, (c + 1) * rows)
            x = x_ref[rs, :]
            if act == "silu_in":
                x = _silu(x.astype(F32))
            x = x.astype(BF)
            acc = jnp.dot(x, wbf[0][...], preferred_element_type=F32)
            if n_w == 2:
                acc = _silu(acc) * jnp.dot(x, wbf[1][...], preferred_element_type=F32)
            if has_bias:
                acc = acc + bias_ref[...]
            if act == "gelu":
                acc = _gelu_tanh(acc)
            o_ref[rs, :] = acc.astype(o_ref.dtype)

    @pl.when(i >= nu_ref[0])
    def _():
        o_ref[...] = jnp.zeros(o_ref.shape, o_ref.dtype)


def _wsmm(x, ws, *, bm, bn, out_dtype, block_w=None, next_w=None, n_used=None, w_index=0, n_off=0,
          n_out=None, act=None, bias=None, name="wsmm"):
    M, K = x.shape
    N = ws[0].shape[-1] if n_out is None else n_out
    bn = min(bn, N)
    assert M % bm == 0 and N % bn == 0 and n_off % bn == 0 and all(w.ndim == 3 and w.shape[1] == K for w in ws)
    n_i = M // bm
    if block_w is None:
        block_w = jnp.full((n_i,), w_index, jnp.int32)
        next_w = jnp.full((n_i,), -1, jnp.int32)
        n_used = jnp.full((1,), n_i, jnp.int32)
    n_w = len(ws)
    offb = n_off // bn
    row_chunks = max(c for c in (1, 2, 4) if bm % (16 * c) == 0 and (c == 1 or bm // c >= 384))
    in_specs = [pl.BlockSpec((bm, K), lambda j, i, *_: (i, 0))] + [pl.BlockSpec(memory_space=pl.ANY)] * n_w
    args = [x, *ws]
    if bias is not None:
        in_specs.append(pl.BlockSpec((1, bn), lambda j, i, *_: (0, j + offb)))
        args.append(bias)
    grid_spec = pltpu.PrefetchScalarGridSpec(
        num_scalar_prefetch=3,
        grid=(N // bn, n_i),
        in_specs=in_specs,
        out_specs=pl.BlockSpec((bm, bn), lambda j, i, *_: (i, j)),
        scratch_shapes=[pltpu.VMEM((2, K, bn), ws[0].dtype)] * n_w + [pltpu.VMEM((K, bn), BF)] * n_w
        + [pltpu.SemaphoreType.DMA((n_w, 2)), pltpu.SMEM((1,), jnp.int32)],
    )
    return pl.pallas_call(
        functools.partial(_wsmm_kernel, n_w=n_w, act=act, has_bias=bias is not None, bn=bn, n_off=n_off,
                          row_chunks=row_chunks),
        out_shape=jax.ShapeDtypeStruct((M, N), out_dtype),
        grid_spec=grid_spec,
        compiler_params=_params(2),
        name=name,
    )(block_w, next_w, n_used, *args)


def _mod_spec(tiles_per_batch, D, chunk):
    return pl.BlockSpec((ROW_TILE, D), lambda i: (i // tiles_per_batch, chunk))


def _modulate_kernel(x_ref, sc_ref, sh_ref, h_ref):
    h_ref[...] = (x_ref[...] * (1.0 + sc_ref[...]) + sh_ref[...]).astype(h_ref.dtype)


def _modulate(x, mod, sc_chunk, sh_chunk, tiles_per_batch):
    T, D = x.shape
    row = pl.BlockSpec((ROW_TILE, D), lambda i: (i, 0))
    return pl.pallas_call(
        _modulate_kernel,
        out_shape=jax.ShapeDtypeStruct((T, D), BF),
        grid=(T // ROW_TILE,),
        in_specs=[row, _mod_spec(tiles_per_batch, D, sc_chunk), _mod_spec(tiles_per_batch, D, sh_chunk)],
        out_specs=row,
        compiler_params=_params(1),
        name="modulate",
    )(x, mod, mod)


def _postln_kernel(*refs, alpha, modulate_next, emit_bf):
    x_ref, mix_ref, g_ref, lng_ref, lnb_ref = refs[:5]
    k = 5
    if modulate_next:
        sc_ref, sh_ref = refs[k], refs[k + 1]
        k += 2
    xo_ref = refs[k]; k += 1
    xn = _layernorm_rows(alpha * x_ref[...] + g_ref[...] * mix_ref[...], lng_ref[...], lnb_ref[...])
    xo_ref[...] = xn
    if modulate_next:
        refs[k][...] = (xn * (1.0 + sc_ref[...]) + sh_ref[...]).astype(BF)
        k += 1
    if emit_bf:
        refs[k][...] = xn.astype(BF)


def _postln(x, mix, mod_g, g_chunk, ln_g, ln_b, tiles_per_batch, *, alpha, mod_next=None,
            sc_chunk=None, sh_chunk=None, emit_bf=False):
    T, D = x.shape
    row = pl.BlockSpec((ROW_TILE, D), lambda i: (i, 0))
    vec = pl.BlockSpec((1, D), lambda i: (0, 0))
    in_specs = [row, row, _mod_spec(tiles_per_batch, D, g_chunk), vec, vec]
    args = [x, mix, mod_g, ln_g.reshape(1, D), ln_b.reshape(1, D)]
    out_shape = [jax.ShapeDtypeStruct((T, D), F32)]
    out_specs = [row]
    if mod_next is not None:
        in_specs += [_mod_spec(tiles_per_batch, D, sc_chunk), _mod_spec(tiles_per_batch, D, sh_chunk)]
        args += [mod_next, mod_next]
        out_shape.append(jax.ShapeDtypeStruct((T, D), BF))
        out_specs.append(row)
    if emit_bf:
        out_shape.append(jax.ShapeDtypeStruct((T, D), BF))
        out_specs.append(row)
    return pl.pallas_call(
        functools.partial(_postln_kernel, alpha=alpha, modulate_next=mod_next is not None, emit_bf=emit_bf),
        out_shape=out_shape,
        grid=(T // ROW_TILE,),
        in_specs=in_specs,
        out_specs=out_specs,
        compiler_params=_params(1),
        name="postln",
    )(*args)


def _spatial_gate_kernel(vpre_ref, u_ref, ws_ref, b_ref, vg_ref, vb_ref, mix_ref, vout_ref, *, groups):
    v = _layernorm_rows(vpre_ref[...], vg_ref[...], vb_ref[...])

    @pl.when(pl.program_id(0) == pl.num_programs(0) - 1)
    def _():
        vout_ref[...] = v

    vb16 = v.astype(BF)
    gd = v.shape[1] // groups
    for g in range(groups):
        cols = slice(g * gd, (g + 1) * gd)
        s = jnp.dot(ws_ref[g], vb16[:, cols], preferred_element_type=F32) + b_ref[:, cols]
        mix_ref[:, cols] = (u_ref[:, cols].astype(F32) * s).astype(BF)


def _spatial_gate(vpre, u, ws_tiles, b_tiles, v_g, v_b, n_prompt_tiles):
    T, A = vpre.shape
    G = ws_tiles.shape[1]
    row = pl.BlockSpec((ROW_TILE, A), lambda i: (i, 0))
    vec = pl.BlockSpec((1, A), lambda i: (0, 0))
    return pl.pallas_call(
        functools.partial(_spatial_gate_kernel, groups=G),
        out_shape=[jax.ShapeDtypeStruct((T, A), BF), jax.ShapeDtypeStruct((ROW_TILE, A), F32)],
        grid=(T // ROW_TILE,),
        in_specs=[row, row,
                  pl.BlockSpec((None, G, ROW_TILE, ROW_TILE), lambda i: (i // n_prompt_tiles, 0, 0, 0)),
                  pl.BlockSpec((None, ROW_TILE, A), lambda i: (i // n_prompt_tiles, 0, 0)),
                  vec, vec],
        out_specs=[row, pl.BlockSpec((ROW_TILE, A), lambda i: (0, 0))],
        compiler_params=_params(1),
        name="spatial_gate",
    )(vpre, u, ws_tiles, b_tiles, v_g.reshape(1, A), v_b.reshape(1, A))


_CONTRACT_LAST = (((1,), (1,)), ((), ()))


def _kv_kernel(x_ref, wkv_ref, kvg_ref, cos_ref, sin_ref, wuk_ref, wuvt_ref,
               ckv_ref, kpe_ref, kfull_ref, vt_ref, *, c_dim, r_dim, heads, nope):
    y = jnp.dot(x_ref[...], wkv_ref[...], preferred_element_type=F32)
    c = y[:, :c_dim]
    ckv = c * lax.rsqrt(jnp.mean(c * c, axis=-1, keepdims=True) + RMS_EPS) * kvg_ref[...]
    ckv_ref[...] = ckv
    kpe = y[:, c_dim:c_dim + r_dim] * cos_ref[...] + y[:, c_dim + 128:c_dim + 128 + r_dim] * sin_ref[...]
    kpe_ref[...] = kpe
    cb = ckv.astype(BF)
    kn = jnp.dot(cb, wuk_ref[...], preferred_element_type=F32)
    vt_ref[...] = lax.dot_general(wuvt_ref[...], cb, _CONTRACT_LAST, preferred_element_type=F32).astype(BF)
    kpe_pad = jnp.concatenate([kpe, jnp.zeros((kpe.shape[0], 128 - r_dim), F32)], axis=-1).astype(BF)
    for h in range(heads):
        kfull_ref[:, h * 256:h * 256 + nope] = kn[:, h * nope:(h + 1) * nope].astype(BF)
        kfull_ref[:, h * 256 + nope:(h + 1) * 256] = kpe_pad


def _kv_side(xb, wkv, kv_g, cos_k, sin_k, wuk, wuv_t, *, bm, c_dim, r_dim, heads, nope):
    T, D = xb.shape
    assert nope == 128 and r_dim <= 128
    row = lambda w: pl.BlockSpec((bm, w), lambda i: (i, 0))
    full = lambda a: pl.BlockSpec(a.shape, lambda i: (0,) * a.ndim)
    hv = wuv_t.shape[0]
    return pl.pallas_call(
        functools.partial(_kv_kernel, c_dim=c_dim, r_dim=r_dim, heads=heads, nope=nope),
        out_shape=[jax.ShapeDtypeStruct((T, c_dim), F32), jax.ShapeDtypeStruct((T, r_dim), F32),
                   jax.ShapeDtypeStruct((T, heads * 256), BF), jax.ShapeDtypeStruct((hv, T), BF)],
        grid=(T // bm,),
        in_specs=[row(D), full(wkv), full(kv_g), row(r_dim), row(r_dim), full(wuk), full(wuv_t)],
        out_specs=[row(c_dim), row(r_dim), row(heads * 256), pl.BlockSpec((hv, bm), lambda i: (0, i))],
        compiler_params=_params(1),
        name="kv_side",
    )(xb, wkv, kv_g, cos_k, sin_k, wuk, wuv_t)


def _q_kernel(h_ref, wdqt_ref, qg_ref, wuqt_ref, wpet_ref, wrott_ref, cost_ref, sint_ref, qt_ref,
              *, heads, nope, scale):
    ct = lax.dot_general(wdqt_ref[...], h_ref[...], _CONTRACT_LAST, preferred_element_type=F32)
    cqt = (ct * lax.rsqrt(jnp.mean(ct * ct, axis=0, keepdims=True) + RMS_EPS) * qg_ref[...]).astype(BF)
    qn = jnp.dot(wuqt_ref[...], cqt, preferred_element_type=F32) * scale
    qp = jnp.dot(wpet_ref[...], cqt, preferred_element_type=F32)
    qr = jnp.dot(wrott_ref[...], cqt, preferred_element_type=F32)
    cos = cost_ref[...]
    sin = sint_ref[...]
    for h in range(heads):
        qt_ref[h * 256:h * 256 + nope, :] = qn[h * nope:(h + 1) * nope].astype(BF)
        pe = qp[h * 128:(h + 1) * 128] * cos + qr[h * 128:(h + 1) * 128] * sin
        qt_ref[h * 256 + nope:(h + 1) * 256, :] = pe.astype(BF)


def _q_side(hb, wdq_t, q_g_col, wuq_t, wpe_t, wrot_t, cos_t, sin_t, *, bm, heads, nope, scale):
    T, D = hb.shape
    assert nope == 128
    full = lambda a: pl.BlockSpec(a.shape, lambda i: (0,) * a.ndim)
    col = lambda r: pl.BlockSpec((r, bm), lambda i: (0, i))
    return pl.pallas_call(
        functools.partial(_q_kernel, heads=heads, nope=nope, scale=scale),
        out_shape=jax.ShapeDtypeStruct((heads * 256, T), BF),
        grid=(T // bm,),
        in_specs=[pl.BlockSpec((bm, D), lambda i: (i, 0)), full(wdq_t), full(q_g_col), full(wuq_t),
                  full(wpe_t), full(wrot_t), col(128), col(128)],
        out_specs=col(heads * 256),
        compiler_params=_params(1),
        name="q_side",
    )(hb, wdq_t, q_g_col, wuq_t, wpe_t, wrot_t, cos_t, sin_t)


def _flash_kernel(qi_ref, ki_ref, k_ref, qt_ref, vt_ref, o_init_ref, o_ref, m_ref, l_ref, acc_ref, *, hb, vdim):
    del o_init_ref
    step = pl.program_id(2)
    qi = qi_ref[step]
    ki = ki_ref[step]

    @pl.when(ki == 0)
    def _():
        m_ref[...] = jnp.full(m_ref.shape, NEG_INF, F32)
        l_ref[...] = jnp.zeros(l_ref.shape, F32)
        acc_ref[...] = jnp.zeros(acc_ref.shape, F32)

    def update(masked):
        for h in range(hb):
            qr = slice(h * 256, (h + 1) * 256)
            vr = slice(h * vdim, (h + 1) * vdim)
            st = jnp.dot(k_ref[:, qr], qt_ref[qr, :], preferred_element_type=F32)
            if masked:
                key = lax.broadcasted_iota(jnp.int32, st.shape, 0)
                qry = lax.broadcasted_iota(jnp.int32, st.shape, 1)
                st = jnp.where(key <= qry, st, NEG_INF)
            m_prev = m_ref[h]
            m_new = jnp.maximum(m_prev, jnp.max(st, axis=0, keepdims=True))
            corr = jnp.exp2(m_prev - m_new)
            pt = jnp.exp2(st - m_new)
            l_ref[h] = corr * l_ref[h] + jnp.sum(pt, axis=0, keepdims=True)
            acc_ref[vr, :] = corr * acc_ref[vr, :] + jnp.dot(vt_ref[vr, :], pt.astype(BF),
                                                             preferred_element_type=F32)
            m_ref[h] = m_new

    @pl.when(ki < qi)
    def _():
        update(False)

    @pl.when(ki == qi)
    def _():
        update(True)
        for h in range(hb):
            vr = slice(h * vdim, (h + 1) * vdim)
            o_ref[:, vr] = (acc_ref[vr, :] / l_ref[h]).T.astype(o_ref.dtype)


def _flash_attention(kfull, q_t, v_t, *, batch, seq, heads, vdim, blk, hb, total_rows):
    assert heads % hb == 0
    nb = seq // blk
    pairs = [(a, b) for a in range(nb) for b in range(a + 1)]
    qi_map = jnp.asarray([p[0] for p in pairs], jnp.int32)
    ki_map = jnp.asarray([p[1] for p in pairs], jnp.int32)
    grid_spec = pltpu.PrefetchScalarGridSpec(
        num_scalar_prefetch=2,
        grid=(batch, heads // hb, len(pairs)),
        in_specs=[
            pl.BlockSpec((blk, hb * 256), lambda b, h, s, qi, ki: (b * nb + ki[s], h)),
            pl.BlockSpec((hb * 256, blk), lambda b, h, s, qi, ki: (h, b * nb + qi[s])),
            pl.BlockSpec((hb * vdim, blk), lambda b, h, s, qi, ki: (h, b * nb + ki[s])),
            pl.BlockSpec(memory_space=pl.ANY),
        ],
        out_specs=pl.BlockSpec((blk, hb * vdim), lambda b, h, s, qi, ki: (b * nb + qi[s], h)),
        scratch_shapes=[pltpu.VMEM((hb, 1, blk), F32), pltpu.VMEM((hb, 1, blk), F32),
                        pltpu.VMEM((hb * vdim, blk), F32)],
    )
    return pl.pallas_call(
        functools.partial(_flash_kernel, hb=hb, vdim=vdim),
        out_shape=jax.ShapeDtypeStruct((total_rows, heads * vdim), BF),
        grid_spec=grid_spec,
        input_output_aliases={5: 0},
        compiler_params=_params(3),
        name="flash_prompt",
    )(qi_map, ki_map, kfull, q_t, v_t, jnp.zeros((total_rows, heads * vdim), BF))


def _absorb_kernel(wuk_ref, qt_ref, o_ref):
    o_ref[...] = jnp.dot(wuk_ref[...], qt_ref[...], preferred_element_type=F32).astype(o_ref.dtype)


def _absorb_query(q_t, wuk, *, heads, nope, c_dim, col_block, cols):
    return pl.pallas_call(
        _absorb_kernel,
        out_shape=jax.ShapeDtypeStruct((heads * c_dim, cols), BF),
        grid=(heads,),
        in_specs=[pl.BlockSpec((c_dim, nope), lambda h: (0, h)),
                  pl.BlockSpec((nope, cols), lambda h: (2 * h, col_block))],
        out_specs=pl.BlockSpec((c_dim, cols), lambda h: (h, 0)),
        compiler_params=_params(1),
        name="absorb_query",
    )(wuk, q_t)


def _page_copies(pt_ref, ck_hbm, kp_hbm, ck_buf, kp_buf, sem, step, slot, i, pages):
    page = pt_ref[step * pages + i]
    return (pltpu.make_async_copy(ck_hbm.at[page], ck_buf.at[slot, i], sem.at[0, slot]),
            pltpu.make_async_copy(kp_hbm.at[page], kp_buf.at[slot, i], sem.at[1, slot]))


def _decode_kernel(pt_ref, qlat_ref, qlt_ref, qpe_ref, cnew_ref, knew_ref, ck_hbm, kp_hbm, o_ref,
                   ck_buf, kp_buf, sem, m_ref, l_ref, acc_ref, *, pages, steps_per_seq):
    n = pl.program_id(0)
    j = n % steps_per_seq
    slot = n % DECODE_SLOTS

    def start_step(step):
        for i in range(pages):
            for cp in _page_copies(pt_ref, ck_hbm, kp_hbm, ck_buf, kp_buf, sem, step, step % DECODE_SLOTS, i, pages):
                cp.start()

    @pl.when(n == 0)
    def _():
        for step in range(DECODE_SLOTS - 1):
            @pl.when(step < pl.num_programs(0))
            def _():
                start_step(step)

    @pl.when(n + DECODE_SLOTS - 1 < pl.num_programs(0))
    def _():
        start_step(n + DECODE_SLOTS - 1)

    @pl.when(j == 0)
    def _():
        m_ref[...] = jnp.full(m_ref.shape, NEG_INF, F32)
        l_ref[...] = jnp.zeros(l_ref.shape, F32)
        acc_ref[...] = jnp.zeros(acc_ref.shape, F32)

    for i in range(pages):
        for cp in _page_copies(pt_ref, ck_hbm, kp_hbm, ck_buf, kp_buf, sem, n, slot, i, pages):
            cp.wait()

    ql = qlat_ref[0]
    qlt = qlt_ref[0]
    qp = qpe_ref[0]
    heads = ql.shape[0]
    cks = []
    scores = []
    for i in range(pages):
        ck = ck_buf[slot, i].astype(BF)
        kp = kp_buf[slot, i].astype(BF)
        cks.append(ck)
        st = jnp.dot(ck, qlt, preferred_element_type=F32)
        st = jnp.where(lax.broadcasted_iota(jnp.int32, st.shape, 1) < heads, st, 0.0)
        scores.append(st.T[:heads] + jnp.dot(qp, kp, preferred_element_type=F32))
    s = jnp.concatenate(scores, axis=-1)
    m_prev = m_ref[...]
    m_new = jnp.maximum(m_prev, jnp.max(s, axis=-1, keepdims=True))
    corr = jnp.exp2(m_prev - m_new)
    p = jnp.exp2(s - m_new)
    l_new = corr * l_ref[...] + jnp.sum(p, axis=-1, keepdims=True)
    pb = p.astype(BF)
    rows = cks[0].shape[0]
    pv = jnp.dot(pb[:, :rows], cks[0], preferred_element_type=F32)
    for i in range(1, pages):
        pv = pv + jnp.dot(pb[:, i * rows:(i + 1) * rows], cks[i], preferred_element_type=F32)
    acc_new = corr * acc_ref[...] + pv
    m_ref[...] = m_new
    l_ref[...] = l_new
    acc_ref[...] = acc_new

    @pl.when(j == steps_per_seq - 1)
    def _():
        cn = cnew_ref[0].astype(BF).astype(F32)
        kn = knew_ref[0].astype(BF).astype(F32)
        s_self = (jnp.sum(ql.astype(F32) * cn, axis=-1, keepdims=True)
                  + jnp.sum(qp.astype(F32) * kn, axis=-1, keepdims=True))
        m_fin = jnp.maximum(m_new, s_self)
        c2 = jnp.exp2(m_new - m_fin)
        p_self = jnp.exp2(s_self - m_fin)
        l_fin = c2 * l_new + p_self
        acc_fin = c2 * acc_new + p_self.astype(BF).astype(F32) * cn
        o_ref[0] = (acc_fin / l_fin).astype(o_ref.dtype)


def _decode_attention(page_table, q_lat, q_lat_t, q_pe, ckv_new, kpe_new, cache_ckv, cache_kpe_t, *, pages):
    Bd, H, C = q_lat.shape
    R = q_pe.shape[2]
    n_pages = page_table.shape[1]
    page_rows = cache_ckv.shape[1]
    assert n_pages % pages == 0
    sps = n_pages // pages
    per_b = lambda *shape: pl.BlockSpec((1,) + shape, lambda n, pt: (n // sps, 0, 0))
    hbm = pl.BlockSpec(memory_space=pl.ANY)
    grid_spec = pltpu.PrefetchScalarGridSpec(
        num_scalar_prefetch=1,
        grid=(Bd * sps,),
        in_specs=[per_b(H, C), per_b(C, 128), per_b(H, R), per_b(1, C), per_b(1, R), hbm, hbm],
        out_specs=per_b(H, C),
        scratch_shapes=[pltpu.VMEM((DECODE_SLOTS, pages, page_rows, C), cache_ckv.dtype),
                        pltpu.VMEM((DECODE_SLOTS, pages, R, page_rows), cache_kpe_t.dtype),
                        pltpu.SemaphoreType.DMA((2, DECODE_SLOTS)),
                        pltpu.VMEM((H, 1), F32), pltpu.VMEM((H, 1), F32), pltpu.VMEM((H, C), F32)],
    )
    return pl.pallas_call(
        functools.partial(_decode_kernel, pages=pages, steps_per_seq=sps),
        out_shape=jax.ShapeDtypeStruct((Bd, H, C), BF),
        grid_spec=grid_spec,
        compiler_params=_params(1),
        name="decode_attention",
    )(page_table.reshape(-1), q_lat, q_lat_t, q_pe, ckv_new, kpe_new, cache_ckv, cache_kpe_t)


def _value_up_kernel(o_hbm_ref, olat_ref, wuv_ref, o_ref):
    del o_hbm_ref
    o_ref[...] = jnp.dot(olat_ref[...], wuv_ref[...], preferred_element_type=F32).astype(o_ref.dtype)


def _value_up_into(o_all, o_lat, wuv, *, heads, c_dim, vdim, row_block, rows):
    return pl.pallas_call(
        _value_up_kernel,
        out_shape=jax.ShapeDtypeStruct(o_all.shape, o_all.dtype),
        grid=(heads,),
        in_specs=[pl.BlockSpec(memory_space=pl.ANY),
                  pl.BlockSpec((rows, c_dim), lambda h: (0, h)),
                  pl.BlockSpec((c_dim, vdim), lambda h: (0, h))],
        out_specs=pl.BlockSpec((rows, vdim), lambda h: (row_block, h)),
        input_output_aliases={0: 0},
        compiler_params=_params(1),
        name="value_up",
    )(o_all, o_lat, wuv)


def _router_kernel(x_ref, sc_ref, sh_ref, wr_ref, br_ref, w_ref, i_ref, h_ref, *, n_exp):
    h = x_ref[...] * (1.0 + sc_ref[...]) + sh_ref[...]
    logits = jnp.dot(h, wr_ref[...], preferred_element_type=F32, precision=lax.Precision.HIGHEST) + br_ref[...]
    lane = lax.broadcasted_iota(jnp.int32, logits.shape, 1)
    valid = lane < n_exp
    logits = jnp.where(valid, logits, NEG_INF)
    mx = jnp.max(logits, axis=-1, keepdims=True)
    e = jnp.exp(logits - mx)
    probs = e / jnp.sum(e, axis=-1, keepdims=True)
    big = jnp.int32(1 << 30)
    p1 = jnp.max(probs, axis=-1, keepdims=True)
    i1 = jnp.min(jnp.where((probs == p1) & valid, lane, big), axis=-1, keepdims=True)
    rest = jnp.where((lane == i1) | ~valid, -1.0, probs)
    p2 = jnp.max(rest, axis=-1, keepdims=True)
    i2 = jnp.min(jnp.where(rest == p2, lane, big), axis=-1, keepdims=True)
    denom = p1 + p2
    w_ref[...] = jnp.where(lane == 0, p1 / denom, jnp.where(lane == 1, p2 / denom, 0.0))
    i_ref[...] = jnp.where(lane == 0, i1, jnp.where(lane == 1, i2, 0))
    h_ref[...] = h


def _router(x, mod, sc_chunk, sh_chunk, wr_pad, br_pad, tiles_per_batch, n_exp):
    T, D = x.shape
    row = pl.BlockSpec((ROW_TILE, D), lambda i: (i, 0))
    sel = pl.BlockSpec((ROW_TILE, 128), lambda i: (i, 0))
    return pl.pallas_call(
        functools.partial(_router_kernel, n_exp=n_exp),
        out_shape=[jax.ShapeDtypeStruct((T, 128), F32), jax.ShapeDtypeStruct((T, 128), jnp.int32),
                   jax.ShapeDtypeStruct((T, D), F32)],
        grid=(T // ROW_TILE,),
        in_specs=[row, _mod_spec(tiles_per_batch, D, sc_chunk), _mod_spec(tiles_per_batch, D, sh_chunk),
                  pl.BlockSpec((D, 128), lambda i: (0, 0)), pl.BlockSpec((1, 128), lambda i: (0, 0))],
        out_specs=[sel, sel, row],
        compiler_params=_params(1),
        name="router",
    )(x, mod, mod, wr_pad, br_pad)


def _route_tables(idx2, n_exp, bm, n_blocks):
    T = idx2.shape[0]
    e_flat = idx2.reshape(-1)
    onehot = (e_flat[:, None] == jnp.arange(n_exp, dtype=jnp.int32)[None, :]).astype(jnp.int32)
    rank = jnp.sum((jnp.cumsum(onehot, axis=0) - onehot) * onehot, axis=1)
    counts = jnp.sum(onehot, axis=0)
    blocks_per = (counts + bm - 1) // bm
    blocks_end = jnp.cumsum(blocks_per)
    dest = (blocks_end - blocks_per)[e_flat] * bm + rank
    n_used = blocks_end[-1]
    b = jnp.minimum(jnp.arange(n_blocks, dtype=jnp.int32), n_used - 1)
    block_expert = jnp.sum((b[:, None] >= blocks_end[None, :]).astype(jnp.int32), axis=1)
    row_token = jnp.zeros((n_blocks * bm,), jnp.int32).at[dest].set(jnp.arange(2 * T, dtype=jnp.int32) // 2)
    ids = jnp.arange(n_exp, dtype=jnp.int32)
    later_used = (ids[None, :] > ids[:, None]) & (blocks_per[None, :] > 0)
    next_of_expert = jnp.min(jnp.where(later_used, ids[None, :], n_exp), axis=1)
    next_expert = jnp.where(next_of_expert < n_exp, next_of_expert, -1)[block_expert]
    return (row_token, dest.astype(jnp.int32), block_expert.astype(jnp.int32), next_expert.astype(jnp.int32),
            n_used.reshape(1).astype(jnp.int32))


def _row_copy(src_hbm, row, dst, sem):
    return pltpu.make_async_copy(src_hbm.at[pl.ds(row, 1)], dst, sem)


def _dispatch_kernel(tok_ref, h_hbm, o_ref, buf, sem, *, rows):
    i = pl.program_id(0)
    slot = i % 2

    def issue(step, s):
        def body(r, c):
            _row_copy(h_hbm, tok_ref[step * rows + r], buf.at[s, pl.ds(r, 1)], sem.at[s]).start()
            return c
        lax.fori_loop(0, rows, body, 0, unroll=8)

    @pl.when(i == 0)
    def _():
        issue(0, 0)

    @pl.when(i + 1 < pl.num_programs(0))
    def _():
        issue(i + 1, 1 - slot)

    def wait_body(r, c):
        _row_copy(h_hbm, 0, buf.at[slot, pl.ds(r, 1)], sem.at[slot]).wait()
        return c
    lax.fori_loop(0, rows, wait_body, 0, unroll=8)
    o_ref[...] = buf[slot].astype(o_ref.dtype)


def _dispatch(h, row_token, *, rows=128):
    T, D = h.shape
    Rn = row_token.shape[0]
    assert Rn % rows == 0
    grid_spec = pltpu.PrefetchScalarGridSpec(
        num_scalar_prefetch=1,
        grid=(Rn // rows,),
        in_specs=[pl.BlockSpec(memory_space=pl.ANY)],
        out_specs=pl.BlockSpec((rows, D), lambda i, tok: (i, 0)),
        scratch_shapes=[pltpu.VMEM((2, rows, D), h.dtype), pltpu.SemaphoreType.DMA((2,))],
    )
    return pl.pallas_call(
        functools.partial(_dispatch_kernel, rows=rows),
        out_shape=jax.ShapeDtypeStruct((Rn, D), BF),
        grid_spec=grid_spec,
        compiler_params=_params(1),
        name="moe_dispatch",
    )(row_token, h)


def _combine_postln_kernel(dest_ref, x_ref, g_ref, lng_ref, lnb_ref, w_ref, y_hbm, xo_ref, buf, sem, *, alpha, rows):
    i = pl.program_id(0)
    slot = i % 2

    def issue(step, s):
        def body(r, c):
            a = (step * rows + r) * 2
            _row_copy(y_hbm, dest_ref[a], buf.at[s, 0, pl.ds(r, 1)], sem.at[s]).start()
            _row_copy(y_hbm, dest_ref[a + 1], buf.at[s, 1, pl.ds(r, 1)], sem.at[s]).start()
            return c
        lax.fori_loop(0, rows, body, 0, unroll=8)

    @pl.when(i == 0)
    def _():
        issue(0, 0)

    @pl.when(i + 1 < pl.num_programs(0))
    def _():
        issue(i + 1, 1 - slot)

    def wait_body(r, c):
        _row_copy(y_hbm, 0, buf.at[slot, 0, pl.ds(r, 1)], sem.at[slot]).wait()
        _row_copy(y_hbm, 0, buf.at[slot, 1, pl.ds(r, 1)], sem.at[slot]).wait()
        return c
    lax.fori_loop(0, rows, wait_body, 0, unroll=8)
    f = w_ref[:, 0:1] * buf[slot, 0] + w_ref[:, 1:2] * buf[slot, 1]
    xo_ref[...] = _layernorm_rows(alpha * x_ref[...] + g_ref[...] * f, lng_ref[...], lnb_ref[...])


def _combine_postln(x, y, dest, wts, mod_g, g_chunk, ln_g, ln_b, tiles_per_batch, *, alpha):
    T, D = x.shape
    rows = ROW_TILE
    row = pl.BlockSpec((rows, D), lambda i, d: (i, 0))
    vec = pl.BlockSpec((1, D), lambda i, d: (0, 0))
    grid_spec = pltpu.PrefetchScalarGridSpec(
        num_scalar_prefetch=1,
        grid=(T // rows,),
        in_specs=[row, pl.BlockSpec((rows, D), lambda i, d: (i // tiles_per_batch, g_chunk)), vec, vec,
                  pl.BlockSpec((rows, 128), lambda i, d: (i, 0)), pl.BlockSpec(memory_space=pl.ANY)],
        out_specs=row,
        scratch_shapes=[pltpu.VMEM((2, 2, rows, D), F32), pltpu.SemaphoreType.DMA((2,))],
    )
    return pl.pallas_call(
        functools.partial(_combine_postln_kernel, alpha=alpha, rows=rows),
        out_shape=jax.ShapeDtypeStruct((T, D), F32),
        grid_spec=grid_spec,
        compiler_params=_params(1),
        name="moe_combine_postln",
    )(dest, x, mod_g, ln_g.reshape(1, D), ln_b.reshape(1, D), wts, y)


def _pick_bm(T, target):
    best = ROW_TILE
    for m in range(ROW_TILE, target + 1, ROW_TILE):
        if T % m == 0:
            best = m
    return best


def kernel(x_prompt, x_sample, cache_ckv, cache_kpe, page_table, c_prompt, c_sample, w_ada, b_ada, ln_g, ln_b, a_w_in, a_v_g, a_v_b, a_w_s, a_b_s, a_w_out, b_w_dq, b_q_g, b_w_uq, b_w_qr, b_w_o, kv_w_dkv, kv_g, kv_w_kr, kv_w_uk, kv_w_uv, ffn_w_gate, ffn_w_up, ffn_w_down, moe_w_router, moe_b_router, moe_w_gate, moe_w_up, moe_w_down):
    B, S, D = x_prompt.shape
    Bd, Td, _ = x_sample.shape
    depth = w_ada.shape[0]
    n_a = a_w_in.shape[0]
    assert depth == 2 and n_a == 1 and Td == 1 and Bd == ROW_TILE and S % ROW_TILE == 0
    A = a_w_in.shape[2] // 2
    G = a_w_s.shape[1]
    H, nope = b_w_uq.shape[2], b_w_uq.shape[3]
    R = b_w_qr.shape[3]
    C = kv_w_dkv.shape[1]
    vdim = kv_w_uv.shape[2]
    E = moe_w_router.shape[2]
    n_pages, page_rows = page_table.shape[1], cache_ckv.shape[1]
    past_len = n_pages * page_rows
    alpha = (2.0 * depth) ** 0.25
    scale = (nope + R) ** -0.5
    Tp = B * S
    T = Tp + Bd
    tiles_per_batch = S // ROW_TILE
    n_prompt_tiles = Tp // ROW_TILE
    bm = _pick_bm(T, 640)
    bm_mid = _pick_bm(T, 832)
    bm_big = _pick_bm(T, 1664)

    x = jnp.concatenate([x_prompt.reshape(Tp, D), x_sample.reshape(Bd, D)], axis=0)

    c_tiles = jnp.concatenate([jnp.repeat(c_prompt, ROW_TILE, axis=0), c_sample], axis=0)
    mods = [_wsmm(c_tiles, [w_ada], bm=c_tiles.shape[0], bn=512, out_dtype=F32, act="silu_in",
                  bias=b_ada[l].reshape(1, -1), w_index=l, name="adaln") for l in range(depth)]
    SH_M, SC_M, G_M, SH_F, SC_F, G_F = range(6)

    half = R // 2
    inv = ROPE_BASE ** (-jnp.arange(half, dtype=F32) / half)
    pos = jnp.concatenate([jnp.tile(jnp.arange(S, dtype=jnp.int32), B),
                           jnp.full((Bd,), past_len, jnp.int32)]).astype(F32)
    ang = pos[:, None] * inv[None, :]
    cos2 = jnp.concatenate([jnp.cos(ang), jnp.cos(ang)], axis=-1)
    sin2 = jnp.concatenate([jnp.sin(ang), jnp.sin(ang)], axis=-1)
    qscale = scale * math.log2(math.e)
    row_pad = jnp.zeros((128 - R, T), F32)
    cos_qt = jnp.concatenate([cos2.T * qscale, row_pad], axis=0)
    sin_qt = jnp.concatenate([sin2.T * qscale, row_pad], axis=0)

    h = _modulate(x, mods[0], SC_M, SH_M, tiles_per_batch)
    u = _wsmm(h, [a_w_in], bm=bm_big, bn=512, out_dtype=BF, n_off=0, n_out=A, act="gelu", name="a_in_u")
    vpre = _wsmm(h, [a_w_in], bm=bm_big, bn=512, out_dtype=F32, n_off=A, n_out=A, act="gelu", name="a_in_v")
    tril = jnp.tril(jnp.ones((ROW_TILE, ROW_TILE), bool))
    ws_prompt = jnp.where(tril, a_w_s[0], 0.0)
    ws_sample = a_w_s[0][:, :1, :1] * jnp.eye(ROW_TILE, dtype=F32)
    ws_tiles = jnp.stack([ws_prompt, ws_sample]).astype(BF)
    gd = A // G
    b_prompt = jnp.repeat(a_b_s[0].T, gd, axis=1)
    b_sample = jnp.broadcast_to(b_prompt[:1], b_prompt.shape)
    b_tiles = jnp.stack([b_prompt, b_sample])
    gated, v_sample = _spatial_gate(vpre, u, ws_tiles, b_tiles, a_v_g[0], a_v_b[0], n_prompt_tiles)
    mix = _wsmm(gated, [a_w_out], bm=bm_mid, bn=512, out_dtype=F32, name="a_out")
    x, h = _postln(x, mix, mods[0], G_M, ln_g[0, 0], ln_b[0, 0], tiles_per_batch, alpha=alpha,
                   mod_next=mods[0], sc_chunk=SC_F, sh_chunk=SH_F)
    act = _wsmm(h, [ffn_w_gate, ffn_w_up], bm=bm_big, bn=512, out_dtype=BF, name="ffn_up")
    f = _wsmm(act, [ffn_w_down], bm=bm, bn=512, out_dtype=F32, name="ffn_down")
    x, h, xb = _postln(x, f, mods[0], G_F, ln_g[0, 1], ln_b[0, 1], tiles_per_batch, alpha=alpha,
                       mod_next=mods[1], sc_chunk=SC_M, sh_chunk=SH_M, emit_bf=True)

    w1, w2 = kv_w_kr[:, :half], kv_w_kr[:, half:]
    zpad = jnp.zeros((D, 128 - R), F32)
    wkv = jnp.concatenate([kv_w_dkv, kv_w_kr, zpad, -w2, w1, zpad], axis=1).astype(BF)
    wuk2 = kv_w_uk.reshape(C, H * nope)
    wuv2 = kv_w_uv.reshape(C, H * vdim)
    ckv, kpe, kfull, v_t = _kv_side(xb, wkv, kv_g.reshape(1, C), cos2, sin2, wuk2.astype(BF), wuv2.T.astype(BF),
                                    bm=bm, c_dim=C, r_dim=R, heads=H, nope=nope)

    wqr = b_w_qr[0]
    qpad = jnp.zeros(wqr.shape[:2] + (128 - R,), F32)
    wpe = jnp.concatenate([wqr, qpad], axis=-1).reshape(wqr.shape[0], H * 128)
    wrot = jnp.concatenate([-wqr[..., half:], wqr[..., :half], qpad], axis=-1).reshape(wqr.shape[0], H * 128)
    q_t = _q_side(h, b_w_dq[0].T.astype(BF), b_q_g[0].reshape(-1, 1), b_w_uq[0].reshape(-1, H * nope).T.astype(BF),
                  wpe.T.astype(BF), wrot.T.astype(BF), cos_qt, sin_qt, bm=bm, heads=H, nope=nope, scale=qscale)
    o_all = _flash_attention(kfull, q_t, v_t, batch=B, seq=S, heads=H, vdim=vdim, blk=min(1024, S), hb=min(4, H),
                             total_rows=T)

    q_lat_hc = _absorb_query(q_t, wuk2.astype(BF), heads=H, nope=nope, c_dim=C,
                             col_block=Tp // Bd, cols=Bd).reshape(H, C, Bd)
    q_lat = q_lat_hc.transpose(2, 0, 1)
    q_lat_t = jnp.pad(q_lat_hc.transpose(2, 1, 0), ((0, 0), (0, 0), (0, 128 - H)))
    q_pe_s = q_t[:, Tp:].reshape(H, 256, Bd)[:, nope:nope + R, :].transpose(2, 0, 1)
    o_lat = _decode_attention(page_table, q_lat, q_lat_t, q_pe_s, ckv[Tp:].reshape(Bd, 1, C),
                              kpe[Tp:].reshape(Bd, 1, R), cache_ckv, jnp.swapaxes(cache_kpe, 1, 2),
                              pages=min(16, n_pages))
    o_all = _value_up_into(o_all, o_lat.reshape(Bd, H * C), wuv2.astype(BF), heads=H, c_dim=C, vdim=vdim,
                           row_block=Tp // Bd, rows=Bd)
    mix = _wsmm(o_all, [b_w_o], bm=bm_big, bn=512, out_dtype=F32, name="attn_out")
    x = _postln(x, mix, mods[1], G_M, ln_g[1, 0], ln_b[1, 0], tiles_per_batch, alpha=alpha)[0]

    wr_pad = jnp.concatenate([moe_w_router[0], jnp.zeros((D, 128 - E), F32)], axis=1)
    br_pad = jnp.concatenate([moe_b_router[0], jnp.zeros((128 - E,), F32)]).reshape(1, 128)
    wts, idx, hf = _router(x, mods[1], SC_F, SH_F, wr_pad, br_pad, tiles_per_batch, E)
    bme = 384
    n_blocks = pl.cdiv(2 * T + E * (bme - 1), bme)
    row_token, dest, block_expert, next_expert, n_used = _route_tables(idx[:, :2], E, bme, n_blocks)
    grouped = dict(block_w=block_expert, next_w=next_expert, n_used=n_used)
    hs = _dispatch(hf, row_token)
    per_expert = lambda w: w.reshape((-1,) + w.shape[-2:])
    act = _wsmm(hs, [per_expert(moe_w_gate), per_expert(moe_w_up)], bm=bme, bn=1024, out_dtype=BF,
                name="moe_up", **grouped)
    ys = _wsmm(act, [per_expert(moe_w_down)], bm=bme, bn=512, out_dtype=F32, name="moe_down", **grouped)
    x = _combine_postln(x, ys, dest, wts, mods[1], G_F, ln_g[1, 1], ln_b[1, 1], tiles_per_batch, alpha=alpha)

    y_prompt = x[:Tp].reshape(B, S, D)
    y_sample = x[Tp:].reshape(Bd, Td, D)
    return (y_prompt, y_sample,
            ckv[:Tp].reshape(B, S, C), kpe[:Tp].reshape(B, S, R),
            ckv[Tp:].reshape(Bd, Td, C), kpe[Tp:].reshape(Bd, Td, R),
            v_sample.reshape(n_a, Bd, Td, A))
```

```python
import functools
import math

import jax
import jax.numpy as jnp
from jax import lax
from jax.experimental import pallas as pl
from jax.experimental.pallas import tpu as pltpu

BF = jnp.bfloat16
F32 = jnp.float32

ROW_TILE = 128
LN_EPS = 1e-5
RMS_EPS = 1e-6
ROPE_BASE = 10000.0
NEG_INF = float("-inf")
VMEM_LIMIT = 56 * 1024 * 1024
DECODE_SLOTS = 3


def _params(n_axes, vmem=VMEM_LIMIT):
    return pltpu.CompilerParams(dimension_semantics=("arbitrary",) * n_axes, vmem_limit_bytes=vmem)


def _gelu_tanh(x):
    return 0.5 * x * (1.0 + jnp.tanh(math.sqrt(2.0 / math.pi) * (x + 0.044715 * x * x * x)))


def _silu(x):
    return x * (1.0 / (1.0 + jnp.exp(-x)))


def _layernorm_rows(x, g, b):
    mu = jnp.mean(x, axis=-1, keepdims=True)
    xc = x - mu
    var = jnp.mean(xc * xc, axis=-1, keepdims=True)
    return xc * lax.rsqrt(var + LN_EPS) * g + b


def _group_starts(bw_ref, i):
    return (i == 0) | (bw_ref[i] != bw_ref[jnp.maximum(i - 1, 0)])


def _wsmm_kernel(bw_ref, nxt_ref, nu_ref, x_ref, *refs, n_w, act, has_bias, bn, n_off, row_chunks):
    w_hbm = refs[:n_w]
    k = n_w
    bias_ref = None
    if has_bias:
        bias_ref = refs[k]
        k += 1
    o_ref = refs[k]
    stage = refs[k + 1:k + 1 + n_w]
    wbf = refs[k + 1 + n_w:k + 1 + 2 * n_w]
    sem, cnt = refs[k + 1 + 2 * n_w], refs[k + 2 + 2 * n_w]
    j = pl.program_id(0)
    i = pl.program_id(1)

    def weight_copies(w_idx, jj, slot):
        col = pl.multiple_of(jj * bn + n_off, 128)
        return [pltpu.make_async_copy(w_hbm[t].at[w_idx, :, pl.ds(col, bn)], stage[t].at[slot], sem.at[t, slot])
                for t in range(n_w)]

    @pl.when((j == 0) & (i == 0))
    def _():
        cnt[0] = 0
        for cp in weight_copies(bw_ref[0], 0, 0):
            cp.start()

    @pl.when(_group_starts(bw_ref, i))
    def _():
        slot = cnt[0] % 2
        for cp in weight_copies(bw_ref[i], j, slot):
            cp.wait()
        more = nxt_ref[i] >= 0
        nw = jnp.where(more, nxt_ref[i], bw_ref[0])
        nj = jnp.where(more, j, j + 1)

        @pl.when(nj < pl.num_programs(0))
        def _():
            for cp in weight_copies(nw, nj, 1 - slot):
                cp.start()

        for t in range(n_w):
            wbf[t][...] = stage[t][slot].astype(BF)
        cnt[0] = cnt[0] + 1

    @pl.when(i < nu_ref[0])
    def _():
        rows = x_ref.shape[0] // row_chunks
        for c in range(row_chunks):
            rs = slice(c * rows, (c + 1) * rows)
            x = x_ref[rs, :]
            if act == "silu_in":
                x = _silu(x.astype(F32))
            x = x.astype(BF)
            acc = jnp.dot(x, wbf[0][...], preferred_element_type=F32)
            if n_w == 2:
                acc = _silu(acc) * jnp.dot(x, wbf[1][...], preferred_element_type=F32)
            if has_bias:
                acc = acc + bias_ref[...]
            if act == "gelu":
                acc = _gelu_tanh(acc)
            o_ref[rs, :] = acc.astype(o_ref.dtype)

    @pl.when(i >= nu_ref[0])
    def _():
        o_ref[...] = jnp.zeros(o_ref.shape, o_ref.dtype)


def _wsmm(x, ws, *, bm, bn, out_dtype, block_w=None, next_w=None, n_used=None, w_index=0, n_off=0,
          n_out=None, act=None, bias=None, name="wsmm"):
    M, K = x.shape
    N = ws[0].shape[-1] if n_out is None else n_out
    bn = min(bn, N)
    assert M % bm == 0 and N % bn == 0 and n_off % bn == 0 and all(w.ndim == 3 and w.shape[1] == K for w in ws)
    n_i = M // bm
    if block_w is None:
        block_w = jnp.full((n_i,), w_index, jnp.int32)
        next_w = jnp.full((n_i,), -1, jnp.int32)
        n_used = jnp.full((1,), n_i, jnp.int32)
    n_w = len(ws)
    offb = n_off // bn
    row_chunks = max(c for c in (1, 2, 4) if bm % (16 * c) == 0 and (c == 1 or bm // c >= 384))
    in_specs = [pl.BlockSpec((bm, K), lambda j, i, *_: (i, 0))] + [pl.BlockSpec(memory_space=pl.ANY)] * n_w
    args = [x, *ws]
    if bias is not None:
        in_specs.append(pl.BlockSpec((1, bn), lambda j, i, *_: (0, j + offb)))
        args.append(bias)
    grid_spec = pltpu.PrefetchScalarGridSpec(
        num_scalar_prefetch=3,
        grid=(N // bn, n_i),
        in_specs=in_specs,
        out_specs=pl.BlockSpec((bm, bn), lambda j, i, *_: (i, j)),
        scratch_shapes=[pltpu.VMEM((2, K, bn), ws[0].dtype)] * n_w + [pltpu.VMEM((K, bn), BF)] * n_w
        + [pltpu.SemaphoreType.DMA((n_w, 2)), pltpu.SMEM((1,), jnp.int32)],
    )
    return pl.pallas_call(
        functools.partial(_wsmm_kernel, n_w=n_w, act=act, has_bias=bias is not None, bn=bn, n_off=n_off,
                          row_chunks=row_chunks),
        out_shape=jax.ShapeDtypeStruct((M, N), out_dtype),
        grid_spec=grid_spec,
        compiler_params=_params(2),
        name=name,
    )(block_w, next_w, n_used, *args)


def _mod_spec(tiles_per_batch, D, chunk):
    return pl.BlockSpec((ROW_TILE, D), lambda i: (i // tiles_per_batch, chunk))


def _token_rows(x):
    if not isinstance(x, tuple):
        T, D = x.shape
        return [x], [pl.BlockSpec((ROW_TILE, D), lambda i, *_: (i, 0))], T, D, None
    xp, xs = x
    npt = xp.shape[0] // ROW_TILE
    assert xs.shape[0] == ROW_TILE
    D = xp.shape[1]
    specs = [pl.BlockSpec((ROW_TILE, D), lambda i, *_: (jnp.minimum(i, npt - 1), 0)),
             pl.BlockSpec((ROW_TILE, D), lambda i, *_: (0, 0))]
    return [xp, xs], specs, xp.shape[0] + xs.shape[0], D, npt


def _read_token_tile(x_refs, n_prompt_tiles):
    if len(x_refs) == 1:
        return x_refs[0][...]
    return jnp.where(pl.program_id(0) < n_prompt_tiles, x_refs[0][...], x_refs[1][...])


def _modulate_kernel(*refs, n_x, n_prompt_tiles):
    sc_ref, sh_ref, h_ref = refs[n_x:]
    x = _read_token_tile(refs[:n_x], n_prompt_tiles)
    h_ref[...] = (x * (1.0 + sc_ref[...]) + sh_ref[...]).astype(h_ref.dtype)


def _modulate(x, mod, sc_chunk, sh_chunk, tiles_per_batch):
    xs, x_specs, T, D, npt = _token_rows(x)
    return pl.pallas_call(
        functools.partial(_modulate_kernel, n_x=len(xs), n_prompt_tiles=npt),
        out_shape=jax.ShapeDtypeStruct((T, D), BF),
        grid=(T // ROW_TILE,),
        in_specs=x_specs + [_mod_spec(tiles_per_batch, D, sc_chunk), _mod_spec(tiles_per_batch, D, sh_chunk)],
        out_specs=pl.BlockSpec((ROW_TILE, D), lambda i: (i, 0)),
        compiler_params=_params(1),
        name="modulate",
    )(*xs, mod, mod)


def _postln_kernel(*refs, n_x, n_prompt_tiles, alpha, modulate_next, emit_bf):
    x = _read_token_tile(refs[:n_x], n_prompt_tiles)
    mix_ref, g_ref, lng_ref, lnb_ref = refs[n_x:n_x + 4]
    k = n_x + 4
    if modulate_next:
        sc_ref, sh_ref = refs[k], refs[k + 1]
        k += 2
    xo_ref = refs[k]; k += 1
    xn = _layernorm_rows(alpha * x + g_ref[...] * mix_ref[...], lng_ref[...], lnb_ref[...])
    xo_ref[...] = xn
    if modulate_next:
        refs[k][...] = (xn * (1.0 + sc_ref[...]) + sh_ref[...]).astype(BF)
        k += 1
    if emit_bf:
        refs[k][...] = xn.astype(BF)


def _postln(x, mix, mod_g, g_chunk, ln_g, ln_b, tiles_per_batch, *, alpha, mod_next=None,
            sc_chunk=None, sh_chunk=None, emit_bf=False):
    xs, x_specs, T, D, npt = _token_rows(x)
    row = pl.BlockSpec((ROW_TILE, D), lambda i: (i, 0))
    vec = pl.BlockSpec((1, D), lambda i: (0, 0))
    in_specs = x_specs + [row, _mod_spec(tiles_per_batch, D, g_chunk), vec, vec]
    args = xs + [mix, mod_g, ln_g.reshape(1, D), ln_b.reshape(1, D)]
    out_shape = [jax.ShapeDtypeStruct((T, D), F32)]
    out_specs = [row]
    if mod_next is not None:
        in_specs += [_mod_spec(tiles_per_batch, D, sc_chunk), _mod_spec(tiles_per_batch, D, sh_chunk)]
        args += [mod_next, mod_next]
        out_shape.append(jax.ShapeDtypeStruct((T, D), BF))
        out_specs.append(row)
    if emit_bf:
        out_shape.append(jax.ShapeDtypeStruct((T, D), BF))
        out_specs.append(row)
    return pl.pallas_call(
        functools.partial(_postln_kernel, n_x=len(xs), n_prompt_tiles=npt, alpha=alpha,
                          modulate_next=mod_next is not None, emit_bf=emit_bf),
        out_shape=out_shape,
        grid=(T // ROW_TILE,),
        in_specs=in_specs,
        out_specs=out_specs,
        compiler_params=_params(1),
        name="postln",
    )(*args)


def _spatial_gate_kernel(vpre_ref, u_ref, ws_ref, b_ref, vg_ref, vb_ref, mix_ref, vout_ref, *, groups):
    v = _layernorm_rows(vpre_ref[...], vg_ref[...], vb_ref[...])

    @pl.when(pl.program_id(0) == pl.num_programs(0) - 1)
    def _():
        vout_ref[...] = v

    vb16 = v.astype(BF)
    gd = v.shape[1] // groups
    for g in range(groups):
        cols = slice(g * gd, (g + 1) * gd)
        s = jnp.dot(ws_ref[g], vb16[:, cols], preferred_element_type=F32) + b_ref[:, cols]
        mix_ref[:, cols] = (u_ref[:, cols].astype(F32) * s).astype(BF)


def _spatial_gate(vpre, u, ws_tiles, b_tiles, v_g, v_b, n_prompt_tiles):
    T, A = vpre.shape
    G = ws_tiles.shape[1]
    row = pl.BlockSpec((ROW_TILE, A), lambda i: (i, 0))
    vec = pl.BlockSpec((1, A), lambda i: (0, 0))
    return pl.pallas_call(
        functools.partial(_spatial_gate_kernel, groups=G),
        out_shape=[jax.ShapeDtypeStruct((T, A), BF), jax.ShapeDtypeStruct((ROW_TILE, A), F32)],
        grid=(T // ROW_TILE,),
        in_specs=[row, row,
                  pl.BlockSpec((None, G, ROW_TILE, ROW_TILE), lambda i: (i // n_prompt_tiles, 0, 0, 0)),
                  pl.BlockSpec((None, ROW_TILE, A), lambda i: (i // n_prompt_tiles, 0, 0)),
                  vec, vec],
        out_specs=[row, pl.BlockSpec((ROW_TILE, A), lambda i: (0, 0))],
        compiler_params=_params(1),
        name="spatial_gate",
    )(vpre, u, ws_tiles, b_tiles, v_g.reshape(1, A), v_b.reshape(1, A))


_CONTRACT_LAST = (((1,), (1,)), ((), ()))


def _kv_kernel(x_ref, wkv_ref, kvg_ref, cos_ref, sin_ref, wuk_ref, wuvt_ref,
               ckv_ref, kpe_ref, kfull_ref, vt_ref, *, c_dim, r_dim, heads, nope):
    y = jnp.dot(x_ref[...], wkv_ref[...], preferred_element_type=F32)
    c = y[:, :c_dim]
    ckv = c * lax.rsqrt(jnp.mean(c * c, axis=-1, keepdims=True) + RMS_EPS) * kvg_ref[...]
    ckv_ref[...] = ckv
    kpe = y[:, c_dim:c_dim + r_dim] * cos_ref[...] + y[:, c_dim + 128:c_dim + 128 + r_dim] * sin_ref[...]
    kpe_ref[...] = kpe
    cb = ckv.astype(BF)
    kn = jnp.dot(cb, wuk_ref[...], preferred_element_type=F32)
    vt_ref[...] = lax.dot_general(wuvt_ref[...], cb, _CONTRACT_LAST, preferred_element_type=F32).astype(BF)
    kpe_pad = jnp.concatenate([kpe, jnp.zeros((kpe.shape[0], 128 - r_dim), F32)], axis=-1).astype(BF)
    for h in range(heads):
        kfull_ref[:, h * 256:h * 256 + nope] = kn[:, h * nope:(h + 1) * nope].astype(BF)
        kfull_ref[:, h * 256 + nope:(h + 1) * 256] = kpe_pad


def _kv_side(xb, wkv, kv_g, cos_k, sin_k, wuk, wuv_t, *, bm, c_dim, r_dim, heads, nope):
    T, D = xb.shape
    assert nope == 128 and r_dim <= 128
    row = lambda w: pl.BlockSpec((bm, w), lambda i: (i, 0))
    full = lambda a: pl.BlockSpec(a.shape, lambda i: (0,) * a.ndim)
    hv = wuv_t.shape[0]
    return pl.pallas_call(
        functools.partial(_kv_kernel, c_dim=c_dim, r_dim=r_dim, heads=heads, nope=nope),
        out_shape=[jax.ShapeDtypeStruct((T, c_dim), F32), jax.ShapeDtypeStruct((T, r_dim), F32),
                   jax.ShapeDtypeStruct((T, heads * 256), BF), jax.ShapeDtypeStruct((hv, T), BF)],
        grid=(T // bm,),
        in_specs=[row(D), full(wkv), full(kv_g), row(r_dim), row(r_dim), full(wuk), full(wuv_t)],
        out_specs=[row(c_dim), row(r_dim), row(heads * 256), pl.BlockSpec((hv, bm), lambda i: (0, i))],
        compiler_params=_params(1),
        name="kv_side",
    )(xb, wkv, kv_g, cos_k, sin_k, wuk, wuv_t)


def _q_kernel(h_ref, wdqt_ref, qg_ref, wuqt_ref, wpet_ref, wrott_ref, cost_ref, sint_ref, qt_ref,
              *, heads, nope, scale):
    ct = lax.dot_general(wdqt_ref[...], h_ref[...], _CONTRACT_LAST, preferred_element_type=F32)
    cqt = (ct * lax.rsqrt(jnp.mean(ct * ct, axis=0, keepdims=True) + RMS_EPS) * qg_ref[...]).astype(BF)
    qn = jnp.dot(wuqt_ref[...], cqt, preferred_element_type=F32) * scale
    qp = jnp.dot(wpet_ref[...], cqt, preferred_element_type=F32)
    qr = jnp.dot(wrott_ref[...], cqt, preferred_element_type=F32)
    cos = cost_ref[...]
    sin = sint_ref[...]
    for h in range(heads):
        qt_ref[h * 256:h * 256 + nope, :] = qn[h * nope:(h + 1) * nope].astype(BF)
        pe = qp[h * 128:(h + 1) * 128] * cos + qr[h * 128:(h + 1) * 128] * sin
        qt_ref[h * 256 + nope:(h + 1) * 256, :] = pe.astype(BF)


def _q_side(hb, wdq_t, q_g_col, wuq_t, wpe_t, wrot_t, cos_t, sin_t, *, bm, heads, nope, scale):
    T, D = hb.shape
    assert nope == 128
    full = lambda a: pl.BlockSpec(a.shape, lambda i: (0,) * a.ndim)
    col = lambda r: pl.BlockSpec((r, bm), lambda i: (0, i))
    return pl.pallas_call(
        functools.partial(_q_kernel, heads=heads, nope=nope, scale=scale),
        out_shape=jax.ShapeDtypeStruct((heads * 256, T), BF),
        grid=(T // bm,),
        in_specs=[pl.BlockSpec((bm, D), lambda i: (i, 0)), full(wdq_t), full(q_g_col), full(wuq_t),
                  full(wpe_t), full(wrot_t), col(128), col(128)],
        out_specs=col(heads * 256),
        compiler_params=_params(1),
        name="q_side",
    )(hb, wdq_t, q_g_col, wuq_t, wpe_t, wrot_t, cos_t, sin_t)


def _flash_kernel(qi_ref, ki_ref, k_ref, qt_ref, vt_ref, o_init_ref, o_ref, m_ref, l_ref, acc_ref, *, hb, vdim):
    del o_init_ref
    step = pl.program_id(2)
    qi = qi_ref[step]
    ki = ki_ref[step]

    @pl.when(ki == 0)
    def _():
        m_ref[...] = jnp.full(m_ref.shape, NEG_INF, F32)
        l_ref[...] = jnp.zeros(l_ref.shape, F32)
        acc_ref[...] = jnp.zeros(acc_ref.shape, F32)

    def update(masked):
        for h in range(hb):
            qr = slice(h * 256, (h + 1) * 256)
            vr = slice(h * vdim, (h + 1) * vdim)
            st = jnp.dot(k_ref[:, qr], qt_ref[qr, :], preferred_element_type=F32)
            if masked:
                key = lax.broadcasted_iota(jnp.int32, st.shape, 0)
                qry = lax.broadcasted_iota(jnp.int32, st.shape, 1)
                st = jnp.where(key <= qry, st, NEG_INF)
            m_prev = m_ref[h]
            m_new = jnp.maximum(m_prev, jnp.max(st, axis=0, keepdims=True))
            corr = jnp.exp2(m_prev - m_new)
            pt = jnp.exp2(st - m_new)
            l_ref[h] = corr * l_ref[h] + jnp.sum(pt, axis=0, keepdims=True)
            acc_ref[vr, :] = corr * acc_ref[vr, :] + jnp.dot(vt_ref[vr, :], pt.astype(BF),
                                                             preferred_element_type=F32)
            m_ref[h] = m_new

    @pl.when(ki < qi)
    def _():
        update(False)

    @pl.when(ki == qi)
    def _():
        update(True)
        for h in range(hb):
            vr = slice(h * vdim, (h + 1) * vdim)
            o_ref[:, vr] = (acc_ref[vr, :] / l_ref[h]).T.astype(o_ref.dtype)


def _flash_attention(kfull, q_t, v_t, *, batch, seq, heads, vdim, blk, hb, total_rows):
    assert heads % hb == 0
    nb = seq // blk
    pairs = [(a, b) for a in range(nb) for b in range(a + 1)]
    qi_map = jnp.asarray([p[0] for p in pairs], jnp.int32)
    ki_map = jnp.asarray([p[1] for p in pairs], jnp.int32)
    grid_spec = pltpu.PrefetchScalarGridSpec(
        num_scalar_prefetch=2,
        grid=(batch, heads // hb, len(pairs)),
        in_specs=[
            pl.BlockSpec((blk, hb * 256), lambda b, h, s, qi, ki: (b * nb + ki[s], h)),
            pl.BlockSpec((hb * 256, blk), lambda b, h, s, qi, ki: (h, b * nb + qi[s])),
            pl.BlockSpec((hb * vdim, blk), lambda b, h, s, qi, ki: (h, b * nb + ki[s])),
            pl.BlockSpec(memory_space=pl.ANY),
        ],
        out_specs=pl.BlockSpec((blk, hb * vdim), lambda b, h, s, qi, ki: (b * nb + qi[s], h)),
        scratch_shapes=[pltpu.VMEM((hb, 1, blk), F32), pltpu.VMEM((hb, 1, blk), F32),
                        pltpu.VMEM((hb * vdim, blk), F32)],
    )
    return pl.pallas_call(
        functools.partial(_flash_kernel, hb=hb, vdim=vdim),
        out_shape=jax.ShapeDtypeStruct((total_rows, heads * vdim), BF),
        grid_spec=grid_spec,
        input_output_aliases={5: 0},
        compiler_params=_params(3),
        name="flash_prompt",
    )(qi_map, ki_map, kfull, q_t, v_t, jnp.zeros((total_rows, heads * vdim), BF))


def _absorb_kernel(wuk_ref, qt_ref, o_ref):
    o_ref[...] = jnp.dot(wuk_ref[...], qt_ref[...], preferred_element_type=F32).astype(o_ref.dtype)


def _absorb_query(q_t, wuk, *, heads, nope, c_dim, col_block, cols):
    return pl.pallas_call(
        _absorb_kernel,
        out_shape=jax.ShapeDtypeStruct((heads * c_dim, cols), BF),
        grid=(heads,),
        in_specs=[pl.BlockSpec((c_dim, nope), lambda h: (0, h)),
                  pl.BlockSpec((nope, cols), lambda h: (2 * h, col_block))],
        out_specs=pl.BlockSpec((c_dim, cols), lambda h: (h, 0)),
        compiler_params=_params(1),
        name="absorb_query",
    )(wuk, q_t)


def _page_copies(pt_ref, ck_hbm, kp_hbm, ck_buf, kp_buf, sem, step, slot, i, pages):
    page = pt_ref[step * pages + i]
    return (pltpu.make_async_copy(ck_hbm.at[page], ck_buf.at[slot, i], sem.at[0, slot]),
            pltpu.make_async_copy(kp_hbm.at[page], kp_buf.at[slot, i], sem.at[1, slot]))


def _decode_kernel(pt_ref, qlat_ref, qlt_ref, qpe_ref, cnew_ref, knew_ref, ck_hbm, kp_hbm, o_ref,
                   ck_buf, kp_buf, sem, m_ref, l_ref, acc_ref, *, pages, steps_per_seq):
    n = pl.program_id(0)
    j = n % steps_per_seq
    slot = n % DECODE_SLOTS

    def start_step(step):
        for i in range(pages):
            for cp in _page_copies(pt_ref, ck_hbm, kp_hbm, ck_buf, kp_buf, sem, step, step % DECODE_SLOTS, i, pages):
                cp.start()

    @pl.when(n == 0)
    def _():
        for step in range(DECODE_SLOTS - 1):
            @pl.when(step < pl.num_programs(0))
            def _():
                start_step(step)

    @pl.when(n + DECODE_SLOTS - 1 < pl.num_programs(0))
    def _():
        start_step(n + DECODE_SLOTS - 1)

    @pl.when(j == 0)
    def _():
        m_ref[...] = jnp.full(m_ref.shape, NEG_INF, F32)
        l_ref[...] = jnp.zeros(l_ref.shape, F32)
        acc_ref[...] = jnp.zeros(acc_ref.shape, F32)

    for i in range(pages):
        for cp in _page_copies(pt_ref, ck_hbm, kp_hbm, ck_buf, kp_buf, sem, n, slot, i, pages):
            cp.wait()

    ql = qlat_ref[0]
    qlt = qlt_ref[0]
    qp = qpe_ref[0]
    heads = ql.shape[0]
    cks = []
    scores = []
    for i in range(pages):
        ck = ck_buf[slot, i].astype(BF)
        kp = kp_buf[slot, i].astype(BF)
        cks.append(ck)
        st = jnp.dot(ck, qlt, preferred_element_type=F32)
        st = jnp.where(lax.broadcasted_iota(jnp.int32, st.shape, 1) < heads, st, 0.0)
        scores.append(st.T[:heads] + jnp.dot(qp, kp, preferred_element_type=F32))
    s = jnp.concatenate(scores, axis=-1)
    m_prev = m_ref[...]
    m_new = jnp.maximum(m_prev, jnp.max(s, axis=-1, keepdims=True))
    corr = jnp.exp2(m_prev - m_new)
    p = jnp.exp2(s - m_new)
    l_new = corr * l_ref[...] + jnp.sum(p, axis=-1, keepdims=True)
    pb = p.astype(BF)
    rows = cks[0].shape[0]
    pv = jnp.dot(pb[:, :rows], cks[0], preferred_element_type=F32)
    for i in range(1, pages):
        pv = pv + jnp.dot(pb[:, i * rows:(i + 1) * rows], cks[i], preferred_element_type=F32)
    acc_new = corr * acc_ref[...] + pv
    m_ref[...] = m_new
    l_ref[...] = l_new
    acc_ref[...] = acc_new

    @pl.when(j == steps_per_seq - 1)
    def _():
        cn = cnew_ref[0].astype(BF).astype(F32)
        kn = knew_ref[0].astype(BF).astype(F32)
        s_self = (jnp.sum(ql.astype(F32) * cn, axis=-1, keepdims=True)
                  + jnp.sum(qp.astype(F32) * kn, axis=-1, keepdims=True))
        m_fin = jnp.maximum(m_new, s_self)
        c2 = jnp.exp2(m_new - m_fin)
        p_self = jnp.exp2(s_self - m_fin)
        l_fin = c2 * l_new + p_self
        acc_fin = c2 * acc_new + p_self.astype(BF).astype(F32) * cn
        o_ref[0] = (acc_fin / l_fin).astype(o_ref.dtype)


def _decode_attention(page_table, q_lat, q_lat_t, q_pe, ckv_new, kpe_new, cache_ckv, cache_kpe_t, *, pages):
    Bd, H, C = q_lat.shape
    R = q_pe.shape[2]
    n_pages = page_table.shape[1]
    page_rows = cache_ckv.shape[1]
    assert n_pages % pages == 0
    sps = n_pages // pages
    per_b = lambda *shape: pl.BlockSpec((1,) + shape, lambda n, pt: (n // sps, 0, 0))
    hbm = pl.BlockSpec(memory_space=pl.ANY)
    grid_spec = pltpu.PrefetchScalarGridSpec(
        num_scalar_prefetch=1,
        grid=(Bd * sps,),
        in_specs=[per_b(H, C), per_b(C, 128), per_b(H, R), per_b(1, C), per_b(1, R), hbm, hbm],
        out_specs=per_b(H, C),
        scratch_shapes=[pltpu.VMEM((DECODE_SLOTS, pages, page_rows, C), cache_ckv.dtype),
                        pltpu.VMEM((DECODE_SLOTS, pages, R, page_rows), cache_kpe_t.dtype),
                        pltpu.SemaphoreType.DMA((2, DECODE_SLOTS)),
                        pltpu.VMEM((H, 1), F32), pltpu.VMEM((H, 1), F32), pltpu.VMEM((H, C), F32)],
    )
    return pl.pallas_call(
        functools.partial(_decode_kernel, pages=pages, steps_per_seq=sps),
        out_shape=jax.ShapeDtypeStruct((Bd, H, C), BF),
        grid_spec=grid_spec,
        compiler_params=_params(1),
        name="decode_attention",
    )(page_table.reshape(-1), q_lat, q_lat_t, q_pe, ckv_new, kpe_new, cache_ckv, cache_kpe_t)


def _value_up_kernel(o_hbm_ref, olat_ref, wuv_ref, o_ref):
    del o_hbm_ref
    o_ref[...] = jnp.dot(olat_ref[...], wuv_ref[...], preferred_element_type=F32).astype(o_ref.dtype)


def _value_up_into(o_all, o_lat, wuv, *, heads, c_dim, vdim, row_block, rows):
    return pl.pallas_call(
        _value_up_kernel,
        out_shape=jax.ShapeDtypeStruct(o_all.shape, o_all.dtype),
        grid=(heads,),
        in_specs=[pl.BlockSpec(memory_space=pl.ANY),
                  pl.BlockSpec((rows, c_dim), lambda h: (0, h)),
                  pl.BlockSpec((c_dim, vdim), lambda h: (0, h))],
        out_specs=pl.BlockSpec((rows, vdim), lambda h: (row_block, h)),
        input_output_aliases={0: 0},
        compiler_params=_params(1),
        name="value_up",
    )(o_all, o_lat, wuv)


def _router_kernel(x_ref, sc_ref, sh_ref, wrh_ref, wrl_ref, br_ref, w_ref, i_ref, h_ref, *, n_exp):
    h = x_ref[...] * (1.0 + sc_ref[...]) + sh_ref[...]
    h_hi = h.astype(BF)
    h_lo = (h - h_hi.astype(F32)).astype(BF)
    logits = (jnp.dot(h_hi, wrh_ref[...], preferred_element_type=F32)
              + jnp.dot(h_hi, wrl_ref[...], preferred_element_type=F32)
              + jnp.dot(h_lo, wrh_ref[...], preferred_element_type=F32)) + br_ref[...]
    lane = lax.broadcasted_iota(jnp.int32, logits.shape, 1)
    valid = lane < n_exp
    logits = jnp.where(valid, logits, NEG_INF)
    mx = jnp.max(logits, axis=-1, keepdims=True)
    e = jnp.exp(logits - mx)
    probs = e / jnp.sum(e, axis=-1, keepdims=True)
    big = jnp.int32(1 << 30)
    p1 = jnp.max(probs, axis=-1, keepdims=True)
    i1 = jnp.min(jnp.where((probs == p1) & valid, lane, big), axis=-1, keepdims=True)
    rest = jnp.where((lane == i1) | ~valid, -1.0, probs)
    p2 = jnp.max(rest, axis=-1, keepdims=True)
    i2 = jnp.min(jnp.where(rest == p2, lane, big), axis=-1, keepdims=True)
    denom = p1 + p2
    w_ref[...] = jnp.where(lane == 0, p1 / denom, jnp.where(lane == 1, p2 / denom, 0.0))
    i_ref[...] = jnp.where(lane == 0, i1, jnp.where(lane == 1, i2, 0))
    h_ref[...] = h


def _router(x, mod, sc_chunk, sh_chunk, wr_pad, br_pad, tiles_per_batch, n_exp):
    T, D = x.shape
    row = pl.BlockSpec((ROW_TILE, D), lambda i: (i, 0))
    sel = pl.BlockSpec((ROW_TILE, 128), lambda i: (i, 0))
    wr_hi = wr_pad.astype(BF)
    wr_lo = (wr_pad - wr_hi.astype(F32)).astype(BF)
    wspec = pl.BlockSpec((D, 128), lambda i: (0, 0))
    return pl.pallas_call(
        functools.partial(_router_kernel, n_exp=n_exp),
        out_shape=[jax.ShapeDtypeStruct((T, 128), F32), jax.ShapeDtypeStruct((T, 128), jnp.int32),
                   jax.ShapeDtypeStruct((T, D), F32)],
        grid=(T // ROW_TILE,),
        in_specs=[row, _mod_spec(tiles_per_batch, D, sc_chunk), _mod_spec(tiles_per_batch, D, sh_chunk),
                  wspec, wspec, pl.BlockSpec((1, 128), lambda i: (0, 0))],
        out_specs=[sel, sel, row],
        compiler_params=_params(1),
        name="router",
    )(x, mod, mod, wr_hi, wr_lo, br_pad)


def _route_tables(idx2, n_exp, bm, n_blocks):
    T = idx2.shape[0]
    e_flat = idx2.reshape(-1)
    onehot = (e_flat[:, None] == jnp.arange(n_exp, dtype=jnp.int32)[None, :]).astype(jnp.int32)
    rank = jnp.sum((jnp.cumsum(onehot, axis=0) - onehot) * onehot, axis=1)
    counts = jnp.sum(onehot, axis=0)
    blocks_per = (counts + bm - 1) // bm
    blocks_end = jnp.cumsum(blocks_per)
    dest = (blocks_end - blocks_per)[e_flat] * bm + rank
    n_used = blocks_end[-1]
    b = jnp.minimum(jnp.arange(n_blocks, dtype=jnp.int32), n_used - 1)
    block_expert = jnp.sum((b[:, None] >= blocks_end[None, :]).astype(jnp.int32), axis=1)
    row_token = jnp.zeros((n_blocks * bm,), jnp.int32).at[dest].set(jnp.arange(2 * T, dtype=jnp.int32) // 2)
    ids = jnp.arange(n_exp, dtype=jnp.int32)
    later_used = (ids[None, :] > ids[:, None]) & (blocks_per[None, :] > 0)
    next_of_expert = jnp.min(jnp.where(later_used, ids[None, :], n_exp), axis=1)
    next_expert = jnp.where(next_of_expert < n_exp, next_of_expert, -1)[block_expert]
    return (row_token, dest.astype(jnp.int32), block_expert.astype(jnp.int32), next_expert.astype(jnp.int32),
            n_used.reshape(1).astype(jnp.int32))


def _row_copy(src_hbm, row, dst, sem):
    return pltpu.make_async_copy(src_hbm.at[pl.ds(row, 1)], dst, sem)


ROW_DMA_UNROLL = 8


def _dispatch_kernel(tok_ref, h_hbm, o_ref, buf, sem, *, rows):
    i = pl.program_id(0)
    slot = i % 2

    def issue(step, s):
        def body(r8, c):
            for k in range(ROW_DMA_UNROLL):
                r = r8 * ROW_DMA_UNROLL + k
                _row_copy(h_hbm, tok_ref[step * rows + r], buf.at[s, pl.ds(r, 1)], sem.at[s]).start(priority=k % 2)
            return c
        lax.fori_loop(0, rows // ROW_DMA_UNROLL, body, 0)

    @pl.when(i == 0)
    def _():
        issue(0, 0)

    @pl.when(i + 1 < pl.num_programs(0))
    def _():
        issue(i + 1, 1 - slot)

    def wait_body(r, c):
        _row_copy(h_hbm, 0, buf.at[slot, pl.ds(r, 1)], sem.at[slot]).wait()
        return c
    lax.fori_loop(0, rows, wait_body, 0, unroll=8)
    o_ref[...] = buf[slot].astype(o_ref.dtype)


def _dispatch(h, row_token, *, rows=256):
    T, D = h.shape
    Rn = row_token.shape[0]
    if Rn % rows:
        rows = ROW_TILE
    assert Rn % rows == 0 and rows % ROW_DMA_UNROLL == 0
    grid_spec = pltpu.PrefetchScalarGridSpec(
        num_scalar_prefetch=1,
        grid=(Rn // rows,),
        in_specs=[pl.BlockSpec(memory_space=pl.ANY)],
        out_specs=pl.BlockSpec((rows, D), lambda i, tok: (i, 0)),
        scratch_shapes=[pltpu.VMEM((2, rows, D), h.dtype), pltpu.SemaphoreType.DMA((2,))],
    )
    return pl.pallas_call(
        functools.partial(_dispatch_kernel, rows=rows),
        out_shape=jax.ShapeDtypeStruct((Rn, D), h.dtype),
        grid_spec=grid_spec,
        compiler_params=_params(1),
        name="moe_dispatch",
    )(row_token, h)


def _combine_postln_kernel(dest_ref, x_ref, g_ref, lng_ref, lnb_ref, w_ref, y_hbm, yp_ref, ys_ref, buf, sem,
                           *, alpha, rows, n_prompt_tiles):
    i = pl.program_id(0)
    slot = i % 2

    def issue(step, s):
        def body(r, c):
            a = (step * rows + r) * 2
            _row_copy(y_hbm, dest_ref[a], buf.at[s, 0, pl.ds(r, 1)], sem.at[s]).start(priority=0)
            _row_copy(y_hbm, dest_ref[a + 1], buf.at[s, 1, pl.ds(r, 1)], sem.at[s]).start(priority=1)
            return c
        lax.fori_loop(0, rows, body, 0, unroll=8)

    @pl.when(i == 0)
    def _():
        issue(0, 0)

    @pl.when(i + 1 < pl.num_programs(0))
    def _():
        issue(i + 1, 1 - slot)

    def wait_body(r, c):
        _row_copy(y_hbm, 0, buf.at[slot, 0, pl.ds(r, 1)], sem.at[slot]).wait()
        _row_copy(y_hbm, 0, buf.at[slot, 1, pl.ds(r, 1)], sem.at[slot]).wait()
        return c
    lax.fori_loop(0, rows, wait_body, 0, unroll=8)
    f = w_ref[:, 0:1] * buf[slot, 0] + w_ref[:, 1:2] * buf[slot, 1]
    xn = _layernorm_rows(alpha * x_ref[...] + g_ref[...] * f, lng_ref[...], lnb_ref[...])

    @pl.when(i < n_prompt_tiles)
    def _():
        yp_ref[...] = xn

    @pl.when(i >= n_prompt_tiles)
    def _():
        ys_ref[...] = xn


def _combine_postln(x, y, dest, wts, mod_g, g_chunk, ln_g, ln_b, tiles_per_batch, n_prompt_tiles, *, alpha):
    T, D = x.shape
    rows = ROW_TILE
    assert T == (n_prompt_tiles + 1) * rows
    row = pl.BlockSpec((rows, D), lambda i, d: (i, 0))
    vec = pl.BlockSpec((1, D), lambda i, d: (0, 0))
    grid_spec = pltpu.PrefetchScalarGridSpec(
        num_scalar_prefetch=1,
        grid=(T // rows,),
        in_specs=[row, pl.BlockSpec((rows, D), lambda i, d: (i // tiles_per_batch, g_chunk)), vec, vec,
                  pl.BlockSpec((rows, 128), lambda i, d: (i, 0)), pl.BlockSpec(memory_space=pl.ANY)],
        out_specs=[pl.BlockSpec((rows, D), lambda i, d: (jnp.minimum(i, n_prompt_tiles - 1), 0)),
                   pl.BlockSpec((rows, D), lambda i, d: (0, 0))],
        scratch_shapes=[pltpu.VMEM((2, 2, rows, D), F32), pltpu.SemaphoreType.DMA((2,))],
    )
    return pl.pallas_call(
        functools.partial(_combine_postln_kernel, alpha=alpha, rows=rows, n_prompt_tiles=n_prompt_tiles),
        out_shape=[jax.ShapeDtypeStruct((n_prompt_tiles * rows, D), F32), jax.ShapeDtypeStruct((rows, D), F32)],
        grid_spec=grid_spec,
        compiler_params=_params(1),
        name="moe_combine_postln",
    )(dest, x, mod_g, ln_g.reshape(1, D), ln_b.reshape(1, D), wts, y)


def _pick_bm(T, target):
    best = ROW_TILE
    for m in range(ROW_TILE, target + 1, ROW_TILE):
        if T % m == 0:
            best = m
    return best


def kernel(x_prompt, x_sample, cache_ckv, cache_kpe, page_table, c_prompt, c_sample, w_ada, b_ada, ln_g, ln_b, a_w_in, a_v_g, a_v_b, a_w_s, a_b_s, a_w_out, b_w_dq, b_q_g, b_w_uq, b_w_qr, b_w_o, kv_w_dkv, kv_g, kv_w_kr, kv_w_uk, kv_w_uv, ffn_w_gate, ffn_w_up, ffn_w_down, moe_w_router, moe_b_router, moe_w_gate, moe_w_up, moe_w_down):
    B, S, D = x_prompt.shape
    Bd, Td, _ = x_sample.shape
    depth = w_ada.shape[0]
    n_a = a_w_in.shape[0]
    assert depth == 2 and n_a == 1 and Td == 1 and Bd == ROW_TILE and S % ROW_TILE == 0
    A = a_w_in.shape[2] // 2
    G = a_w_s.shape[1]
    H, nope = b_w_uq.shape[2], b_w_uq.shape[3]
    R = b_w_qr.shape[3]
    C = kv_w_dkv.shape[1]
    vdim = kv_w_uv.shape[2]
    E = moe_w_router.shape[2]
    n_pages, page_rows = page_table.shape[1], cache_ckv.shape[1]
    past_len = n_pages * page_rows
    alpha = (2.0 * depth) ** 0.25
    scale = (nope + R) ** -0.5
    Tp = B * S
    T = Tp + Bd
    tiles_per_batch = S // ROW_TILE
    n_prompt_tiles = Tp // ROW_TILE
    bm = _pick_bm(T, 640)
    bm_mid = _pick_bm(T, 832)
    bm_big = _pick_bm(T, 1664)

    x = (x_prompt.reshape(Tp, D), x_sample.reshape(Bd, D))

    c_tiles = jnp.concatenate([jnp.repeat(c_prompt, ROW_TILE, axis=0), c_sample], axis=0)
    mods = [_wsmm(c_tiles, [w_ada], bm=c_tiles.shape[0], bn=512, out_dtype=F32, act="silu_in",
                  bias=b_ada[l].reshape(1, -1), w_index=l, name="adaln") for l in range(depth)]
    SH_M, SC_M, G_M, SH_F, SC_F, G_F = range(6)

    half = R // 2
    inv = ROPE_BASE ** (-jnp.arange(half, dtype=F32) / half)
    pos = jnp.concatenate([jnp.tile(jnp.arange(S, dtype=jnp.int32), B),
                           jnp.full((Bd,), past_len, jnp.int32)]).astype(F32)
    ang = pos[:, None] * inv[None, :]
    cos2 = jnp.concatenate([jnp.cos(ang), jnp.cos(ang)], axis=-1)
    sin2 = jnp.concatenate([jnp.sin(ang), jnp.sin(ang)], axis=-1)
    qscale = scale * math.log2(math.e)
    row_pad = jnp.zeros((128 - R, T), F32)
    cos_qt = jnp.concatenate([cos2.T * qscale, row_pad], axis=0)
    sin_qt = jnp.concatenate([sin2.T * qscale, row_pad], axis=0)

    h = _modulate(x, mods[0], SC_M, SH_M, tiles_per_batch)
    u = _wsmm(h, [a_w_in], bm=bm_big, bn=512, out_dtype=BF, n_off=0, n_out=A, act="gelu", name="a_in_u")
    vpre = _wsmm(h, [a_w_in], bm=bm_big, bn=512, out_dtype=F32, n_off=A, n_out=A, act="gelu", name="a_in_v")
    tril = jnp.tril(jnp.ones((ROW_TILE, ROW_TILE), bool))
    ws_prompt = jnp.where(tril, a_w_s[0], 0.0)
    ws_sample = a_w_s[0][:, :1, :1] * jnp.eye(ROW_TILE, dtype=F32)
    ws_tiles = jnp.stack([ws_prompt, ws_sample]).astype(BF)
    gd = A // G
    b_prompt = jnp.repeat(a_b_s[0].T, gd, axis=1)
    b_sample = jnp.broadcast_to(b_prompt[:1], b_prompt.shape)
    b_tiles = jnp.stack([b_prompt, b_sample])
    gated, v_sample = _spatial_gate(vpre, u, ws_tiles, b_tiles, a_v_g[0], a_v_b[0], n_prompt_tiles)
    mix = _wsmm(gated, [a_w_out], bm=bm_mid, bn=512, out_dtype=F32, name="a_out")
    x, h = _postln(x, mix, mods[0], G_M, ln_g[0, 0], ln_b[0, 0], tiles_per_batch, alpha=alpha,
                   mod_next=mods[0], sc_chunk=SC_F, sh_chunk=SH_F)
    act = _wsmm(h, [ffn_w_gate, ffn_w_up], bm=bm_big, bn=512, out_dtype=BF, name="ffn_up")
    f = _wsmm(act, [ffn_w_down], bm=bm, bn=512, out_dtype=F32, name="ffn_down")
    x, h, xb = _postln(x, f, mods[0], G_F, ln_g[0, 1], ln_b[0, 1], tiles_per_batch, alpha=alpha,
                       mod_next=mods[1], sc_chunk=SC_M, sh_chunk=SH_M, emit_bf=True)

    w1, w2 = kv_w_kr[:, :half], kv_w_kr[:, half:]
    zpad = jnp.zeros((D, 128 - R), F32)
    wkv = jnp.concatenate([kv_w_dkv, kv_w_kr, zpad, -w2, w1, zpad], axis=1).astype(BF)
    wuk2 = kv_w_uk.reshape(C, H * nope)
    wuv2 = kv_w_uv.reshape(C, H * vdim)
    ckv, kpe, kfull, v_t = _kv_side(xb, wkv, kv_g.reshape(1, C), cos2, sin2, wuk2.astype(BF), wuv2.T.astype(BF),
                                    bm=bm, c_dim=C, r_dim=R, heads=H, nope=nope)

    wqr = b_w_qr[0]
    qpad = jnp.zeros(wqr.shape[:2] + (128 - R,), F32)
    wpe = jnp.concatenate([wqr, qpad], axis=-1).reshape(wqr.shape[0], H * 128)
    wrot = jnp.concatenate([-wqr[..., half:], wqr[..., :half], qpad], axis=-1).reshape(wqr.shape[0], H * 128)
    q_t = _q_side(h, b_w_dq[0].T.astype(BF), b_q_g[0].reshape(-1, 1), b_w_uq[0].reshape(-1, H * nope).T.astype(BF),
                  wpe.T.astype(BF), wrot.T.astype(BF), cos_qt, sin_qt, bm=bm, heads=H, nope=nope, scale=qscale)
    o_all = _flash_attention(kfull, q_t, v_t, batch=B, seq=S, heads=H, vdim=vdim, blk=min(1024, S), hb=min(4, H),
                             total_rows=T)

    q_lat_hc = _absorb_query(q_t, wuk2.astype(BF), heads=H, nope=nope, c_dim=C,
                             col_block=Tp // Bd, cols=Bd).reshape(H, C, Bd)
    q_lat = q_lat_hc.transpose(2, 0, 1)
    q_lat_t = jnp.pad(q_lat_hc.transpose(2, 1, 0), ((0, 0), (0, 0), (0, 128 - H)))
    q_pe_s = q_t[:, Tp:].reshape(H, 256, Bd)[:, nope:nope + R, :].transpose(2, 0, 1)
    o_lat = _decode_attention(page_table, q_lat, q_lat_t, q_pe_s, ckv[Tp:].reshape(Bd, 1, C),
                              kpe[Tp:].reshape(Bd, 1, R), cache_ckv, jnp.swapaxes(cache_kpe, 1, 2),
                              pages=min(32, n_pages))
    o_all = _value_up_into(o_all, o_lat.reshape(Bd, H * C), wuv2.astype(BF), heads=H, c_dim=C, vdim=vdim,
                           row_block=Tp // Bd, rows=Bd)
    mix = _wsmm(o_all, [b_w_o], bm=bm_big, bn=512, out_dtype=F32, name="attn_out")
    x = _postln(x, mix, mods[1], G_M, ln_g[1, 0], ln_b[1, 0], tiles_per_batch, alpha=alpha)[0]

    wr_pad = jnp.concatenate([moe_w_router[0], jnp.zeros((D, 128 - E), F32)], axis=1)
    br_pad = jnp.concatenate([moe_b_router[0], jnp.zeros((128 - E,), F32)]).reshape(1, 128)
    wts, idx, hf = _router(x, mods[1], SC_F, SH_F, wr_pad, br_pad, tiles_per_batch, E)
    bme = 384
    n_blocks = pl.cdiv(2 * T + E * (bme - 1), bme)
    row_token, dest, block_expert, next_expert, n_used = _route_tables(idx[:, :2], E, bme, n_blocks)
    grouped = dict(block_w=block_expert, next_w=next_expert, n_used=n_used)
    hs = _dispatch(hf, row_token)
    per_expert = lambda w: w.reshape((-1,) + w.shape[-2:])
    act = _wsmm(hs, [per_expert(moe_w_gate), per_expert(moe_w_up)], bm=bme, bn=1024, out_dtype=BF,
                name="moe_up", **grouped)
    ys = _wsmm(act, [per_expert(moe_w_down)], bm=bme, bn=512, out_dtype=F32, name="moe_down", **grouped)
    y_prompt, y_sample = _combine_postln(x, ys, dest, wts, mods[1], G_F, ln_g[1, 1], ln_b[1, 1], tiles_per_batch,
                                         n_prompt_tiles, alpha=alpha)
    return (y_prompt.reshape(B, S, D), y_sample.reshape(Bd, Td, D),
            ckv[:Tp].reshape(B, S, C), kpe[:Tp].reshape(B, S, R),
            ckv[Tp:].reshape(Bd, Td, C), kpe[Tp:].reshape(Bd, Td, R),
            v_sample.reshape(n_a, Bd, Td, A))
```

```python
import functools
import math

import jax
import jax.numpy as jnp
from jax import lax
from jax.experimental import pallas as pl
from jax.experimental.pallas import tpu as pltpu

BF = jnp.bfloat16
F32 = jnp.float32

ROW_TILE = 128
LN_EPS = 1e-5
RMS_EPS = 1e-6
ROPE_BASE = 10000.0
NEG_INF = float("-inf")
VMEM_LIMIT = 56 * 1024 * 1024
DECODE_SLOTS = 3

def _params(n_axes, vmem=VMEM_LIMIT):
    return pltpu.CompilerParams(dimension_semantics=("arbitrary",) * n_axes, vmem_limit_bytes=vmem)


def _gelu_tanh(x):
    return 0.5 * x * (1.0 + jnp.tanh(math.sqrt(2.0 / math.pi) * (x + 0.044715 * x * x * x)))


def _silu(x):
    return x * (1.0 / (1.0 + jnp.exp(-x)))


def _layernorm_rows(x, g, b):
    mu = jnp.mean(x, axis=-1, keepdims=True)
    xc = x - mu
    var = jnp.mean(xc * xc, axis=-1, keepdims=True)
    return xc * lax.rsqrt(var + LN_EPS) * g + b


def _group_starts(bw_ref, i):
    return (i == 0) | (bw_ref[i] != bw_ref[jnp.maximum(i - 1, 0)])


def _wsmm_kernel(bw_ref, nxt_ref, nu_ref, x_ref, *refs, n_w, act, has_bias, bn, n_off, row_chunks):
    w_hbm = refs[:n_w]
    k = n_w
    bias_ref = None
    if has_bias:
        bias_ref = refs[k]
        k += 1
    o_ref = refs[k]
    stage = refs[k + 1:k + 1 + n_w]
    wbf = refs[k + 1 + n_w:k + 1 + 2 * n_w]
    sem, cnt = refs[k + 1 + 2 * n_w], refs[k + 2 + 2 * n_w]
    j = pl.program_id(0)
    i = pl.program_id(1)

    def weight_copies(w_idx, jj, slot):
        col = pl.multiple_of(jj * bn + n_off, 128)
        return [pltpu.make_async_copy(w_hbm[t].at[w_idx, :, pl.ds(col, bn)], stage[t].at[slot], sem.at[t, slot])
                for t in range(n_w)]

    @pl.when((j == 0) & (i == 0))
    def _():
        cnt[0] = 0
        for cp in weight_copies(bw_ref[0], 0, 0):
            cp.start()

    @pl.when(_group_starts(bw_ref, i))
    def _():
        slot = cnt[0] % 2
        for cp in weight_copies(bw_ref[i], j, slot):
            cp.wait()
        more = nxt_ref[i] >= 0
        nw = jnp.where(more, nxt_ref[i], bw_ref[0])
        nj = jnp.where(more, j, j + 1)

        @pl.when(nj < pl.num_programs(0))
        def _():
            for cp in weight_copies(nw, nj, 1 - slot):
                cp.start()

        for t in range(n_w):
            wbf[t][...] = stage[t][slot].astype(BF)
        cnt[0] = cnt[0] + 1

    @pl.when(i < nu_ref[0])
    def _():
        rows = x_ref.shape[0] // row_chunks
        for c in range(row_chunks):
            rs = slice(c * rows, (c + 1) * rows)
            x = x_ref[rs, :]
            if act == "silu_in":
                x = _silu(x.astype(F32))
            x = x.astype(BF)
            acc = jnp.dot(x, wbf[0][...], preferred_element_type=F32)
            if n_w == 2:
                acc = _silu(acc) * jnp.dot(x, wbf[1][...], preferred_element_type=F32)
            if has_bias:
                acc = acc + bias_ref[...]
            if act == "gelu":
                acc = _gelu_tanh(acc)
            o_ref[rs, :] = acc.astype(o_ref.dtype)

    @pl.when(i >= nu_ref[0])
    def _():
        o_ref[...] = jnp.zeros(o_ref.shape, o_ref.dtype)


def _wsmm(x, ws, *, bm, bn, out_dtype, block_w=None, next_w=None, n_used=None, w_index=0, n_off=0,
          n_out=None, act=None, bias=None, name="wsmm"):
    M, K = x.shape
    N = ws[0].shape[-1] if n_out is None else n_out
    bn = min(bn, N)
    assert M % bm == 0 and N % bn == 0 and n_off % bn == 0 and all(w.ndim == 3 and w.shape[1] == K for w in ws)
    n_i = M // bm
    if block_w is None:
        block_w = jnp.full((n_i,), w_index, jnp.int32)
        next_w = jnp.full((n_i,), -1, jnp.int32)
        n_used = jnp.full((1,), n_i, jnp.int32)
    n_w = len(ws)
    offb = n_off // bn
    row_chunks = max(c for c in (1, 2, 4) if bm % (16 * c) == 0 and (c == 1 or bm // c >= 384))
    in_specs = [pl.BlockSpec((bm, K), lambda j, i, *_: (i, 0))] + [pl.BlockSpec(memory_space=pl.ANY)] * n_w
    args = [x, *ws]
    if bias is not None:
        in_specs.append(pl.BlockSpec((1, bn), lambda j, i, *_: (0, j + offb)))
        args.append(bias)
    grid_spec = pltpu.PrefetchScalarGridSpec(
        num_scalar_prefetch=3,
        grid=(N // bn, n_i),
        in_specs=in_specs,
        out_specs=pl.BlockSpec((bm, bn), lambda j, i, *_: (i, j)),
        scratch_shapes=[pltpu.VMEM((2, K, bn), ws[0].dtype)] * n_w + [pltpu.VMEM((K, bn), BF)] * n_w
        + [pltpu.SemaphoreType.DMA((n_w, 2)), pltpu.SMEM((1,), jnp.int32)],
    )
    return pl.pallas_call(
        functools.partial(_wsmm_kernel, n_w=n_w, act=act, has_bias=bias is not None, bn=bn, n_off=n_off,
                          row_chunks=row_chunks),
        out_shape=jax.ShapeDtypeStruct((M, N), out_dtype),
        grid_spec=grid_spec,
        compiler_params=_params(2),
        name=name,
    )(block_w, next_w, n_used, *args)


def _mod_spec(tiles_per_batch, D, chunk):
    return pl.BlockSpec((ROW_TILE, D), lambda i: (i // tiles_per_batch, chunk))


def _token_rows(x):
    if not isinstance(x, tuple):
        T, D = x.shape
        return [x], [pl.BlockSpec((ROW_TILE, D), lambda i, *_: (i, 0))], T, D, None
    xp, xs = x
    npt = xp.shape[0] // ROW_TILE
    assert xs.shape[0] == ROW_TILE
    D = xp.shape[1]
    specs = [pl.BlockSpec((ROW_TILE, D), lambda i, *_: (jnp.minimum(i, npt - 1), 0)),
             pl.BlockSpec((ROW_TILE, D), lambda i, *_: (0, 0))]
    return [xp, xs], specs, xp.shape[0] + xs.shape[0], D, npt


def _read_token_tile(x_refs, n_prompt_tiles):
    if len(x_refs) == 1:
        return x_refs[0][...]
    return jnp.where(pl.program_id(0) < n_prompt_tiles, x_refs[0][...], x_refs[1][...])


def _modulate_kernel(*refs, n_x, n_prompt_tiles):
    sc_ref, sh_ref, h_ref = refs[n_x:]
    x = _read_token_tile(refs[:n_x], n_prompt_tiles)
    h_ref[...] = (x * (1.0 + sc_ref[...]) + sh_ref[...]).astype(h_ref.dtype)


def _modulate(x, mod, sc_chunk, sh_chunk, tiles_per_batch):
    xs, x_specs, T, D, npt = _token_rows(x)
    return pl.pallas_call(
        functools.partial(_modulate_kernel, n_x=len(xs), n_prompt_tiles=npt),
        out_shape=jax.ShapeDtypeStruct((T, D), BF),
        grid=(T // ROW_TILE,),
        in_specs=x_specs + [_mod_spec(tiles_per_batch, D, sc_chunk), _mod_spec(tiles_per_batch, D, sh_chunk)],
        out_specs=pl.BlockSpec((ROW_TILE, D), lambda i: (i, 0)),
        compiler_params=_params(1),
        name="modulate",
    )(*xs, mod, mod)


def _postln_kernel(*refs, n_x, n_prompt_tiles, alpha, modulate_next, emit_bf):
    x = _read_token_tile(refs[:n_x], n_prompt_tiles)
    mix_ref, g_ref, lng_ref, lnb_ref = refs[n_x:n_x + 4]
    k = n_x + 4
    if modulate_next:
        sc_ref, sh_ref = refs[k], refs[k + 1]
        k += 2
    xo_ref = refs[k]; k += 1
    xn = _layernorm_rows(alpha * x + g_ref[...] * mix_ref[...], lng_ref[...], lnb_ref[...])
    xo_ref[...] = xn
    if modulate_next:
        refs[k][...] = (xn * (1.0 + sc_ref[...]) + sh_ref[...]).astype(BF)
        k += 1
    if emit_bf:
        refs[k][...] = xn.astype(BF)


def _postln(x, mix, mod_g, g_chunk, ln_g, ln_b, tiles_per_batch, *, alpha, mod_next=None,
            sc_chunk=None, sh_chunk=None, emit_bf=False):
    xs, x_specs, T, D, npt = _token_rows(x)
    row = pl.BlockSpec((ROW_TILE, D), lambda i: (i, 0))
    vec = pl.BlockSpec((1, D), lambda i: (0, 0))
    in_specs = x_specs + [row, _mod_spec(tiles_per_batch, D, g_chunk), vec, vec]
    args = xs + [mix, mod_g, ln_g.reshape(1, D), ln_b.reshape(1, D)]
    out_shape = [jax.ShapeDtypeStruct((T, D), F32)]
    out_specs = [row]
    if mod_next is not None:
        in_specs += [_mod_spec(tiles_per_batch, D, sc_chunk), _mod_spec(tiles_per_batch, D, sh_chunk)]
        args += [mod_next, mod_next]
        out_shape.append(jax.ShapeDtypeStruct((T, D), BF))
        out_specs.append(row)
    if emit_bf:
        out_shape.append(jax.ShapeDtypeStruct((T, D), BF))
        out_specs.append(row)
    return pl.pallas_call(
        functools.partial(_postln_kernel, n_x=len(xs), n_prompt_tiles=npt, alpha=alpha,
                          modulate_next=mod_next is not None, emit_bf=emit_bf),
        out_shape=out_shape,
        grid=(T // ROW_TILE,),
        in_specs=in_specs,
        out_specs=out_specs,
        compiler_params=_params(1),
        name="postln",
    )(*args)


def _spatial_gate_kernel(vpre_ref, u_ref, ws_ref, b_ref, vg_ref, vb_ref, mix_ref, vout_ref, *, groups):
    v = _layernorm_rows(vpre_ref[...], vg_ref[...], vb_ref[...])

    @pl.when(pl.program_id(0) == pl.num_programs(0) - 1)
    def _():
        vout_ref[...] = v

    vb16 = v.astype(BF)
    gd = v.shape[1] // groups
    for g in range(groups):
        cols = slice(g * gd, (g + 1) * gd)
        s = jnp.dot(ws_ref[g], vb16[:, cols], preferred_element_type=F32) + b_ref[:, cols]
        mix_ref[:, cols] = (u_ref[:, cols].astype(F32) * s).astype(BF)


def _spatial_gate(vpre, u, ws_tiles, b_tiles, v_g, v_b, n_prompt_tiles):
    T, A = vpre.shape
    G = ws_tiles.shape[1]
    row = pl.BlockSpec((ROW_TILE, A), lambda i: (i, 0))
    vec = pl.BlockSpec((1, A), lambda i: (0, 0))
    return pl.pallas_call(
        functools.partial(_spatial_gate_kernel, groups=G),
        out_shape=[jax.ShapeDtypeStruct((T, A), BF), jax.ShapeDtypeStruct((ROW_TILE, A), F32)],
        grid=(T // ROW_TILE,),
        in_specs=[row, row,
                  pl.BlockSpec((None, G, ROW_TILE, ROW_TILE), lambda i: (i // n_prompt_tiles, 0, 0, 0)),
                  pl.BlockSpec((None, ROW_TILE, A), lambda i: (i // n_prompt_tiles, 0, 0)),
                  vec, vec],
        out_specs=[row, pl.BlockSpec((ROW_TILE, A), lambda i: (0, 0))],
        compiler_params=_params(1),
        name="spatial_gate",
    )(vpre, u, ws_tiles, b_tiles, v_g.reshape(1, A), v_b.reshape(1, A))


_CONTRACT_LAST = (((1,), (1,)), ((), ()))


def _kv_kernel(x_ref, wkv_ref, kvg_ref, cos_ref, sin_ref, wuk_ref, wuvt_ref,
               ckv_ref, kpe_ref, kfull_ref, vt_ref, *, c_dim, r_dim, heads, nope):
    y = jnp.dot(x_ref[...], wkv_ref[...], preferred_element_type=F32)
    c = y[:, :c_dim]
    ckv = c * lax.rsqrt(jnp.mean(c * c, axis=-1, keepdims=True) + RMS_EPS) * kvg_ref[...]
    ckv_ref[...] = ckv
    kpe = y[:, c_dim:c_dim + r_dim] * cos_ref[...] + y[:, c_dim + 128:c_dim + 128 + r_dim] * sin_ref[...]
    kpe_ref[...] = kpe
    cb = ckv.astype(BF)
    kn = jnp.dot(cb, wuk_ref[...], preferred_element_type=F32)
    vt_ref[...] = lax.dot_general(wuvt_ref[...], cb, _CONTRACT_LAST, preferred_element_type=F32).astype(BF)
    kpe_pad = jnp.concatenate([kpe, jnp.zeros((kpe.shape[0], 128 - r_dim), F32)], axis=-1).astype(BF)
    for h in range(heads):
        kfull_ref[:, h * 256:h * 256 + nope] = kn[:, h * nope:(h + 1) * nope].astype(BF)
        kfull_ref[:, h * 256 + nope:(h + 1) * 256] = kpe_pad


def _kv_side(xb, wkv, kv_g, cos_k, sin_k, wuk, wuv_t, *, bm, c_dim, r_dim, heads, nope):
    T, D = xb.shape
    assert nope == 128 and r_dim <= 128
    row = lambda w: pl.BlockSpec((bm, w), lambda i: (i, 0))
    full = lambda a: pl.BlockSpec(a.shape, lambda i: (0,) * a.ndim)
    hv = wuv_t.shape[0]
    return pl.pallas_call(
        functools.partial(_kv_kernel, c_dim=c_dim, r_dim=r_dim, heads=heads, nope=nope),
        out_shape=[jax.ShapeDtypeStruct((T, c_dim), F32), jax.ShapeDtypeStruct((T, r_dim), F32),
                   jax.ShapeDtypeStruct((T, heads * 256), BF), jax.ShapeDtypeStruct((hv, T), BF)],
        grid=(T // bm,),
        in_specs=[row(D), full(wkv), full(kv_g), row(r_dim), row(r_dim), full(wuk), full(wuv_t)],
        out_specs=[row(c_dim), row(r_dim), row(heads * 256), pl.BlockSpec((hv, bm), lambda i: (0, i))],
        compiler_params=_params(1),
        name="kv_side",
    )(xb, wkv, kv_g, cos_k, sin_k, wuk, wuv_t)


def _q_kernel(h_ref, wdqt_ref, qg_ref, wuqt_ref, wpet_ref, cost_ref, sint_ref, qt_ref,
              *, heads, nope, scale, half):
    ct = lax.dot_general(wdqt_ref[...], h_ref[...], _CONTRACT_LAST, preferred_element_type=F32)
    cqt = (ct * lax.rsqrt(jnp.mean(ct * ct, axis=0, keepdims=True) + RMS_EPS) * qg_ref[...]).astype(BF)
    qn = jnp.dot(wuqt_ref[...], cqt, preferred_element_type=F32) * scale
    qp = jnp.dot(wpet_ref[...], cqt, preferred_element_type=F32)
    cos = cost_ref[...]
    sin = sint_ref[...]
    for h in range(heads):
        qt_ref[h * 256:h * 256 + nope, :] = qn[h * nope:(h + 1) * nope].astype(BF)
        x = qp[h * 128:(h + 1) * 128]
        rot = (jnp.concatenate([x[128 - half:], x[:128 - half]], axis=0)
               - jnp.concatenate([x[half:], x[:half]], axis=0))
        pe = x * cos + rot * sin
        qt_ref[h * 256 + nope:(h + 1) * 256, :] = pe.astype(BF)


def _q_side(hb, wdq_t, q_g_col, wuq_t, wpe_t, cos_t, sin_t, *, bm, heads, nope, scale, half):
    T, D = hb.shape
    assert nope == 128
    full = lambda a: pl.BlockSpec(a.shape, lambda i: (0,) * a.ndim)
    col = lambda r: pl.BlockSpec((r, bm), lambda i: (0, i))
    return pl.pallas_call(
        functools.partial(_q_kernel, heads=heads, nope=nope, scale=scale, half=half),
        out_shape=jax.ShapeDtypeStruct((heads * 256, T), BF),
        grid=(T // bm,),
        in_specs=[pl.BlockSpec((bm, D), lambda i: (i, 0)), full(wdq_t), full(q_g_col), full(wuq_t),
                  full(wpe_t), col(128), col(128)],
        out_specs=col(heads * 256),
        compiler_params=_params(1),
        name="q_side",
    )(hb, wdq_t, q_g_col, wuq_t, wpe_t, cos_t, sin_t)


def _flash_kernel(qi_ref, ki_ref, k_ref, qt_ref, vt_ref, o_init_ref, o_ref, *scratch, hb, vdim):
    del o_init_ref
    m_refs, l_refs, acc_refs = scratch[:hb], scratch[hb:2 * hb], scratch[2 * hb:]
    step = pl.program_id(2)
    qi = qi_ref[step]
    ki = ki_ref[step]

    @pl.when(ki == 0)
    def _():
        for h in range(hb):
            m_refs[h][...] = jnp.full(m_refs[h].shape, NEG_INF, F32)
            l_refs[h][...] = jnp.zeros(l_refs[h].shape, F32)
            acc_refs[h][...] = jnp.zeros(acc_refs[h].shape, F32)

    def update(masked):
        for h in range(hb):
            qr = slice(h * 256, (h + 1) * 256)
            vr = slice(h * vdim, (h + 1) * vdim)
            st = jnp.dot(k_ref[:, qr], qt_ref[qr, :], preferred_element_type=F32)
            if masked:
                key = lax.broadcasted_iota(jnp.int32, st.shape, 0)
                qry = lax.broadcasted_iota(jnp.int32, st.shape, 1)
                st = jnp.where(key <= qry, st, NEG_INF)
            m_prev = m_refs[h][...]
            m_new = jnp.maximum(m_prev, jnp.max(st, axis=0, keepdims=True))
            corr = jnp.exp2(m_prev - m_new)
            pt = jnp.exp2(st - m_new)
            l_refs[h][...] = corr * l_refs[h][...] + jnp.sum(pt, axis=0, keepdims=True)
            acc_refs[h][...] = corr * acc_refs[h][...] + jnp.dot(vt_ref[vr, :], pt.astype(BF),
                                                                 preferred_element_type=F32)
            m_refs[h][...] = m_new

    @pl.when(ki < qi)
    def _():
        update(False)

    @pl.when(ki == qi)
    def _():
        update(True)
        for h in range(hb):
            vr = slice(h * vdim, (h + 1) * vdim)
            o_ref[:, vr] = (acc_refs[h][...] / l_refs[h][...]).T.astype(o_ref.dtype)


def _flash_attention(kfull, q_t, v_t, *, batch, seq, heads, vdim, blk, hb, total_rows):
    assert heads % hb == 0
    nb = seq // blk
    pairs = [(a, b) for a in range(nb) for b in range(a + 1)]
    qi_map = jnp.asarray([p[0] for p in pairs], jnp.int32)
    ki_map = jnp.asarray([p[1] for p in pairs], jnp.int32)
    grid_spec = pltpu.PrefetchScalarGridSpec(
        num_scalar_prefetch=2,
        grid=(batch, heads // hb, len(pairs)),
        in_specs=[
            pl.BlockSpec((blk, hb * 256), lambda b, h, s, qi, ki: (b * nb + ki[s], h)),
            pl.BlockSpec((hb * 256, blk), lambda b, h, s, qi, ki: (h, b * nb + qi[s])),
            pl.BlockSpec((hb * vdim, blk), lambda b, h, s, qi, ki: (h, b * nb + ki[s])),
            pl.BlockSpec(memory_space=pl.ANY),
        ],
        out_specs=pl.BlockSpec((blk, hb * vdim), lambda b, h, s, qi, ki: (b * nb + qi[s], h)),
        scratch_shapes=[pltpu.VMEM((1, blk), F32)] * (2 * hb) + [pltpu.VMEM((vdim, blk), F32)] * hb,
    )
    return pl.pallas_call(
        functools.partial(_flash_kernel, hb=hb, vdim=vdim),
        out_shape=jax.ShapeDtypeStruct((total_rows, heads * vdim), BF),
        grid_spec=grid_spec,
        input_output_aliases={5: 0},
        compiler_params=_params(3),
        name="flash_prompt",
    )(qi_map, ki_map, kfull, q_t, v_t, jnp.zeros((total_rows, heads * vdim), BF))


def _absorb_kernel(wuk_ref, qt_ref, o_ref):
    o_ref[...] = jnp.dot(wuk_ref[...], qt_ref[...], preferred_element_type=F32).astype(o_ref.dtype)


def _absorb_query(q_t, wuk, *, heads, nope, c_dim, col_block, cols):
    return pl.pallas_call(
        _absorb_kernel,
        out_shape=jax.ShapeDtypeStruct((heads * c_dim, cols), BF),
        grid=(heads,),
        in_specs=[pl.BlockSpec((c_dim, nope), lambda h: (0, h)),
                  pl.BlockSpec((nope, cols), lambda h: (2 * h, col_block))],
        out_specs=pl.BlockSpec((c_dim, cols), lambda h: (h, 0)),
        compiler_params=_params(1),
        name="absorb_query",
    )(wuk, q_t)


def _page_copies(pt_ref, ck_hbm, kp_hbm, ck_buf, kp_buf, sem, step, slot, i, pages):
    page = pt_ref[step * pages + i]
    return (pltpu.make_async_copy(ck_hbm.at[page], ck_buf.at[slot, i], sem.at[0, slot]),
            pltpu.make_async_copy(kp_hbm.at[page], kp_buf.at[slot, i], sem.at[1, slot]))


def _decode_kernel(pt_ref, qlat_ref, qlt_ref, qpe_ref, cnew_ref, knew_ref, ck_hbm, kp_hbm, o_ref,
                   ck_buf, kp_buf, sem, m_ref, l_ref, acc_ref, *, pages, steps_per_seq):
    n = pl.program_id(0)
    j = n % steps_per_seq
    slot = n % DECODE_SLOTS

    def start_step(step):
        for i in range(pages):
            for cp in _page_copies(pt_ref, ck_hbm, kp_hbm, ck_buf, kp_buf, sem, step, step % DECODE_SLOTS, i, pages):
                cp.start()

    @pl.when(n == 0)
    def _():
        for step in range(DECODE_SLOTS - 1):
            @pl.when(step < pl.num_programs(0))
            def _():
                start_step(step)

    @pl.when(n + DECODE_SLOTS - 1 < pl.num_programs(0))
    def _():
        start_step(n + DECODE_SLOTS - 1)

    @pl.when(j == 0)
    def _():
        m_ref[...] = jnp.full(m_ref.shape, NEG_INF, F32)
        l_ref[...] = jnp.zeros(l_ref.shape, F32)
        acc_ref[...] = jnp.zeros(acc_ref.shape, F32)

    for i in range(pages):
        for cp in _page_copies(pt_ref, ck_hbm, kp_hbm, ck_buf, kp_buf, sem, n, slot, i, pages):
            cp.wait()

    ql = qlat_ref[0]
    qlt = qlt_ref[0]
    qp = qpe_ref[0]
    heads = ql.shape[0]
    cks = []
    scores = []
    for i in range(pages):
        ck = ck_buf[slot, i].astype(BF)
        kp = kp_buf[slot, i].astype(BF)
        cks.append(ck)
        st = jnp.dot(ck, qlt, preferred_element_type=F32)
        st = jnp.where(lax.broadcasted_iota(jnp.int32, st.shape, 1) < heads, st, 0.0)
        scores.append(st.T[:heads] + jnp.dot(qp, kp, preferred_element_type=F32))
    s = jnp.concatenate(scores, axis=-1)
    m_prev = m_ref[...]
    m_new = jnp.maximum(m_prev, jnp.max(s, axis=-1, keepdims=True))
    corr = jnp.exp2(m_prev - m_new)
    p = jnp.exp2(s - m_new)
    l_new = corr * l_ref[...] + jnp.sum(p, axis=-1, keepdims=True)
    pb = p.astype(BF)
    rows = cks[0].shape[0]
    pv = jnp.dot(pb[:, :rows], cks[0], preferred_element_type=F32)
    for i in range(1, pages):
        pv = pv + jnp.dot(pb[:, i * rows:(i + 1) * rows], cks[i], preferred_element_type=F32)
    acc_new = corr * acc_ref[...] + pv
    m_ref[...] = m_new
    l_ref[...] = l_new
    acc_ref[...] = acc_new

    @pl.when(j == steps_per_seq - 1)
    def _():
        cn = cnew_ref[0].astype(BF).astype(F32)
        kn = knew_ref[0].astype(BF).astype(F32)
        s_self = (jnp.sum(ql.astype(F32) * cn, axis=-1, keepdims=True)
                  + jnp.sum(qp.astype(F32) * kn, axis=-1, keepdims=True))
        m_fin = jnp.maximum(m_new, s_self)
        c2 = jnp.exp2(m_new - m_fin)
        p_self = jnp.exp2(s_self - m_fin)
        l_fin = c2 * l_new + p_self
        acc_fin = c2 * acc_new + p_self.astype(BF).astype(F32) * cn
        o_ref[0] = (acc_fin / l_fin).astype(o_ref.dtype)


def _decode_attention(page_table, q_lat, q_lat_t, q_pe, ckv_new, kpe_new, cache_ckv, cache_kpe_t, *, pages):
    Bd, H, C = q_lat.shape
    R = q_pe.shape[2]
    n_pages = page_table.shape[1]
    page_rows = cache_ckv.shape[1]
    assert n_pages % pages == 0
    sps = n_pages // pages
    per_b = lambda *shape: pl.BlockSpec((1,) + shape, lambda n, pt: (n // sps, 0, 0))
    hbm = pl.BlockSpec(memory_space=pl.ANY)
    grid_spec = pltpu.PrefetchScalarGridSpec(
        num_scalar_prefetch=1,
        grid=(Bd * sps,),
        in_specs=[per_b(H, C), per_b(C, 128), per_b(H, R), per_b(1, C), per_b(1, R), hbm, hbm],
        out_specs=per_b(H, C),
        scratch_shapes=[pltpu.VMEM((DECODE_SLOTS, pages, page_rows, C), cache_ckv.dtype),
                        pltpu.VMEM((DECODE_SLOTS, pages, R, page_rows), cache_kpe_t.dtype),
                        pltpu.SemaphoreType.DMA((2, DECODE_SLOTS)),
                        pltpu.VMEM((H, 1), F32), pltpu.VMEM((H, 1), F32), pltpu.VMEM((H, C), F32)],
    )
    return pl.pallas_call(
        functools.partial(_decode_kernel, pages=pages, steps_per_seq=sps),
        out_shape=jax.ShapeDtypeStruct((Bd, H, C), BF),
        grid_spec=grid_spec,
        compiler_params=_params(1),
        name="decode_attention",
    )(page_table.reshape(-1), q_lat, q_lat_t, q_pe, ckv_new, kpe_new, cache_ckv, cache_kpe_t)


def _value_up_kernel(o_hbm_ref, olat_ref, wuv_ref, o_ref):
    del o_hbm_ref
    o_ref[...] = jnp.dot(olat_ref[...], wuv_ref[...], preferred_element_type=F32).astype(o_ref.dtype)


def _value_up_into(o_all, o_lat, wuv, *, heads, c_dim, vdim, row_block, rows):
    return pl.pallas_call(
        _value_up_kernel,
        out_shape=jax.ShapeDtypeStruct(o_all.shape, o_all.dtype),
        grid=(heads,),
        in_specs=[pl.BlockSpec(memory_space=pl.ANY),
                  pl.BlockSpec((rows, c_dim), lambda h: (0, h)),
                  pl.BlockSpec((c_dim, vdim), lambda h: (0, h))],
        out_specs=pl.BlockSpec((rows, vdim), lambda h: (row_block, h)),
        input_output_aliases={0: 0},
        compiler_params=_params(1),
        name="value_up",
    )(o_all, o_lat, wuv)


def _router_kernel(x_ref, mix_ref, g_ref, lng_ref, lnb_ref, sc_ref, sh_ref, wrh_ref, wrl_ref, br_ref,
                   xo_ref, w_ref, i_ref, h_ref, *, n_exp, alpha):
    xn = _layernorm_rows(alpha * x_ref[...] + g_ref[...] * mix_ref[...], lng_ref[...], lnb_ref[...])
    xo_ref[...] = xn
    h = xn * (1.0 + sc_ref[...]) + sh_ref[...]
    h_hi = h.astype(BF)
    h_lo = (h - h_hi.astype(F32)).astype(BF)
    logits = (jnp.dot(h_hi, wrh_ref[...], preferred_element_type=F32)
              + jnp.dot(h_hi, wrl_ref[...], preferred_element_type=F32)
              + jnp.dot(h_lo, wrh_ref[...], preferred_element_type=F32)) + br_ref[...]
    lane = lax.broadcasted_iota(jnp.int32, logits.shape, 1)
    valid = lane < n_exp
    logits = jnp.where(valid, logits, NEG_INF)
    mx = jnp.max(logits, axis=-1, keepdims=True)
    e = jnp.exp(logits - mx)
    probs = e / jnp.sum(e, axis=-1, keepdims=True)
    big = jnp.int32(1 << 30)
    p1 = jnp.max(probs, axis=-1, keepdims=True)
    i1 = jnp.min(jnp.where((probs == p1) & valid, lane, big), axis=-1, keepdims=True)
    rest = jnp.where((lane == i1) | ~valid, -1.0, probs)
    p2 = jnp.max(rest, axis=-1, keepdims=True)
    i2 = jnp.min(jnp.where(rest == p2, lane, big), axis=-1, keepdims=True)
    denom = p1 + p2
    w_ref[...] = jnp.where(lane == 0, p1 / denom, jnp.where(lane == 1, p2 / denom, 0.0))
    i_ref[...] = jnp.where(lane == 0, i1, jnp.where(lane == 1, i2, 0))
    h_ref[...] = h


def _postln_router(x, mix, mod, g_chunk, ln_g, ln_b, sc_chunk, sh_chunk, wr_pad, br_pad, tiles_per_batch, n_exp,
                   *, alpha):
    T, D = x.shape
    row = pl.BlockSpec((ROW_TILE, D), lambda i: (i, 0))
    vec = pl.BlockSpec((1, D), lambda i: (0, 0))
    sel = pl.BlockSpec((ROW_TILE, 128), lambda i: (i, 0))
    wr_hi = wr_pad.astype(BF)
    wr_lo = (wr_pad - wr_hi.astype(F32)).astype(BF)
    wspec = pl.BlockSpec((D, 128), lambda i: (0, 0))
    return pl.pallas_call(
        functools.partial(_router_kernel, n_exp=n_exp, alpha=alpha),
        out_shape=[jax.ShapeDtypeStruct((T, D), F32), jax.ShapeDtypeStruct((T, 128), F32),
                   jax.ShapeDtypeStruct((T, 128), jnp.int32), jax.ShapeDtypeStruct((T, D), F32)],
        grid=(T // ROW_TILE,),
        in_specs=[row, row, _mod_spec(tiles_per_batch, D, g_chunk), vec, vec,
                  _mod_spec(tiles_per_batch, D, sc_chunk), _mod_spec(tiles_per_batch, D, sh_chunk),
                  wspec, wspec, pl.BlockSpec((1, 128), lambda i: (0, 0))],
        out_specs=[row, sel, sel, row],
        compiler_params=_params(1),
        name="postln_router",
    )(x, mix, mod, ln_g.reshape(1, D), ln_b.reshape(1, D), mod, mod, wr_hi, wr_lo, br_pad)


def _route_tables(idx2, n_exp, bm, n_blocks):
    T = idx2.shape[0]
    e_flat = idx2.reshape(-1)
    onehot = (e_flat[:, None] == jnp.arange(n_exp, dtype=jnp.int32)[None, :]).astype(jnp.int32)
    rank = jnp.sum((jnp.cumsum(onehot, axis=0) - onehot) * onehot, axis=1)
    counts = jnp.sum(onehot, axis=0)
    blocks_per = (counts + bm - 1) // bm
    blocks_end = jnp.cumsum(blocks_per)
    dest = (blocks_end - blocks_per)[e_flat] * bm + rank
    n_used = blocks_end[-1]
    b = jnp.minimum(jnp.arange(n_blocks, dtype=jnp.int32), n_used - 1)
    block_expert = jnp.sum((b[:, None] >= blocks_end[None, :]).astype(jnp.int32), axis=1)
    row_token = jnp.zeros((n_blocks * bm,), jnp.int32).at[dest].set(jnp.arange(2 * T, dtype=jnp.int32) // 2)
    ids = jnp.arange(n_exp, dtype=jnp.int32)
    later_used = (ids[None, :] > ids[:, None]) & (blocks_per[None, :] > 0)
    next_of_expert = jnp.min(jnp.where(later_used, ids[None, :], n_exp), axis=1)
    next_expert = jnp.where(next_of_expert < n_exp, next_of_expert, -1)[block_expert]
    return (row_token, dest.astype(jnp.int32), block_expert.astype(jnp.int32), next_expert.astype(jnp.int32),
            n_used.reshape(1).astype(jnp.int32))


def _row_copy(src_hbm, row, dst, sem):
    return pltpu.make_async_copy(src_hbm.at[pl.ds(row, 1)], dst, sem)


ROW_DMA_UNROLL = 8


def _dispatch_kernel(tok_ref, h_hbm, o_ref, buf, sem, *, rows):
    i = pl.program_id(0)
    slot = i % 2

    def issue(step, s):
        def body(r8, c):
            for k in range(ROW_DMA_UNROLL):
                r = r8 * ROW_DMA_UNROLL + k
                _row_copy(h_hbm, tok_ref[step * rows + r], buf.at[s, pl.ds(r, 1)], sem.at[s]).start(priority=k % 2)
            return c
        lax.fori_loop(0, rows // ROW_DMA_UNROLL, body, 0)

    @pl.when(i == 0)
    def _():
        issue(0, 0)

    @pl.when(i + 1 < pl.num_programs(0))
    def _():
        issue(i + 1, 1 - slot)

    def wait_body(r, c):
        _row_copy(h_hbm, 0, buf.at[slot, pl.ds(r, 1)], sem.at[slot]).wait()
        return c
    lax.fori_loop(0, rows, wait_body, 0, unroll=8)
    o_ref[...] = buf[slot].astype(o_ref.dtype)


def _dispatch(h, row_token, *, rows=256):
    T, D = h.shape
    Rn = row_token.shape[0]
    if Rn % rows:
        rows = ROW_TILE
    assert Rn % rows == 0 and rows % ROW_DMA_UNROLL == 0
    grid_spec = pltpu.PrefetchScalarGridSpec(
        num_scalar_prefetch=1,
        grid=(Rn // rows,),
        in_specs=[pl.BlockSpec(memory_space=pl.ANY)],
        out_specs=pl.BlockSpec((rows, D), lambda i, tok: (i, 0)),
        scratch_shapes=[pltpu.VMEM((2, rows, D), h.dtype), pltpu.SemaphoreType.DMA((2,))],
    )
    return pl.pallas_call(
        functools.partial(_dispatch_kernel, rows=rows),
        out_shape=jax.ShapeDtypeStruct((Rn, D), h.dtype),
        grid_spec=grid_spec,
        compiler_params=_params(1),
        name="moe_dispatch",
    )(row_token, h)


def _combine_postln_kernel(dest_ref, x_ref, g_ref, lng_ref, lnb_ref, w_ref, y_hbm, yp_ref, ys_ref, buf, sem,
                           *, alpha, rows, n_prompt_tiles):
    i = pl.program_id(0)
    slot = i % 2

    def issue(step, s):
        def body(r, c):
            a = (step * rows + r) * 2
            _row_copy(y_hbm, dest_ref[a], buf.at[s, 0, pl.ds(r, 1)], sem.at[s]).start(priority=0)
            _row_copy(y_hbm, dest_ref[a + 1], buf.at[s, 1, pl.ds(r, 1)], sem.at[s]).start(priority=1)
            return c
        lax.fori_loop(0, rows, body, 0, unroll=8)

    @pl.when(i == 0)
    def _():
        issue(0, 0)

    @pl.when(i + 1 < pl.num_programs(0))
    def _():
        issue(i + 1, 1 - slot)

    def wait_body(r, c):
        _row_copy(y_hbm, 0, buf.at[slot, 0, pl.ds(r, 1)], sem.at[slot]).wait()
        _row_copy(y_hbm, 0, buf.at[slot, 1, pl.ds(r, 1)], sem.at[slot]).wait()
        return c
    lax.fori_loop(0, rows, wait_body, 0, unroll=8)
    f = w_ref[:, 0:1] * buf[slot, 0] + w_ref[:, 1:2] * buf[slot, 1]
    xn = _layernorm_rows(alpha * x_ref[...] + g_ref[...] * f, lng_ref[...], lnb_ref[...])

    @pl.when(i < n_prompt_tiles)
    def _():
        yp_ref[...] = xn

    @pl.when(i >= n_prompt_tiles)
    def _():
        ys_ref[...] = xn


def _combine_postln(x, y, dest, wts, mod_g, g_chunk, ln_g, ln_b, tiles_per_batch, n_prompt_tiles, *, alpha):
    T, D = x.shape
    rows = ROW_TILE
    assert T == (n_prompt_tiles + 1) * rows
    row = pl.BlockSpec((rows, D), lambda i, d: (i, 0))
    vec = pl.BlockSpec((1, D), lambda i, d: (0, 0))
    grid_spec = pltpu.PrefetchScalarGridSpec(
        num_scalar_prefetch=1,
        grid=(T // rows,),
        in_specs=[row, pl.BlockSpec((rows, D), lambda i, d: (i // tiles_per_batch, g_chunk)), vec, vec,
                  pl.BlockSpec((rows, 128), lambda i, d: (i, 0)), pl.BlockSpec(memory_space=pl.ANY)],
        out_specs=[pl.BlockSpec((rows, D), lambda i, d: (jnp.minimum(i, n_prompt_tiles - 1), 0)),
                   pl.BlockSpec((rows, D), lambda i, d: (0, 0))],
        scratch_shapes=[pltpu.VMEM((2, 2, rows, D), F32), pltpu.SemaphoreType.DMA((2,))],
    )
    return pl.pallas_call(
        functools.partial(_combine_postln_kernel, alpha=alpha, rows=rows, n_prompt_tiles=n_prompt_tiles),
        out_shape=[jax.ShapeDtypeStruct((n_prompt_tiles * rows, D), F32), jax.ShapeDtypeStruct((rows, D), F32)],
        grid_spec=grid_spec,
        compiler_params=_params(1),
        name="moe_combine_postln",
    )(dest, x, mod_g, ln_g.reshape(1, D), ln_b.reshape(1, D), wts, y)


def _pick_bm(T, target):
    best = ROW_TILE
    for m in range(ROW_TILE, target + 1, ROW_TILE):
        if T % m == 0:
            best = m
    return best


def kernel(x_prompt, x_sample, cache_ckv, cache_kpe, page_table, c_prompt, c_sample, w_ada, b_ada, ln_g, ln_b, a_w_in, a_v_g, a_v_b, a_w_s, a_b_s, a_w_out, b_w_dq, b_q_g, b_w_uq, b_w_qr, b_w_o, kv_w_dkv, kv_g, kv_w_kr, kv_w_uk, kv_w_uv, ffn_w_gate, ffn_w_up, ffn_w_down, moe_w_router, moe_b_router, moe_w_gate, moe_w_up, moe_w_down):
    B, S, D = x_prompt.shape
    Bd, Td, _ = x_sample.shape
    depth = w_ada.shape[0]
    n_a = a_w_in.shape[0]
    assert depth == 2 and n_a == 1 and Td == 1 and Bd == ROW_TILE and S % ROW_TILE == 0
    A = a_w_in.shape[2] // 2
    G = a_w_s.shape[1]
    H, nope = b_w_uq.shape[2], b_w_uq.shape[3]
    R = b_w_qr.shape[3]
    C = kv_w_dkv.shape[1]
    vdim = kv_w_uv.shape[2]
    E = moe_w_router.shape[2]
    n_pages, page_rows = page_table.shape[1], cache_ckv.shape[1]
    past_len = n_pages * page_rows
    alpha = (2.0 * depth) ** 0.25
    scale = (nope + R) ** -0.5
    Tp = B * S
    T = Tp + Bd
    tiles_per_batch = S // ROW_TILE
    n_prompt_tiles = Tp // ROW_TILE
    bm = _pick_bm(T, 640)
    bm_mid = _pick_bm(T, 832)
    bm_big = _pick_bm(T, 1664)

    x = (x_prompt.reshape(Tp, D), x_sample.reshape(Bd, D))

    c_tiles = jnp.concatenate([jnp.repeat(c_prompt, ROW_TILE, axis=0), c_sample], axis=0)
    mods = [_wsmm(c_tiles, [w_ada], bm=c_tiles.shape[0], bn=512, out_dtype=F32, act="silu_in",
                  bias=b_ada[l].reshape(1, -1), w_index=l, name="adaln") for l in range(depth)]
    SH_M, SC_M, G_M, SH_F, SC_F, G_F = range(6)

    half = R // 2
    inv = ROPE_BASE ** (-jnp.arange(half, dtype=F32) / half)
    pos = jnp.concatenate([jnp.tile(jnp.arange(S, dtype=jnp.int32), B),
                           jnp.full((Bd,), past_len, jnp.int32)]).astype(F32)
    ang = pos[:, None] * inv[None, :]
    cos2 = jnp.concatenate([jnp.cos(ang), jnp.cos(ang)], axis=-1)
    sin2 = jnp.concatenate([jnp.sin(ang), jnp.sin(ang)], axis=-1)
    qscale = scale * math.log2(math.e)
    row_pad = jnp.zeros((128 - R, T), F32)
    cos_qt = jnp.concatenate([cos2.T * qscale, row_pad], axis=0)
    sin_qt = jnp.concatenate([sin2.T * qscale, row_pad], axis=0)

    h = _modulate(x, mods[0], SC_M, SH_M, tiles_per_batch)
    u = _wsmm(h, [a_w_in], bm=bm_big, bn=512, out_dtype=BF, n_off=0, n_out=A, act="gelu", name="a_in_u")
    vpre = _wsmm(h, [a_w_in], bm=bm_big, bn=512, out_dtype=F32, n_off=A, n_out=A, act="gelu", name="a_in_v")
    tril = jnp.tril(jnp.ones((ROW_TILE, ROW_TILE), bool))
    ws_prompt = jnp.where(tril, a_w_s[0], 0.0)
    ws_sample = a_w_s[0][:, :1, :1] * jnp.eye(ROW_TILE, dtype=F32)
    ws_tiles = jnp.stack([ws_prompt, ws_sample]).astype(BF)
    gd = A // G
    b_prompt = jnp.repeat(a_b_s[0].T, gd, axis=1)
    b_sample = jnp.broadcast_to(b_prompt[:1], b_prompt.shape)
    b_tiles = jnp.stack([b_prompt, b_sample])
    gated, v_sample = _spatial_gate(vpre, u, ws_tiles, b_tiles, a_v_g[0], a_v_b[0], n_prompt_tiles)
    mix = _wsmm(gated, [a_w_out], bm=bm_mid, bn=512, out_dtype=F32, name="a_out")
    x, h = _postln(x, mix, mods[0], G_M, ln_g[0, 0], ln_b[0, 0], tiles_per_batch, alpha=alpha,
                   mod_next=mods[0], sc_chunk=SC_F, sh_chunk=SH_F)
    act = _wsmm(h, [ffn_w_gate, ffn_w_up], bm=bm_big, bn=512, out_dtype=BF, name="ffn_up")
    f = _wsmm(act, [ffn_w_down], bm=bm, bn=512, out_dtype=F32, name="ffn_down")
    x, h, xb = _postln(x, f, mods[0], G_F, ln_g[0, 1], ln_b[0, 1], tiles_per_batch, alpha=alpha,
                       mod_next=mods[1], sc_chunk=SC_M, sh_chunk=SH_M, emit_bf=True)

    w1, w2 = kv_w_kr[:, :half], kv_w_kr[:, half:]
    zpad = jnp.zeros((D, 128 - R), F32)
    wkv = jnp.concatenate([kv_w_dkv, kv_w_kr, zpad, -w2, w1, zpad], axis=1).astype(BF)
    wuk2 = kv_w_uk.reshape(C, H * nope)
    wuv2 = kv_w_uv.reshape(C, H * vdim)
    ckv, kpe, kfull, v_t = _kv_side(xb, wkv, kv_g.reshape(1, C), cos2, sin2, wuk2.astype(BF), wuv2.T.astype(BF),
                                    bm=bm, c_dim=C, r_dim=R, heads=H, nope=nope)

    wqr = b_w_qr[0]
    qpad = jnp.zeros(wqr.shape[:2] + (128 - R,), F32)
    wpe = jnp.concatenate([wqr, qpad], axis=-1).reshape(wqr.shape[0], H * 128)
    q_t = _q_side(h, b_w_dq[0].T.astype(BF), b_q_g[0].reshape(-1, 1), b_w_uq[0].reshape(-1, H * nope).T.astype(BF),
                  wpe.T.astype(BF), cos_qt, sin_qt, bm=bm, heads=H, nope=nope, scale=qscale, half=half)
    o_all = _flash_attention(kfull, q_t, v_t, batch=B, seq=S, heads=H, vdim=vdim, blk=min(1024, S), hb=min(4, H),
                             total_rows=T)

    q_lat_hc = _absorb_query(q_t, wuk2.astype(BF), heads=H, nope=nope, c_dim=C,
                             col_block=Tp // Bd, cols=Bd).reshape(H, C, Bd)
    q_lat = q_lat_hc.transpose(2, 0, 1)
    q_lat_t = jnp.pad(q_lat_hc.transpose(2, 1, 0), ((0, 0), (0, 0), (0, 128 - H)))
    q_pe_s = q_t[:, Tp:].reshape(H, 256, Bd)[:, nope:nope + R, :].transpose(2, 0, 1)
    o_lat = _decode_attention(page_table, q_lat, q_lat_t, q_pe_s, ckv[Tp:].reshape(Bd, 1, C),
                              kpe[Tp:].reshape(Bd, 1, R), cache_ckv, jnp.swapaxes(cache_kpe, 1, 2),
                              pages=min(32, n_pages))
    o_all = _value_up_into(o_all, o_lat.reshape(Bd, H * C), wuv2.astype(BF), heads=H, c_dim=C, vdim=vdim,
                           row_block=Tp // Bd, rows=Bd)
    mix = _wsmm(o_all, [b_w_o], bm=bm_big, bn=512, out_dtype=F32, name="attn_out")
    wr_pad = jnp.concatenate([moe_w_router[0], jnp.zeros((D, 128 - E), F32)], axis=1)
    br_pad = jnp.concatenate([moe_b_router[0], jnp.zeros((128 - E,), F32)]).reshape(1, 128)
    x, wts, idx, hf = _postln_router(x, mix, mods[1], G_M, ln_g[1, 0], ln_b[1, 0], SC_F, SH_F, wr_pad, br_pad,
                                     tiles_per_batch, E, alpha=alpha)
    bme = 384
    n_blocks = pl.cdiv(2 * T + E * (bme - 1), bme)
    row_token, dest, block_expert, next_expert, n_used = _route_tables(idx[:, :2], E, bme, n_blocks)
    grouped = dict(block_w=block_expert, next_w=next_expert, n_used=n_used)
    hs = _dispatch(hf, row_token)
    per_expert = lambda w: w.reshape((-1,) + w.shape[-2:])
    act = _wsmm(hs, [per_expert(moe_w_gate), per_expert(moe_w_up)], bm=bme, bn=1024, out_dtype=BF,
                name="moe_up", **grouped)
    ys = _wsmm(act, [per_expert(moe_w_down)], bm=bme, bn=512, out_dtype=F32, name="moe_down", **grouped)
    y_prompt, y_sample = _combine_postln(x, ys, dest, wts, mods[1], G_F, ln_g[1, 1], ln_b[1, 1], tiles_per_batch,
                                         n_prompt_tiles, alpha=alpha)
    return (y_prompt.reshape(B, S, D), y_sample.reshape(Bd, Td, D),
            ckv[:Tp].reshape(B, S, C), kpe[:Tp].reshape(B, S, R),
            ckv[Tp:].reshape(Bd, Td, C), kpe[Tp:].reshape(Bd, Td, R),
            v_sample.reshape(n_a, Bd, Td, A))
```

```python
import functools
import math

import jax
import jax.numpy as jnp
from jax import lax
from jax.experimental import pallas as pl
from jax.experimental.pallas import tpu as pltpu

BF = jnp.bfloat16
F32 = jnp.float32

ROW_TILE = 128
LN_EPS = 1e-5
RMS_EPS = 1e-6
ROPE_BASE = 10000.0
NEG_INF = float("-inf")
VMEM_LIMIT = 56 * 1024 * 1024
DECODE_SLOTS = 3

def _params(n_axes, vmem=VMEM_LIMIT):
    return pltpu.CompilerParams(dimension_semantics=("arbitrary",) * n_axes, vmem_limit_bytes=vmem)


def _gelu_tanh(x):
    return 0.5 * x * (1.0 + jnp.tanh(math.sqrt(2.0 / math.pi) * (x + 0.044715 * x * x * x)))


def _silu(x):
    return x * (1.0 / (1.0 + jnp.exp(-x)))


def _layernorm_rows(x, g, b):
    mu = jnp.mean(x, axis=-1, keepdims=True)
    xc = x - mu
    var = jnp.mean(xc * xc, axis=-1, keepdims=True)
    return xc * lax.rsqrt(var + LN_EPS) * g + b


def _group_starts(bw_ref, i):
    return (i == 0) | (bw_ref[i] != bw_ref[jnp.maximum(i - 1, 0)])


def _wsmm_kernel(bw_ref, nxt_ref, nu_ref, x_ref, *refs, n_w, act, has_bias, bn, n_off, row_chunks):
    w_hbm = refs[:n_w]
    k = n_w
    bias_ref = None
    if has_bias:
        bias_ref = refs[k]
        k += 1
    o_ref = refs[k]
    stage = refs[k + 1:k + 1 + n_w]
    wbf = refs[k + 1 + n_w:k + 1 + 2 * n_w]
    sem, cnt = refs[k + 1 + 2 * n_w], refs[k + 2 + 2 * n_w]
    j = pl.program_id(0)
    i = pl.program_id(1)

    def weight_copies(w_idx, jj, slot):
        col = pl.multiple_of(jj * bn + n_off, 128)
        return [pltpu.make_async_copy(w_hbm[t].at[w_idx, :, pl.ds(col, bn)], stage[t].at[slot], sem.at[t, slot])
                for t in range(n_w)]

    @pl.when((j == 0) & (i == 0))
    def _():
        cnt[0] = 0
        for cp in weight_copies(bw_ref[0], 0, 0):
            cp.start()

    @pl.when(_group_starts(bw_ref, i))
    def _():
        slot = cnt[0] % 2
        for cp in weight_copies(bw_ref[i], j, slot):
            cp.wait()
        more = nxt_ref[i] >= 0
        nw = jnp.where(more, nxt_ref[i], bw_ref[0])
        nj = jnp.where(more, j, j + 1)

        @pl.when(nj < pl.num_programs(0))
        def _():
            for cp in weight_copies(nw, nj, 1 - slot):
                cp.start()

        for t in range(n_w):
            wbf[t][...] = stage[t][slot].astype(BF)
        cnt[0] = cnt[0] + 1

    @pl.when(i < nu_ref[0])
    def _():
        rows = x_ref.shape[0] // row_chunks
        for c in range(row_chunks):
            rs = slice(c * rows, (c + 1) * rows)
            x = x_ref[rs, :]
            if act == "silu_in":
                x = _silu(x.astype(F32))
            x = x.astype(BF)
            acc = jnp.dot(x, wbf[0][...], preferred_element_type=F32)
            if n_w == 2:
                acc = _silu(acc) * jnp.dot(x, wbf[1][...], preferred_element_type=F32)
            if has_bias:
                acc = acc + bias_ref[...]
            if act == "gelu":
                acc = _gelu_tanh(acc)
            o_ref[rs, :] = acc.astype(o_ref.dtype)

    @pl.when(i >= nu_ref[0])
    def _():
        o_ref[...] = jnp.zeros(o_ref.shape, o_ref.dtype)


def _wsmm(x, ws, *, bm, bn, out_dtype, block_w=None, next_w=None, n_used=None, w_index=0, n_off=0,
          n_out=None, act=None, bias=None, name="wsmm"):
    M, K = x.shape
    N = ws[0].shape[-1] if n_out is None else n_out
    bn = min(bn, N)
    assert M % bm == 0 and N % bn == 0 and n_off % bn == 0 and all(w.ndim == 3 and w.shape[1] == K for w in ws)
    n_i = M // bm
    if block_w is None:
        block_w = jnp.full((n_i,), w_index, jnp.int32)
        next_w = jnp.full((n_i,), -1, jnp.int32)
        n_used = jnp.full((1,), n_i, jnp.int32)
    n_w = len(ws)
    offb = n_off // bn
    row_chunks = max(c for c in (1, 2, 4) if bm % (16 * c) == 0 and (c == 1 or bm // c >= 384))
    in_specs = [pl.BlockSpec((bm, K), lambda j, i, *_: (i, 0))] + [pl.BlockSpec(memory_space=pl.ANY)] * n_w
    args = [x, *ws]
    if bias is not None:
        in_specs.append(pl.BlockSpec((1, bn), lambda j, i, *_: (0, j + offb)))
        args.append(bias)
    grid_spec = pltpu.PrefetchScalarGridSpec(
        num_scalar_prefetch=3,
        grid=(N // bn, n_i),
        in_specs=in_specs,
        out_specs=pl.BlockSpec((bm, bn), lambda j, i, *_: (i, j)),
        scratch_shapes=[pltpu.VMEM((2, K, bn), ws[0].dtype)] * n_w + [pltpu.VMEM((K, bn), BF)] * n_w
        + [pltpu.SemaphoreType.DMA((n_w, 2)), pltpu.SMEM((1,), jnp.int32)],
    )
    return pl.pallas_call(
        functools.partial(_wsmm_kernel, n_w=n_w, act=act, has_bias=bias is not None, bn=bn, n_off=n_off,
                          row_chunks=row_chunks),
        out_shape=jax.ShapeDtypeStruct((M, N), out_dtype),
        grid_spec=grid_spec,
        compiler_params=_params(2),
        name=name,
    )(block_w, next_w, n_used, *args)


def _mod_spec(tiles_per_batch, D, chunk):
    return pl.BlockSpec((ROW_TILE, D), lambda i: (i // tiles_per_batch, chunk))


def _token_rows(x):
    if not isinstance(x, tuple):
        T, D = x.shape
        return [x], [pl.BlockSpec((ROW_TILE, D), lambda i, *_: (i, 0))], T, D, None
    xp, xs = x
    npt = xp.shape[0] // ROW_TILE
    assert xs.shape[0] == ROW_TILE
    D = xp.shape[1]
    specs = [pl.BlockSpec((ROW_TILE, D), lambda i, *_: (jnp.minimum(i, npt - 1), 0)),
             pl.BlockSpec((ROW_TILE, D), lambda i, *_: (0, 0))]
    return [xp, xs], specs, xp.shape[0] + xs.shape[0], D, npt


def _read_token_tile(x_refs, n_prompt_tiles):
    if len(x_refs) == 1:
        return x_refs[0][...]
    return jnp.where(pl.program_id(0) < n_prompt_tiles, x_refs[0][...], x_refs[1][...])


def _modulate_kernel(*refs, n_x, n_prompt_tiles):
    sc_ref, sh_ref, h_ref = refs[n_x:]
    x = _read_token_tile(refs[:n_x], n_prompt_tiles)
    h_ref[...] = (x * (1.0 + sc_ref[...]) + sh_ref[...]).astype(h_ref.dtype)


def _modulate(x, mod, sc_chunk, sh_chunk, tiles_per_batch):
    xs, x_specs, T, D, npt = _token_rows(x)
    return pl.pallas_call(
        functools.partial(_modulate_kernel, n_x=len(xs), n_prompt_tiles=npt),
        out_shape=jax.ShapeDtypeStruct((T, D), BF),
        grid=(T // ROW_TILE,),
        in_specs=x_specs + [_mod_spec(tiles_per_batch, D, sc_chunk), _mod_spec(tiles_per_batch, D, sh_chunk)],
        out_specs=pl.BlockSpec((ROW_TILE, D), lambda i: (i, 0)),
        compiler_params=_params(1),
        name="modulate",
    )(*xs, mod, mod)


def _postln_kernel(*refs, n_x, n_prompt_tiles, alpha, modulate_next, emit_bf):
    x = _read_token_tile(refs[:n_x], n_prompt_tiles)
    mix_ref, g_ref, lng_ref, lnb_ref = refs[n_x:n_x + 4]
    k = n_x + 4
    if modulate_next:
        sc_ref, sh_ref = refs[k], refs[k + 1]
        k += 2
    xo_ref = refs[k]; k += 1
    xn = _layernorm_rows(alpha * x + g_ref[...] * mix_ref[...], lng_ref[...], lnb_ref[...])
    xo_ref[...] = xn
    if modulate_next:
        refs[k][...] = (xn * (1.0 + sc_ref[...]) + sh_ref[...]).astype(BF)
        k += 1
    if emit_bf:
        refs[k][...] = xn.astype(BF)


def _postln(x, mix, mod_g, g_chunk, ln_g, ln_b, tiles_per_batch, *, alpha, mod_next=None,
            sc_chunk=None, sh_chunk=None, emit_bf=False):
    xs, x_specs, T, D, npt = _token_rows(x)
    row = pl.BlockSpec((ROW_TILE, D), lambda i: (i, 0))
    vec = pl.BlockSpec((1, D), lambda i: (0, 0))
    in_specs = x_specs + [row, _mod_spec(tiles_per_batch, D, g_chunk), vec, vec]
    args = xs + [mix, mod_g, ln_g.reshape(1, D), ln_b.reshape(1, D)]
    out_shape = [jax.ShapeDtypeStruct((T, D), F32)]
    out_specs = [row]
    if mod_next is not None:
        in_specs += [_mod_spec(tiles_per_batch, D, sc_chunk), _mod_spec(tiles_per_batch, D, sh_chunk)]
        args += [mod_next, mod_next]
        out_shape.append(jax.ShapeDtypeStruct((T, D), BF))
        out_specs.append(row)
    if emit_bf:
        out_shape.append(jax.ShapeDtypeStruct((T, D), BF))
        out_specs.append(row)
    return pl.pallas_call(
        functools.partial(_postln_kernel, n_x=len(xs), n_prompt_tiles=npt, alpha=alpha,
                          modulate_next=mod_next is not None, emit_bf=emit_bf),
        out_shape=out_shape,
        grid=(T // ROW_TILE,),
        in_specs=in_specs,
        out_specs=out_specs,
        compiler_params=_params(1),
        name="postln",
    )(*args)


def _spatial_gate_kernel(vpre_ref, u_ref, ws_ref, b_ref, vg_ref, vb_ref, mix_ref, vout_ref, *, groups):
    v = _layernorm_rows(vpre_ref[...], vg_ref[...], vb_ref[...])

    @pl.when(pl.program_id(0) == pl.num_programs(0) - 1)
    def _():
        vout_ref[...] = v

    vb16 = v.astype(BF)
    gd = v.shape[1] // groups
    for g in range(groups):
        cols = slice(g * gd, (g + 1) * gd)
        s = jnp.dot(ws_ref[g], vb16[:, cols], preferred_element_type=F32) + b_ref[:, cols]
        mix_ref[:, cols] = (u_ref[:, cols].astype(F32) * s).astype(BF)


def _spatial_gate(vpre, u, ws_tiles, b_tiles, v_g, v_b, n_prompt_tiles):
    T, A = vpre.shape
    G = ws_tiles.shape[1]
    row = pl.BlockSpec((ROW_TILE, A), lambda i: (i, 0))
    vec = pl.BlockSpec((1, A), lambda i: (0, 0))
    return pl.pallas_call(
        functools.partial(_spatial_gate_kernel, groups=G),
        out_shape=[jax.ShapeDtypeStruct((T, A), BF), jax.ShapeDtypeStruct((ROW_TILE, A), F32)],
        grid=(T // ROW_TILE,),
        in_specs=[row, row,
                  pl.BlockSpec((None, G, ROW_TILE, ROW_TILE), lambda i: (i // n_prompt_tiles, 0, 0, 0)),
                  pl.BlockSpec((None, ROW_TILE, A), lambda i: (i // n_prompt_tiles, 0, 0)),
                  vec, vec],
        out_specs=[row, pl.BlockSpec((ROW_TILE, A), lambda i: (0, 0))],
        compiler_params=_params(1),
        name="spatial_gate",
    )(vpre, u, ws_tiles, b_tiles, v_g.reshape(1, A), v_b.reshape(1, A))


_CONTRACT_LAST = (((1,), (1,)), ((), ()))


def _kv_kernel(x_ref, wkv_ref, kvg_ref, cos_ref, sin_ref, wuk_ref, wuvt_ref,
               ckv_ref, kpe_ref, kfull_ref, vt_ref, *, c_dim, r_dim, heads, nope):
    y = jnp.dot(x_ref[...], wkv_ref[...], preferred_element_type=F32)
    c = y[:, :c_dim]
    ckv = c * lax.rsqrt(jnp.mean(c * c, axis=-1, keepdims=True) + RMS_EPS) * kvg_ref[...]
    ckv_ref[...] = ckv
    kpe = y[:, c_dim:c_dim + r_dim] * cos_ref[...] + y[:, c_dim + 128:c_dim + 128 + r_dim] * sin_ref[...]
    kpe_ref[...] = kpe
    cb = ckv.astype(BF)
    kn = jnp.dot(cb, wuk_ref[...], preferred_element_type=F32)
    vt_ref[...] = lax.dot_general(wuvt_ref[...], cb, _CONTRACT_LAST, preferred_element_type=F32).astype(BF)
    kpe_pad = jnp.concatenate([kpe, jnp.zeros((kpe.shape[0], 128 - r_dim), F32)], axis=-1).astype(BF)
    for h in range(heads):
        kfull_ref[:, h * 256:h * 256 + nope] = kn[:, h * nope:(h + 1) * nope].astype(BF)
        kfull_ref[:, h * 256 + nope:(h + 1) * 256] = kpe_pad


def _kv_side(xb, wkv, kv_g, cos_k, sin_k, wuk, wuv_t, *, bm, c_dim, r_dim, heads, nope):
    T, D = xb.shape
    assert nope == 128 and r_dim <= 128
    row = lambda w: pl.BlockSpec((bm, w), lambda i: (i, 0))
    full = lambda a: pl.BlockSpec(a.shape, lambda i: (0,) * a.ndim)
    hv = wuv_t.shape[0]
    return pl.pallas_call(
        functools.partial(_kv_kernel, c_dim=c_dim, r_dim=r_dim, heads=heads, nope=nope),
        out_shape=[jax.ShapeDtypeStruct((T, c_dim), F32), jax.ShapeDtypeStruct((T, r_dim), F32),
                   jax.ShapeDtypeStruct((T, heads * 256), BF), jax.ShapeDtypeStruct((hv, T), BF)],
        grid=(T // bm,),
        in_specs=[row(D), full(wkv), full(kv_g), row(r_dim), row(r_dim), full(wuk), full(wuv_t)],
        out_specs=[row(c_dim), row(r_dim), row(heads * 256), pl.BlockSpec((hv, bm), lambda i: (0, i))],
        compiler_params=_params(1),
        name="kv_side",
    )(xb, wkv, kv_g, cos_k, sin_k, wuk, wuv_t)


def _q_kernel(h_ref, wdqt_ref, qg_ref, wuqt_ref, wpet_ref, cost_ref, sint_ref, qt_ref,
              *, heads, nope, scale, half):
    ct = lax.dot_general(wdqt_ref[...], h_ref[...], _CONTRACT_LAST, preferred_element_type=F32)
    cqt = (ct * lax.rsqrt(jnp.mean(ct * ct, axis=0, keepdims=True) + RMS_EPS) * qg_ref[...]).astype(BF)
    qn = jnp.dot(wuqt_ref[...], cqt, preferred_element_type=F32) * scale
    qp = jnp.dot(wpet_ref[...], cqt, preferred_element_type=F32)
    cos = cost_ref[...]
    sin = sint_ref[...]
    for h in range(heads):
        qt_ref[h * 256:h * 256 + nope, :] = qn[h * nope:(h + 1) * nope].astype(BF)
        x = qp[h * 128:(h + 1) * 128]
        rot = (jnp.concatenate([x[128 - half:], x[:128 - half]], axis=0)
               - jnp.concatenate([x[half:], x[:half]], axis=0))
        pe = x * cos + rot * sin
        qt_ref[h * 256 + nope:(h + 1) * 256, :] = pe.astype(BF)


def _q_side(hb, wdq_t, q_g_col, wuq_t, wpe_t, cos_t, sin_t, *, bm, heads, nope, scale, half):
    T, D = hb.shape
    assert nope == 128
    full = lambda a: pl.BlockSpec(a.shape, lambda i: (0,) * a.ndim)
    col = lambda r: pl.BlockSpec((r, bm), lambda i: (0, i))
    return pl.pallas_call(
        functools.partial(_q_kernel, heads=heads, nope=nope, scale=scale, half=half),
        out_shape=jax.ShapeDtypeStruct((heads * 256, T), BF),
        grid=(T // bm,),
        in_specs=[pl.BlockSpec((bm, D), lambda i: (i, 0)), full(wdq_t), full(q_g_col), full(wuq_t),
                  full(wpe_t), col(128), col(128)],
        out_specs=col(heads * 256),
        compiler_params=_params(1),
        name="q_side",
    )(hb, wdq_t, q_g_col, wuq_t, wpe_t, cos_t, sin_t)


def _flash_kernel(qi_ref, ki_ref, k_ref, qt_ref, vt_ref, o_init_ref, o_ref, *scratch, hb, vdim):
    del o_init_ref
    m_refs, l_refs, acc_refs = scratch[:hb], scratch[hb:2 * hb], scratch[2 * hb:]
    step = pl.program_id(2)
    qi = qi_ref[step]
    ki = ki_ref[step]

    @pl.when(ki == 0)
    def _():
        for h in range(hb):
            m_refs[h][...] = jnp.full(m_refs[h].shape, NEG_INF, F32)
            l_refs[h][...] = jnp.zeros(l_refs[h].shape, F32)
            acc_refs[h][...] = jnp.zeros(acc_refs[h].shape, F32)

    def update(masked):
        for h in range(hb):
            qr = slice(h * 256, (h + 1) * 256)
            vr = slice(h * vdim, (h + 1) * vdim)
            st = jnp.dot(k_ref[:, qr], qt_ref[qr, :], preferred_element_type=F32)
            if masked:
                key = lax.broadcasted_iota(jnp.int32, st.shape, 0)
                qry = lax.broadcasted_iota(jnp.int32, st.shape, 1)
                st = jnp.where(key <= qry, st, NEG_INF)
            m_prev = m_refs[h][...]
            m_new = jnp.maximum(m_prev, jnp.max(st, axis=0, keepdims=True))
            corr = jnp.exp2(m_prev - m_new)
            pt = jnp.exp2(st - m_new)
            l_refs[h][...] = corr * l_refs[h][...] + jnp.sum(pt, axis=0, keepdims=True)
            acc_refs[h][...] = corr * acc_refs[h][...] + jnp.dot(vt_ref[vr, :], pt.astype(BF),
                                                                 preferred_element_type=F32)
            m_refs[h][...] = m_new

    @pl.when(ki < qi)
    def _():
        update(False)

    @pl.when(ki == qi)
    def _():
        update(True)
        for h in range(hb):
            vr = slice(h * vdim, (h + 1) * vdim)
            o_ref[:, vr] = (acc_refs[h][...] / l_refs[h][...]).T.astype(o_ref.dtype)


def _flash_attention(kfull, q_t, v_t, *, batch, seq, heads, vdim, blk, hb, total_rows):
    assert heads % hb == 0
    nb = seq // blk
    pairs = [(a, b) for a in range(nb) for b in range(a + 1)]
    qi_map = jnp.asarray([p[0] for p in pairs], jnp.int32)
    ki_map = jnp.asarray([p[1] for p in pairs], jnp.int32)
    grid_spec = pltpu.PrefetchScalarGridSpec(
        num_scalar_prefetch=2,
        grid=(batch, heads // hb, len(pairs)),
        in_specs=[
            pl.BlockSpec((blk, hb * 256), lambda b, h, s, qi, ki: (b * nb + ki[s], h)),
            pl.BlockSpec((hb * 256, blk), lambda b, h, s, qi, ki: (h, b * nb + qi[s])),
            pl.BlockSpec((hb * vdim, blk), lambda b, h, s, qi, ki: (h, b * nb + ki[s])),
            pl.BlockSpec(memory_space=pl.ANY),
        ],
        out_specs=pl.BlockSpec((blk, hb * vdim), lambda b, h, s, qi, ki: (b * nb + qi[s], h)),
        scratch_shapes=[pltpu.VMEM((1, blk), F32)] * (2 * hb) + [pltpu.VMEM((vdim, blk), F32)] * hb,
    )
    return pl.pallas_call(
        functools.partial(_flash_kernel, hb=hb, vdim=vdim),
        out_shape=jax.ShapeDtypeStruct((total_rows, heads * vdim), BF),
        grid_spec=grid_spec,
        input_output_aliases={5: 0},
        compiler_params=_params(3),
        name="flash_prompt",
    )(qi_map, ki_map, kfull, q_t, v_t, jnp.zeros((total_rows, heads * vdim), BF))


def _absorb_kernel(wuk_ref, qt_ref, o_ref):
    o_ref[...] = jnp.dot(wuk_ref[...], qt_ref[...], preferred_element_type=F32).astype(o_ref.dtype)


def _absorb_query(q_t, wuk, *, heads, nope, c_dim, col_block, cols):
    return pl.pallas_call(
        _absorb_kernel,
        out_shape=jax.ShapeDtypeStruct((heads * c_dim, cols), BF),
        grid=(heads,),
        in_specs=[pl.BlockSpec((c_dim, nope), lambda h: (0, h)),
                  pl.BlockSpec((nope, cols), lambda h: (2 * h, col_block))],
        out_specs=pl.BlockSpec((c_dim, cols), lambda h: (h, 0)),
        compiler_params=_params(1),
        name="absorb_query",
    )(wuk, q_t)


def _page_copies(pt_ref, ck_hbm, kp_hbm, ck_buf, kp_buf, sem, step, slot, i, pages):
    page = pt_ref[step * pages + i]
    return (pltpu.make_async_copy(ck_hbm.at[page], ck_buf.at[slot, i], sem.at[0, slot]),
            pltpu.make_async_copy(kp_hbm.at[page], kp_buf.at[slot, i], sem.at[1, slot]))


def _decode_kernel(pt_ref, qlat_ref, qlt_ref, qpe_ref, cnew_ref, knew_ref, ck_hbm, kp_hbm, o_ref,
                   ck_buf, kp_buf, sem, m_ref, l_ref, acc_ref, *, pages, steps_per_seq):
    n = pl.program_id(0)
    j = n % steps_per_seq
    slot = n % DECODE_SLOTS

    def start_step(step):
        for i in range(pages):
            for cp in _page_copies(pt_ref, ck_hbm, kp_hbm, ck_buf, kp_buf, sem, step, step % DECODE_SLOTS, i, pages):
                cp.start()

    @pl.when(n == 0)
    def _():
        for step in range(DECODE_SLOTS - 1):
            @pl.when(step < pl.num_programs(0))
            def _():
                start_step(step)

    @pl.when(n + DECODE_SLOTS - 1 < pl.num_programs(0))
    def _():
        start_step(n + DECODE_SLOTS - 1)

    @pl.when(j == 0)
    def _():
        m_ref[...] = jnp.full(m_ref.shape, NEG_INF, F32)
        l_ref[...] = jnp.zeros(l_ref.shape, F32)
        acc_ref[...] = jnp.zeros(acc_ref.shape, F32)

    for i in range(pages):
        for cp in _page_copies(pt_ref, ck_hbm, kp_hbm, ck_buf, kp_buf, sem, n, slot, i, pages):
            cp.wait()

    ql = qlat_ref[0]
    qlt = qlt_ref[0]
    qp = qpe_ref[0]
    heads = ql.shape[0]
    cks = []
    scores = []
    for i in range(pages):
        ck = ck_buf[slot, i].astype(BF)
        kp = kp_buf[slot, i].astype(BF)
        cks.append(ck)
        st = jnp.dot(ck, qlt, preferred_element_type=F32)
        st = jnp.where(lax.broadcasted_iota(jnp.int32, st.shape, 1) < heads, st, 0.0)
        scores.append(st.T[:heads] + jnp.dot(qp, kp, preferred_element_type=F32))
    s = jnp.concatenate(scores, axis=-1)
    m_prev = m_ref[...]
    m_new = jnp.maximum(m_prev, jnp.max(s, axis=-1, keepdims=True))
    corr = jnp.exp2(m_prev - m_new)
    p = jnp.exp2(s - m_new)
    l_new = corr * l_ref[...] + jnp.sum(p, axis=-1, keepdims=True)
    pb = p.astype(BF)
    rows = cks[0].shape[0]
    pv = jnp.dot(pb[:, :rows], cks[0], preferred_element_type=F32)
    for i in range(1, pages):
        pv = pv + jnp.dot(pb[:, i * rows:(i + 1) * rows], cks[i], preferred_element_type=F32)
    acc_new = corr * acc_ref[...] + pv
    m_ref[...] = m_new
    l_ref[...] = l_new
    acc_ref[...] = acc_new

    @pl.when(j == steps_per_seq - 1)
    def _():
        cn = cnew_ref[0].astype(BF).astype(F32)
        kn = knew_ref[0].astype(BF).astype(F32)
        s_self = (jnp.sum(ql.astype(F32) * cn, axis=-1, keepdims=True)
                  + jnp.sum(qp.astype(F32) * kn, axis=-1, keepdims=True))
        m_fin = jnp.maximum(m_new, s_self)
        c2 = jnp.exp2(m_new - m_fin)
        p_self = jnp.exp2(s_self - m_fin)
        l_fin = c2 * l_new + p_self
        acc_fin = c2 * acc_new + p_self.astype(BF).astype(F32) * cn
        o_ref[0] = (acc_fin / l_fin).astype(o_ref.dtype)


def _decode_attention(page_table, q_lat, q_lat_t, q_pe, ckv_new, kpe_new, cache_ckv, cache_kpe_t, *, pages):
    Bd, H, C = q_lat.shape
    R = q_pe.shape[2]
    n_pages = page_table.shape[1]
    page_rows = cache_ckv.shape[1]
    assert n_pages % pages == 0
    sps = n_pages // pages
    per_b = lambda *shape: pl.BlockSpec((1,) + shape, lambda n, pt: (n // sps, 0, 0))
    hbm = pl.BlockSpec(memory_space=pl.ANY)
    grid_spec = pltpu.PrefetchScalarGridSpec(
        num_scalar_prefetch=1,
        grid=(Bd * sps,),
        in_specs=[per_b(H, C), per_b(C, 128), per_b(H, R), per_b(1, C), per_b(1, R), hbm, hbm],
        out_specs=per_b(H, C),
        scratch_shapes=[pltpu.VMEM((DECODE_SLOTS, pages, page_rows, C), cache_ckv.dtype),
                        pltpu.VMEM((DECODE_SLOTS, pages, R, page_rows), cache_kpe_t.dtype),
                        pltpu.SemaphoreType.DMA((2, DECODE_SLOTS)),
                        pltpu.VMEM((H, 1), F32), pltpu.VMEM((H, 1), F32), pltpu.VMEM((H, C), F32)],
    )
    return pl.pallas_call(
        functools.partial(_decode_kernel, pages=pages, steps_per_seq=sps),
        out_shape=jax.ShapeDtypeStruct((Bd, H, C), BF),
        grid_spec=grid_spec,
        compiler_params=_params(1),
        name="decode_attention",
    )(page_table.reshape(-1), q_lat, q_lat_t, q_pe, ckv_new, kpe_new, cache_ckv, cache_kpe_t)


def _value_up_kernel(o_hbm_ref, olat_ref, wuv_ref, o_ref):
    del o_hbm_ref
    o_ref[...] = jnp.dot(olat_ref[...], wuv_ref[...], preferred_element_type=F32).astype(o_ref.dtype)


def _value_up_into(o_all, o_lat, wuv, *, heads, c_dim, vdim, row_block, rows):
    return pl.pallas_call(
        _value_up_kernel,
        out_shape=jax.ShapeDtypeStruct(o_all.shape, o_all.dtype),
        grid=(heads,),
        in_specs=[pl.BlockSpec(memory_space=pl.ANY),
                  pl.BlockSpec((rows, c_dim), lambda h: (0, h)),
                  pl.BlockSpec((c_dim, vdim), lambda h: (0, h))],
        out_specs=pl.BlockSpec((rows, vdim), lambda h: (row_block, h)),
        input_output_aliases={0: 0},
        compiler_params=_params(1),
        name="value_up",
    )(o_all, o_lat, wuv)


def _router_kernel(x_ref, mix_ref, g_ref, lng_ref, lnb_ref, sc_ref, sh_ref, wrh_ref, wrl_ref, br_ref,
                   xo_ref, w_ref, i_ref, h_ref, *, n_exp, alpha):
    xn = _layernorm_rows(alpha * x_ref[...] + g_ref[...] * mix_ref[...], lng_ref[...], lnb_ref[...])
    xo_ref[...] = xn
    h = xn * (1.0 + sc_ref[...]) + sh_ref[...]
    h_hi = h.astype(BF)
    h_lo = (h - h_hi.astype(F32)).astype(BF)
    logits = (jnp.dot(h_hi, wrh_ref[...], preferred_element_type=F32)
              + jnp.dot(h_hi, wrl_ref[...], preferred_element_type=F32)
              + jnp.dot(h_lo, wrh_ref[...], preferred_element_type=F32)) + br_ref[...]
    lane = lax.broadcasted_iota(jnp.int32, logits.shape, 1)
    valid = lane < n_exp
    logits = jnp.where(valid, logits, NEG_INF)
    mx = jnp.max(logits, axis=-1, keepdims=True)
    e = jnp.exp(logits - mx)
    probs = e / jnp.sum(e, axis=-1, keepdims=True)
    big = jnp.int32(1 << 30)
    p1 = jnp.max(probs, axis=-1, keepdims=True)
    i1 = jnp.min(jnp.where((probs == p1) & valid, lane, big), axis=-1, keepdims=True)
    rest = jnp.where((lane == i1) | ~valid, -1.0, probs)
    p2 = jnp.max(rest, axis=-1, keepdims=True)
    i2 = jnp.min(jnp.where(rest == p2, lane, big), axis=-1, keepdims=True)
    denom = p1 + p2
    w_ref[...] = jnp.where(lane == 0, p1 / denom, jnp.where(lane == 1, p2 / denom, 0.0))
    i_ref[...] = jnp.where(lane == 0, i1, jnp.where(lane == 1, i2, 0))
    h_ref[...] = h


def _postln_router(x, mix, mod, g_chunk, ln_g, ln_b, sc_chunk, sh_chunk, wr_pad, br_pad, tiles_per_batch, n_exp,
                   *, alpha):
    T, D = x.shape
    row = pl.BlockSpec((ROW_TILE, D), lambda i: (i, 0))
    vec = pl.BlockSpec((1, D), lambda i: (0, 0))
    sel = pl.BlockSpec((ROW_TILE, 128), lambda i: (i, 0))
    wr_hi = wr_pad.astype(BF)
    wr_lo = (wr_pad - wr_hi.astype(F32)).astype(BF)
    wspec = pl.BlockSpec((D, 128), lambda i: (0, 0))
    return pl.pallas_call(
        functools.partial(_router_kernel, n_exp=n_exp, alpha=alpha),
        out_shape=[jax.ShapeDtypeStruct((T, D), F32), jax.ShapeDtypeStruct((T, 128), F32),
                   jax.ShapeDtypeStruct((T, 128), jnp.int32), jax.ShapeDtypeStruct((T, D), F32)],
        grid=(T // ROW_TILE,),
        in_specs=[row, row, _mod_spec(tiles_per_batch, D, g_chunk), vec, vec,
                  _mod_spec(tiles_per_batch, D, sc_chunk), _mod_spec(tiles_per_batch, D, sh_chunk),
                  wspec, wspec, pl.BlockSpec((1, 128), lambda i: (0, 0))],
        out_specs=[row, sel, sel, row],
        compiler_params=_params(1),
        name="postln_router",
    )(x, mix, mod, ln_g.reshape(1, D), ln_b.reshape(1, D), mod, mod, wr_hi, wr_lo, br_pad)


def _route_tables(idx2, n_exp, bm, n_blocks):
    T = idx2.shape[0]
    e_flat = idx2.reshape(-1)
    onehot = (e_flat[:, None] == jnp.arange(n_exp, dtype=jnp.int32)[None, :]).astype(jnp.int32)
    rank = jnp.sum((jnp.cumsum(onehot, axis=0) - onehot) * onehot, axis=1)
    counts = jnp.sum(onehot, axis=0)
    blocks_per = (counts + bm - 1) // bm
    blocks_end = jnp.cumsum(blocks_per)
    dest = (blocks_end - blocks_per)[e_flat] * bm + rank
    n_used = blocks_end[-1]
    b = jnp.minimum(jnp.arange(n_blocks, dtype=jnp.int32), n_used - 1)
    block_expert = jnp.sum((b[:, None] >= blocks_end[None, :]).astype(jnp.int32), axis=1)
    row_token = jnp.full((n_blocks * bm,), -1, jnp.int32).at[dest].set(jnp.arange(2 * T, dtype=jnp.int32) // 2)
    ids = jnp.arange(n_exp, dtype=jnp.int32)
    later_used = (ids[None, :] > ids[:, None]) & (blocks_per[None, :] > 0)
    next_of_expert = jnp.min(jnp.where(later_used, ids[None, :], n_exp), axis=1)
    next_expert = jnp.where(next_of_expert < n_exp, next_of_expert, -1)[block_expert]
    return (row_token, dest.astype(jnp.int32), block_expert.astype(jnp.int32), next_expert.astype(jnp.int32),
            n_used.reshape(1).astype(jnp.int32))


def _row_copy(src_hbm, row, dst, sem):
    return pltpu.make_async_copy(src_hbm.at[pl.ds(row, 1)], dst, sem)


ROW_DMA_UNROLL = 8


def _dispatch_kernel(tok_ref, h_hbm, o_ref, buf, sem, *, rows):
    i = pl.program_id(0)
    slot = i % 2

    def issue(step, s):
        def body(r8, c):
            for k in range(ROW_DMA_UNROLL):
                r = r8 * ROW_DMA_UNROLL + k
                tok = tok_ref[step * rows + r]

                @pl.when(tok >= 0)
                def _():
                    _row_copy(h_hbm, tok, buf.at[s, pl.ds(r, 1)], sem.at[s]).start(priority=k % 2)
            return c
        lax.fori_loop(0, rows // ROW_DMA_UNROLL, body, 0)

    @pl.when(i == 0)
    def _():
        buf[...] = jnp.zeros(buf.shape, buf.dtype)
        issue(0, 0)

    @pl.when(i + 1 < pl.num_programs(0))
    def _():
        issue(i + 1, 1 - slot)

    def wait_body(r, c):
        @pl.when(tok_ref[i * rows + r] >= 0)
        def _():
            _row_copy(h_hbm, 0, buf.at[slot, pl.ds(r, 1)], sem.at[slot]).wait()
        return c
    lax.fori_loop(0, rows, wait_body, 0, unroll=8)
    o_ref[...] = buf[slot].astype(o_ref.dtype)


def _dispatch(h, row_token, *, rows=256):
    T, D = h.shape
    Rn = row_token.shape[0]
    if Rn % rows:
        rows = ROW_TILE
    assert Rn % rows == 0 and rows % ROW_DMA_UNROLL == 0
    grid_spec = pltpu.PrefetchScalarGridSpec(
        num_scalar_prefetch=1,
        grid=(Rn // rows,),
        in_specs=[pl.BlockSpec(memory_space=pl.ANY)],
        out_specs=pl.BlockSpec((rows, D), lambda i, tok: (i, 0)),
        scratch_shapes=[pltpu.VMEM((2, rows, D), h.dtype), pltpu.SemaphoreType.DMA((2,))],
    )
    return pl.pallas_call(
        functools.partial(_dispatch_kernel, rows=rows),
        out_shape=jax.ShapeDtypeStruct((Rn, D), h.dtype),
        grid_spec=grid_spec,
        compiler_params=_params(1),
        name="moe_dispatch",
    )(row_token, h)


def _combine_postln_kernel(dest_ref, x_ref, g_ref, lng_ref, lnb_ref, w_ref, y_hbm, yp_ref, ys_ref, buf, sem,
                           *, alpha, rows, n_prompt_tiles):
    i = pl.program_id(0)
    slot = i % 2

    def issue(step, s):
        def body(r, c):
            a = (step * rows + r) * 2
            _row_copy(y_hbm, dest_ref[a], buf.at[s, 0, pl.ds(r, 1)], sem.at[s]).start(priority=0)
            _row_copy(y_hbm, dest_ref[a + 1], buf.at[s, 1, pl.ds(r, 1)], sem.at[s]).start(priority=1)
            return c
        lax.fori_loop(0, rows, body, 0, unroll=8)

    @pl.when(i == 0)
    def _():
        issue(0, 0)

    @pl.when(i + 1 < pl.num_programs(0))
    def _():
        issue(i + 1, 1 - slot)

    def wait_body(r, c):
        _row_copy(y_hbm, 0, buf.at[slot, 0, pl.ds(r, 1)], sem.at[slot]).wait()
        _row_copy(y_hbm, 0, buf.at[slot, 1, pl.ds(r, 1)], sem.at[slot]).wait()
        return c
    lax.fori_loop(0, rows, wait_body, 0, unroll=8)
    f = w_ref[:, 0:1] * buf[slot, 0] + w_ref[:, 1:2] * buf[slot, 1]
    xn = _layernorm_rows(alpha * x_ref[...] + g_ref[...] * f, lng_ref[...], lnb_ref[...])

    @pl.when(i < n_prompt_tiles)
    def _():
        yp_ref[...] = xn

    @pl.when(i >= n_prompt_tiles)
    def _():
        ys_ref[...] = xn


def _combine_postln(x, y, dest, wts, mod_g, g_chunk, ln_g, ln_b, tiles_per_batch, n_prompt_tiles, *, alpha):
    T, D = x.shape
    rows = ROW_TILE
    assert T == (n_prompt_tiles + 1) * rows
    row = pl.BlockSpec((rows, D), lambda i, d: (i, 0))
    vec = pl.BlockSpec((1, D), lambda i, d: (0, 0))
    grid_spec = pltpu.PrefetchScalarGridSpec(
        num_scalar_prefetch=1,
        grid=(T // rows,),
        in_specs=[row, pl.BlockSpec((rows, D), lambda i, d: (i // tiles_per_batch, g_chunk)), vec, vec,
                  pl.BlockSpec((rows, 128), lambda i, d: (i, 0)), pl.BlockSpec(memory_space=pl.ANY)],
        out_specs=[pl.BlockSpec((rows, D), lambda i, d: (jnp.minimum(i, n_prompt_tiles - 1), 0)),
                   pl.BlockSpec((rows, D), lambda i, d: (0, 0))],
        scratch_shapes=[pltpu.VMEM((2, 2, rows, D), F32), pltpu.SemaphoreType.DMA((2,))],
    )
    return pl.pallas_call(
        functools.partial(_combine_postln_kernel, alpha=alpha, rows=rows, n_prompt_tiles=n_prompt_tiles),
        out_shape=[jax.ShapeDtypeStruct((n_prompt_tiles * rows, D), F32), jax.ShapeDtypeStruct((rows, D), F32)],
        grid_spec=grid_spec,
        compiler_params=_params(1),
        name="moe_combine_postln",
    )(dest, x, mod_g, ln_g.reshape(1, D), ln_b.reshape(1, D), wts, y)


def _pick_bm(T, target):
    best = ROW_TILE
    for m in range(ROW_TILE, target + 1, ROW_TILE):
        if T % m == 0:
            best = m
    return best


def kernel(x_prompt, x_sample, cache_ckv, cache_kpe, page_table, c_prompt, c_sample, w_ada, b_ada, ln_g, ln_b, a_w_in, a_v_g, a_v_b, a_w_s, a_b_s, a_w_out, b_w_dq, b_q_g, b_w_uq, b_w_qr, b_w_o, kv_w_dkv, kv_g, kv_w_kr, kv_w_uk, kv_w_uv, ffn_w_gate, ffn_w_up, ffn_w_down, moe_w_router, moe_b_router, moe_w_gate, moe_w_up, moe_w_down):
    B, S, D = x_prompt.shape
    Bd, Td, _ = x_sample.shape
    depth = w_ada.shape[0]
    n_a = a_w_in.shape[0]
    assert depth == 2 and n_a == 1 and Td == 1 and Bd == ROW_TILE and S % ROW_TILE == 0
    A = a_w_in.shape[2] // 2
    G = a_w_s.shape[1]
    H, nope = b_w_uq.shape[2], b_w_uq.shape[3]
    R = b_w_qr.shape[3]
    C = kv_w_dkv.shape[1]
    vdim = kv_w_uv.shape[2]
    E = moe_w_router.shape[2]
    n_pages, page_rows = page_table.shape[1], cache_ckv.shape[1]
    past_len = n_pages * page_rows
    alpha = (2.0 * depth) ** 0.25
    scale = (nope + R) ** -0.5
    Tp = B * S
    T = Tp + Bd
    tiles_per_batch = S // ROW_TILE
    n_prompt_tiles = Tp // ROW_TILE
    bm = _pick_bm(T, 640)
    bm_mid = _pick_bm(T, 832)
    bm_big = _pick_bm(T, 1664)

    x = (x_prompt.reshape(Tp, D), x_sample.reshape(Bd, D))

    c_tiles = jnp.concatenate([jnp.repeat(c_prompt, ROW_TILE, axis=0), c_sample], axis=0)
    mods = [_wsmm(c_tiles, [w_ada], bm=c_tiles.shape[0], bn=512, out_dtype=F32, act="silu_in",
                  bias=b_ada[l].reshape(1, -1), w_index=l, name="adaln") for l in range(depth)]
    SH_M, SC_M, G_M, SH_F, SC_F, G_F = range(6)

    half = R // 2
    inv = ROPE_BASE ** (-jnp.arange(half, dtype=F32) / half)
    pos = jnp.concatenate([jnp.tile(jnp.arange(S, dtype=jnp.int32), B),
                           jnp.full((Bd,), past_len, jnp.int32)]).astype(F32)
    ang = pos[:, None] * inv[None, :]
    cos2 = jnp.concatenate([jnp.cos(ang), jnp.cos(ang)], axis=-1)
    sin2 = jnp.concatenate([jnp.sin(ang), jnp.sin(ang)], axis=-1)
    qscale = scale * math.log2(math.e)
    row_pad = jnp.zeros((128 - R, T), F32)
    cos_qt = jnp.concatenate([cos2.T * qscale, row_pad], axis=0)
    sin_qt = jnp.concatenate([sin2.T * qscale, row_pad], axis=0)

    h = _modulate(x, mods[0], SC_M, SH_M, tiles_per_batch)
    u = _wsmm(h, [a_w_in], bm=bm_big, bn=512, out_dtype=BF, n_off=0, n_out=A, act="gelu", name="a_in_u")
    vpre = _wsmm(h, [a_w_in], bm=bm_big, bn=512, out_dtype=F32, n_off=A, n_out=A, act="gelu", name="a_in_v")
    tril = jnp.tril(jnp.ones((ROW_TILE, ROW_TILE), bool))
    ws_prompt = jnp.where(tril, a_w_s[0], 0.0)
    ws_sample = a_w_s[0][:, :1, :1] * jnp.eye(ROW_TILE, dtype=F32)
    ws_tiles = jnp.stack([ws_prompt, ws_sample]).astype(BF)
    gd = A // G
    b_prompt = jnp.repeat(a_b_s[0].T, gd, axis=1)
    b_sample = jnp.broadcast_to(b_prompt[:1], b_prompt.shape)
    b_tiles = jnp.stack([b_prompt, b_sample])
    gated, v_sample = _spatial_gate(vpre, u, ws_tiles, b_tiles, a_v_g[0], a_v_b[0], n_prompt_tiles)
    mix = _wsmm(gated, [a_w_out], bm=bm_mid, bn=512, out_dtype=F32, name="a_out")
    x, h = _postln(x, mix, mods[0], G_M, ln_g[0, 0], ln_b[0, 0], tiles_per_batch, alpha=alpha,
                   mod_next=mods[0], sc_chunk=SC_F, sh_chunk=SH_F)
    act = _wsmm(h, [ffn_w_gate, ffn_w_up], bm=bm_big, bn=512, out_dtype=BF, name="ffn_up")
    f = _wsmm(act, [ffn_w_down], bm=bm, bn=512, out_dtype=F32, name="ffn_down")
    x, h, xb = _postln(x, f, mods[0], G_F, ln_g[0, 1], ln_b[0, 1], tiles_per_batch, alpha=alpha,
                       mod_next=mods[1], sc_chunk=SC_M, sh_chunk=SH_M, emit_bf=True)

    w1, w2 = kv_w_kr[:, :half], kv_w_kr[:, half:]
    zpad = jnp.zeros((D, 128 - R), F32)
    wkv = jnp.concatenate([kv_w_dkv, kv_w_kr, zpad, -w2, w1, zpad], axis=1).astype(BF)
    wuk2 = kv_w_uk.reshape(C, H * nope)
    wuv2 = kv_w_uv.reshape(C, H * vdim)
    ckv, kpe, kfull, v_t = _kv_side(xb, wkv, kv_g.reshape(1, C), cos2, sin2, wuk2.astype(BF), wuv2.T.astype(BF),
                                    bm=bm, c_dim=C, r_dim=R, heads=H, nope=nope)

    wqr = b_w_qr[0]
    qpad = jnp.zeros(wqr.shape[:2] + (128 - R,), F32)
    wpe = jnp.concatenate([wqr, qpad], axis=-1).reshape(wqr.shape[0], H * 128)
    q_t = _q_side(h, b_w_dq[0].T.astype(BF), b_q_g[0].reshape(-1, 1), b_w_uq[0].reshape(-1, H * nope).T.astype(BF),
                  wpe.T.astype(BF), cos_qt, sin_qt, bm=bm, heads=H, nope=nope, scale=qscale, half=half)
    o_all = _flash_attention(kfull, q_t, v_t, batch=B, seq=S, heads=H, vdim=vdim, blk=min(1024, S), hb=min(4, H),
                             total_rows=T)

    q_lat_hc = _absorb_query(q_t, wuk2.astype(BF), heads=H, nope=nope, c_dim=C,
                             col_block=Tp // Bd, cols=Bd).reshape(H, C, Bd)
    q_lat = q_lat_hc.transpose(2, 0, 1)
    q_lat_t = jnp.pad(q_lat_hc.transpose(2, 1, 0), ((0, 0), (0, 0), (0, 128 - H)))
    q_pe_s = q_t[:, Tp:].reshape(H, 256, Bd)[:, nope:nope + R, :].transpose(2, 0, 1)
    o_lat = _decode_attention(page_table, q_lat, q_lat_t, q_pe_s, ckv[Tp:].reshape(Bd, 1, C),
                              kpe[Tp:].reshape(Bd, 1, R), cache_ckv, jnp.swapaxes(cache_kpe, 1, 2),
                              pages=min(32, n_pages))
    o_all = _value_up_into(o_all, o_lat.reshape(Bd, H * C), wuv2.astype(BF), heads=H, c_dim=C, vdim=vdim,
                           row_block=Tp // Bd, rows=Bd)
    mix = _wsmm(o_all, [b_w_o], bm=bm_big, bn=512, out_dtype=F32, name="attn_out")
    wr_pad = jnp.concatenate([moe_w_router[0], jnp.zeros((D, 128 - E), F32)], axis=1)
    br_pad = jnp.concatenate([moe_b_router[0], jnp.zeros((128 - E,), F32)]).reshape(1, 128)
    x, wts, idx, hf = _postln_router(x, mix, mods[1], G_M, ln_g[1, 0], ln_b[1, 0], SC_F, SH_F, wr_pad, br_pad,
                                     tiles_per_batch, E, alpha=alpha)
    bme = 384
    n_blocks = pl.cdiv(2 * T + E * (bme - 1), bme)
    row_token, dest, block_expert, next_expert, n_used = _route_tables(idx[:, :2], E, bme, n_blocks)
    grouped = dict(block_w=block_expert, next_w=next_expert, n_used=n_used)
    hs = _dispatch(hf, row_token)
    per_expert = lambda w: w.reshape((-1,) + w.shape[-2:])
    act = _wsmm(hs, [per_expert(moe_w_gate), per_expert(moe_w_up)], bm=bme, bn=1024, out_dtype=BF,
                name="moe_up", **grouped)
    ys = _wsmm(act, [per_expert(moe_w_down)], bm=bme, bn=512, out_dtype=F32, name="moe_down", **grouped)
    y_prompt, y_sample = _combine_postln(x, ys, dest, wts, mods[1], G_F, ln_g[1, 1], ln_b[1, 1], tiles_per_batch,
                                         n_prompt_tiles, alpha=alpha)
    return (y_prompt.reshape(B, S, D), y_sample.reshape(Bd, Td, D),
            ckv[:Tp].reshape(B, S, C), kpe[:Tp].reshape(B, S, R),
            ckv[Tp:].reshape(Bd, Td, C), kpe[Tp:].reshape(Bd, Td, R),
            v_sample.reshape(n_a, Bd, Td, A))
```

```python
import functools
import math

import jax
import jax.numpy as jnp
from jax import lax
from jax.experimental import pallas as pl
from jax.experimental.pallas import tpu as pltpu

BF = jnp.bfloat16
F32 = jnp.float32

ROW_TILE = 128
LN_EPS = 1e-5
RMS_EPS = 1e-6
ROPE_BASE = 10000.0
NEG_INF = float("-inf")
VMEM_LIMIT = 56 * 1024 * 1024
DECODE_SLOTS = 3

def _params(n_axes, vmem=VMEM_LIMIT):
    return pltpu.CompilerParams(dimension_semantics=("arbitrary",) * n_axes, vmem_limit_bytes=vmem)


def _gelu_tanh(x):
    return 0.5 * x * (1.0 + jnp.tanh(math.sqrt(2.0 / math.pi) * (x + 0.044715 * x * x * x)))


def _silu(x):
    return x * (1.0 / (1.0 + jnp.exp(-x)))


def _layernorm_rows(x, g, b):
    mu = jnp.mean(x, axis=-1, keepdims=True)
    xc = x - mu
    var = jnp.mean(xc * xc, axis=-1, keepdims=True)
    return xc * lax.rsqrt(var + LN_EPS) * g + b


def _group_starts(bw_ref, i):
    return (i == 0) | (bw_ref[i] != bw_ref[jnp.maximum(i - 1, 0)])


def _wsmm_kernel(bw_ref, nxt_ref, nu_ref, x_ref, *refs, n_w, act, has_bias, bn, n_off, row_chunks):
    w_hbm = refs[:n_w]
    k = n_w
    bias_ref = None
    if has_bias:
        bias_ref = refs[k]
        k += 1
    o_ref = refs[k]
    stage = refs[k + 1:k + 1 + n_w]
    wbf = refs[k + 1 + n_w:k + 1 + 2 * n_w]
    sem, cnt = refs[k + 1 + 2 * n_w], refs[k + 2 + 2 * n_w]
    j = pl.program_id(0)
    i = pl.program_id(1)

    def weight_copies(w_idx, jj, slot):
        col = pl.multiple_of(jj * bn + n_off, 128)
        return [pltpu.make_async_copy(w_hbm[t].at[w_idx, :, pl.ds(col, bn)], stage[t].at[slot], sem.at[t, slot])
                for t in range(n_w)]

    @pl.when((j == 0) & (i == 0))
    def _():
        cnt[0] = 0
        for cp in weight_copies(bw_ref[0], 0, 0):
            cp.start()

    @pl.when(_group_starts(bw_ref, i))
    def _():
        slot = cnt[0] % 2
        for cp in weight_copies(bw_ref[i], j, slot):
            cp.wait()
        more = nxt_ref[i] >= 0
        nw = jnp.where(more, nxt_ref[i], bw_ref[0])
        nj = jnp.where(more, j, j + 1)

        @pl.when(nj < pl.num_programs(0))
        def _():
            for cp in weight_copies(nw, nj, 1 - slot):
                cp.start()

        for t in range(n_w):
            wbf[t][...] = stage[t][slot].astype(BF)
        cnt[0] = cnt[0] + 1

    @pl.when(i < nu_ref[0])
    def _():
        rows = x_ref.shape[0] // row_chunks
        for c in range(row_chunks):
            rs = slice(c * rows, (c + 1) * rows)
            x = x_ref[rs, :]
            if act == "silu_in":
                x = _silu(x.astype(F32))
            x = x.astype(BF)
            acc = jnp.dot(x, wbf[0][...], preferred_element_type=F32)
            if n_w == 2:
                acc = _silu(acc) * jnp.dot(x, wbf[1][...], preferred_element_type=F32)
            if has_bias:
                acc = acc + bias_ref[...]
            if act == "gelu":
                acc = _gelu_tanh(acc)
            o_ref[rs, :] = acc.astype(o_ref.dtype)

    @pl.when(i >= nu_ref[0])
    def _():
        o_ref[...] = jnp.zeros(o_ref.shape, o_ref.dtype)


def _wsmm(x, ws, *, bm, bn, out_dtype, block_w=None, next_w=None, n_used=None, w_index=0, n_off=0,
          n_out=None, act=None, bias=None, name="wsmm"):
    M, K = x.shape
    N = ws[0].shape[-1] if n_out is None else n_out
    bn = min(bn, N)
    while N % bn:
        bn //= 2
    assert M % bm == 0 and bn % 128 == 0 and n_off % bn == 0 and all(w.ndim == 3 and w.shape[1] == K for w in ws)
    n_i = M // bm
    if block_w is None:
        block_w = jnp.full((n_i,), w_index, jnp.int32)
        next_w = jnp.full((n_i,), -1, jnp.int32)
        n_used = jnp.full((1,), n_i, jnp.int32)
    n_w = len(ws)
    offb = n_off // bn
    row_chunks = max(c for c in (1, 2, 4) if bm % (16 * c) == 0 and (c == 1 or bm // c >= 384))
    in_specs = [pl.BlockSpec((bm, K), lambda j, i, *_: (i, 0))] + [pl.BlockSpec(memory_space=pl.ANY)] * n_w
    args = [x, *ws]
    if bias is not None:
        in_specs.append(pl.BlockSpec((1, bn), lambda j, i, *_: (0, j + offb)))
        args.append(bias)
    grid_spec = pltpu.PrefetchScalarGridSpec(
        num_scalar_prefetch=3,
        grid=(N // bn, n_i),
        in_specs=in_specs,
        out_specs=pl.BlockSpec((bm, bn), lambda j, i, *_: (i, j)),
        scratch_shapes=[pltpu.VMEM((2, K, bn), ws[0].dtype)] * n_w + [pltpu.VMEM((K, bn), BF)] * n_w
        + [pltpu.SemaphoreType.DMA((n_w, 2)), pltpu.SMEM((1,), jnp.int32)],
    )
    return pl.pallas_call(
        functools.partial(_wsmm_kernel, n_w=n_w, act=act, has_bias=bias is not None, bn=bn, n_off=n_off,
                          row_chunks=row_chunks),
        out_shape=jax.ShapeDtypeStruct((M, N), out_dtype),
        grid_spec=grid_spec,
        compiler_params=_params(2),
        name=name,
    )(block_w, next_w, n_used, *args)


def _mod_spec(tiles_per_batch, D, chunk):
    return pl.BlockSpec((ROW_TILE, D), lambda i: (i // tiles_per_batch, chunk))


def _token_rows(x):
    if not isinstance(x, tuple):
        T, D = x.shape
        return [x], [pl.BlockSpec((ROW_TILE, D), lambda i, *_: (i, 0))], T, D, None
    xp, xs = x
    npt = xp.shape[0] // ROW_TILE
    assert xs.shape[0] == ROW_TILE
    D = xp.shape[1]
    specs = [pl.BlockSpec((ROW_TILE, D), lambda i, *_: (jnp.minimum(i, npt - 1), 0)),
             pl.BlockSpec((ROW_TILE, D), lambda i, *_: (0, 0))]
    return [xp, xs], specs, xp.shape[0] + xs.shape[0], D, npt


def _read_token_tile(x_refs, n_prompt_tiles):
    if len(x_refs) == 1:
        return x_refs[0][...]
    return jnp.where(pl.program_id(0) < n_prompt_tiles, x_refs[0][...], x_refs[1][...])


def _modulate_kernel(*refs, n_x, n_prompt_tiles):
    sc_ref, sh_ref, h_ref = refs[n_x:]
    x = _read_token_tile(refs[:n_x], n_prompt_tiles)
    h_ref[...] = (x * (1.0 + sc_ref[...]) + sh_ref[...]).astype(h_ref.dtype)


def _modulate(x, mod, sc_chunk, sh_chunk, tiles_per_batch):
    xs, x_specs, T, D, npt = _token_rows(x)
    return pl.pallas_call(
        functools.partial(_modulate_kernel, n_x=len(xs), n_prompt_tiles=npt),
        out_shape=jax.ShapeDtypeStruct((T, D), BF),
        grid=(T // ROW_TILE,),
        in_specs=x_specs + [_mod_spec(tiles_per_batch, D, sc_chunk), _mod_spec(tiles_per_batch, D, sh_chunk)],
        out_specs=pl.BlockSpec((ROW_TILE, D), lambda i: (i, 0)),
        compiler_params=_params(1),
        name="modulate",
    )(*xs, mod, mod)


def _postln_kernel(*refs, n_x, n_prompt_tiles, alpha, modulate_next, emit_bf):
    x = _read_token_tile(refs[:n_x], n_prompt_tiles)
    mix_ref, g_ref, lng_ref, lnb_ref = refs[n_x:n_x + 4]
    k = n_x + 4
    if modulate_next:
        sc_ref, sh_ref = refs[k], refs[k + 1]
        k += 2
    xo_ref = refs[k]; k += 1
    xn = _layernorm_rows(alpha * x + g_ref[...] * mix_ref[...], lng_ref[...], lnb_ref[...])
    xo_ref[...] = xn
    if modulate_next:
        refs[k][...] = (xn * (1.0 + sc_ref[...]) + sh_ref[...]).astype(BF)
        k += 1
    if emit_bf:
        refs[k][...] = xn.astype(BF)


def _postln(x, mix, mod_g, g_chunk, ln_g, ln_b, tiles_per_batch, *, alpha, mod_next=None,
            sc_chunk=None, sh_chunk=None, emit_bf=False):
    xs, x_specs, T, D, npt = _token_rows(x)
    row = pl.BlockSpec((ROW_TILE, D), lambda i: (i, 0))
    vec = pl.BlockSpec((1, D), lambda i: (0, 0))
    in_specs = x_specs + [row, _mod_spec(tiles_per_batch, D, g_chunk), vec, vec]
    args = xs + [mix, mod_g, ln_g.reshape(1, D), ln_b.reshape(1, D)]
    out_shape = [jax.ShapeDtypeStruct((T, D), F32)]
    out_specs = [row]
    if mod_next is not None:
        in_specs += [_mod_spec(tiles_per_batch, D, sc_chunk), _mod_spec(tiles_per_batch, D, sh_chunk)]
        args += [mod_next, mod_next]
        out_shape.append(jax.ShapeDtypeStruct((T, D), BF))
        out_specs.append(row)
    if emit_bf:
        out_shape.append(jax.ShapeDtypeStruct((T, D), BF))
        out_specs.append(row)
    return pl.pallas_call(
        functools.partial(_postln_kernel, n_x=len(xs), n_prompt_tiles=npt, alpha=alpha,
                          modulate_next=mod_next is not None, emit_bf=emit_bf),
        out_shape=out_shape,
        grid=(T // ROW_TILE,),
        in_specs=in_specs,
        out_specs=out_specs,
        compiler_params=_params(1),
        name="postln",
    )(*args)


def _spatial_gate_kernel(vpre_ref, u_ref, ws_ref, b_ref, vg_ref, vb_ref, mix_ref, vout_ref, *, groups):
    v = _layernorm_rows(vpre_ref[...], vg_ref[...], vb_ref[...])

    @pl.when(pl.program_id(0) == pl.num_programs(0) - 1)
    def _():
        vout_ref[...] = v

    vb16 = v.astype(BF)
    gd = v.shape[1] // groups
    for g in range(groups):
        cols = slice(g * gd, (g + 1) * gd)
        s = jnp.dot(ws_ref[g], vb16[:, cols], preferred_element_type=F32) + b_ref[:, cols]
        mix_ref[:, cols] = (u_ref[:, cols].astype(F32) * s).astype(BF)


def _spatial_gate(vpre, u, ws_tiles, b_tiles, v_g, v_b, n_prompt_tiles):
    T, A = vpre.shape
    G = ws_tiles.shape[1]
    row = pl.BlockSpec((ROW_TILE, A), lambda i: (i, 0))
    vec = pl.BlockSpec((1, A), lambda i: (0, 0))
    return pl.pallas_call(
        functools.partial(_spatial_gate_kernel, groups=G),
        out_shape=[jax.ShapeDtypeStruct((T, A), BF), jax.ShapeDtypeStruct((ROW_TILE, A), F32)],
        grid=(T // ROW_TILE,),
        in_specs=[row, row,
                  pl.BlockSpec((None, G, ROW_TILE, ROW_TILE), lambda i: (i // n_prompt_tiles, 0, 0, 0)),
                  pl.BlockSpec((None, ROW_TILE, A), lambda i: (i // n_prompt_tiles, 0, 0)),
                  vec, vec],
        out_specs=[row, pl.BlockSpec((ROW_TILE, A), lambda i: (0, 0))],
        compiler_params=_params(1),
        name="spatial_gate",
    )(vpre, u, ws_tiles, b_tiles, v_g.reshape(1, A), v_b.reshape(1, A))


_CONTRACT_LAST = (((1,), (1,)), ((), ()))


def _kv_kernel(x_ref, wkv_ref, kvg_ref, cos_ref, sin_ref, wuk_ref, wuvt_ref,
               ckv_ref, kpe_ref, kfull_ref, vt_ref, *, c_dim, r_dim, heads, nope):
    y = jnp.dot(x_ref[...], wkv_ref[...], preferred_element_type=F32)
    c = y[:, :c_dim]
    ckv = c * lax.rsqrt(jnp.mean(c * c, axis=-1, keepdims=True) + RMS_EPS) * kvg_ref[...]
    ckv_ref[...] = ckv
    kpe = y[:, c_dim:c_dim + r_dim] * cos_ref[...] + y[:, c_dim + 128:c_dim + 128 + r_dim] * sin_ref[...]
    kpe_ref[...] = kpe
    cb = ckv.astype(BF)
    kn = jnp.dot(cb, wuk_ref[...], preferred_element_type=F32)
    vt_ref[...] = lax.dot_general(wuvt_ref[...], cb, _CONTRACT_LAST, preferred_element_type=F32).astype(BF)
    kpe_pad = jnp.concatenate([kpe, jnp.zeros((kpe.shape[0], 128 - r_dim), F32)], axis=-1).astype(BF)
    for h in range(heads):
        kfull_ref[:, h * 256:h * 256 + nope] = kn[:, h * nope:(h + 1) * nope].astype(BF)
        kfull_ref[:, h * 256 + nope:(h + 1) * 256] = kpe_pad


def _kv_side(xb, wkv, kv_g, cos_k, sin_k, wuk, wuv_t, *, bm, c_dim, r_dim, heads, nope):
    T, D = xb.shape
    assert nope == 128 and r_dim <= 128
    row = lambda w: pl.BlockSpec((bm, w), lambda i: (i, 0))
    full = lambda a: pl.BlockSpec(a.shape, lambda i: (0,) * a.ndim)
    hv = wuv_t.shape[0]
    return pl.pallas_call(
        functools.partial(_kv_kernel, c_dim=c_dim, r_dim=r_dim, heads=heads, nope=nope),
        out_shape=[jax.ShapeDtypeStruct((T, c_dim), F32), jax.ShapeDtypeStruct((T, r_dim), F32),
                   jax.ShapeDtypeStruct((T, heads * 256), BF), jax.ShapeDtypeStruct((hv, T), BF)],
        grid=(T // bm,),
        in_specs=[row(D), full(wkv), full(kv_g), row(r_dim), row(r_dim), full(wuk), full(wuv_t)],
        out_specs=[row(c_dim), row(r_dim), row(heads * 256), pl.BlockSpec((hv, bm), lambda i: (0, i))],
        compiler_params=_params(1),
        name="kv_side",
    )(xb, wkv, kv_g, cos_k, sin_k, wuk, wuv_t)


def _q_kernel(h_ref, wdqt_ref, qg_ref, wuqt_ref, wpet_ref, cost_ref, sint_ref, qt_ref,
              *, heads, nope, scale, half):
    ct = lax.dot_general(wdqt_ref[...], h_ref[...], _CONTRACT_LAST, preferred_element_type=F32)
    cqt = (ct * lax.rsqrt(jnp.mean(ct * ct, axis=0, keepdims=True) + RMS_EPS) * qg_ref[...]).astype(BF)
    qn = jnp.dot(wuqt_ref[...], cqt, preferred_element_type=F32) * scale
    qp = jnp.dot(wpet_ref[...], cqt, preferred_element_type=F32)
    cos = cost_ref[...]
    sin = sint_ref[...]
    for h in range(heads):
        qt_ref[h * 256:h * 256 + nope, :] = qn[h * nope:(h + 1) * nope].astype(BF)
        x = qp[h * 128:(h + 1) * 128]
        rot = (jnp.concatenate([x[128 - half:], x[:128 - half]], axis=0)
               - jnp.concatenate([x[half:], x[:half]], axis=0))
        pe = x * cos + rot * sin
        qt_ref[h * 256 + nope:(h + 1) * 256, :] = pe.astype(BF)


def _q_side(hb, wdq_t, q_g_col, wuq_t, wpe_t, cos_t, sin_t, *, bm, heads, nope, scale, half):
    T, D = hb.shape
    assert nope == 128
    full = lambda a: pl.BlockSpec(a.shape, lambda i: (0,) * a.ndim)
    col = lambda r: pl.BlockSpec((r, bm), lambda i: (0, i))
    return pl.pallas_call(
        functools.partial(_q_kernel, heads=heads, nope=nope, scale=scale, half=half),
        out_shape=jax.ShapeDtypeStruct((heads * 256, T), BF),
        grid=(T // bm,),
        in_specs=[pl.BlockSpec((bm, D), lambda i: (i, 0)), full(wdq_t), full(q_g_col), full(wuq_t),
                  full(wpe_t), col(128), col(128)],
        out_specs=col(heads * 256),
        compiler_params=_params(1),
        name="q_side",
    )(hb, wdq_t, q_g_col, wuq_t, wpe_t, cos_t, sin_t)


def _flash_kernel(qi_ref, ki_ref, k_ref, qt_ref, vt_ref, o_init_ref, o_ref, *scratch, hb, vdim):
    del o_init_ref
    m_refs, l_refs, acc_refs = scratch[:hb], scratch[hb:2 * hb], scratch[2 * hb:]
    step = pl.program_id(2)
    qi = qi_ref[step]
    ki = ki_ref[step]

    @pl.when(ki == 0)
    def _():
        for h in range(hb):
            m_refs[h][...] = jnp.full(m_refs[h].shape, NEG_INF, F32)
            l_refs[h][...] = jnp.zeros(l_refs[h].shape, F32)
            acc_refs[h][...] = jnp.zeros(acc_refs[h].shape, F32)

    def update(masked):
        for h in range(hb):
            qr = slice(h * 256, (h + 1) * 256)
            vr = slice(h * vdim, (h + 1) * vdim)
            st = jnp.dot(k_ref[:, qr], qt_ref[qr, :], preferred_element_type=F32)
            if masked:
                key = lax.broadcasted_iota(jnp.int32, st.shape, 0)
                qry = lax.broadcasted_iota(jnp.int32, st.shape, 1)
                st = jnp.where(key <= qry, st, NEG_INF)
            m_prev = m_refs[h][...]
            m_new = jnp.maximum(m_prev, jnp.max(st, axis=0, keepdims=True))
            corr = jnp.exp2(m_prev - m_new)
            pt = jnp.exp2(st - m_new)
            l_refs[h][...] = corr * l_refs[h][...] + jnp.sum(pt, axis=0, keepdims=True)
            acc_refs[h][...] = corr * acc_refs[h][...] + jnp.dot(vt_ref[vr, :], pt.astype(BF),
                                                                 preferred_element_type=F32)
            m_refs[h][...] = m_new

    @pl.when(ki < qi)
    def _():
        update(False)

    @pl.when(ki == qi)
    def _():
        update(True)
        for h in range(hb):
            vr = slice(h * vdim, (h + 1) * vdim)
            o_ref[:, vr] = (acc_refs[h][...] / l_refs[h][...]).T.astype(o_ref.dtype)


def _flash_attention(kfull, q_t, v_t, *, batch, seq, heads, vdim, blk, hb, total_rows):
    assert heads % hb == 0
    nb = seq // blk
    pairs = [(a, b) for a in range(nb) for b in range(a + 1)]
    qi_map = jnp.asarray([p[0] for p in pairs], jnp.int32)
    ki_map = jnp.asarray([p[1] for p in pairs], jnp.int32)
    grid_spec = pltpu.PrefetchScalarGridSpec(
        num_scalar_prefetch=2,
        grid=(batch, heads // hb, len(pairs)),
        in_specs=[
            pl.BlockSpec((blk, hb * 256), lambda b, h, s, qi, ki: (b * nb + ki[s], h)),
            pl.BlockSpec((hb * 256, blk), lambda b, h, s, qi, ki: (h, b * nb + qi[s])),
            pl.BlockSpec((hb * vdim, blk), lambda b, h, s, qi, ki: (h, b * nb + ki[s])),
            pl.BlockSpec(memory_space=pl.ANY),
        ],
        out_specs=pl.BlockSpec((blk, hb * vdim), lambda b, h, s, qi, ki: (b * nb + qi[s], h)),
        scratch_shapes=[pltpu.VMEM((1, blk), F32)] * (2 * hb) + [pltpu.VMEM((vdim, blk), F32)] * hb,
    )
    return pl.pallas_call(
        functools.partial(_flash_kernel, hb=hb, vdim=vdim),
        out_shape=jax.ShapeDtypeStruct((total_rows, heads * vdim), BF),
        grid_spec=grid_spec,
        input_output_aliases={5: 0},
        compiler_params=_params(3),
        name="flash_prompt",
    )(qi_map, ki_map, kfull, q_t, v_t, jnp.zeros((total_rows, heads * vdim), BF))


def _absorb_kernel(wuk_ref, qt_ref, o_ref):
    o_ref[...] = jnp.dot(wuk_ref[...], qt_ref[...], preferred_element_type=F32).astype(o_ref.dtype)


def _absorb_query(q_t, wuk, *, heads, nope, c_dim, col_block, cols):
    return pl.pallas_call(
        _absorb_kernel,
        out_shape=jax.ShapeDtypeStruct((heads * c_dim, cols), BF),
        grid=(heads,),
        in_specs=[pl.BlockSpec((c_dim, nope), lambda h: (0, h)),
                  pl.BlockSpec((nope, cols), lambda h: (2 * h, col_block))],
        out_specs=pl.BlockSpec((c_dim, cols), lambda h: (h, 0)),
        compiler_params=_params(1),
        name="absorb_query",
    )(wuk, q_t)


def _page_copies(pt_ref, ck_hbm, kp_hbm, ck_buf, kp_buf, sem, step, slot, i, pages):
    page = pt_ref[step * pages + i]
    return (pltpu.make_async_copy(ck_hbm.at[page], ck_buf.at[slot, i], sem.at[0, slot]),
            pltpu.make_async_copy(kp_hbm.at[page], kp_buf.at[slot, i], sem.at[1, slot]))


def _decode_kernel(pt_ref, qlat_ref, qlt_ref, qpe_ref, cnew_ref, knew_ref, ck_hbm, kp_hbm, o_ref,
                   ck_buf, kp_buf, sem, m_ref, l_ref, acc_ref, *, pages, steps_per_seq):
    n = pl.program_id(0)
    j = n % steps_per_seq
    slot = n % DECODE_SLOTS

    def start_step(step):
        for i in range(pages):
            for cp in _page_copies(pt_ref, ck_hbm, kp_hbm, ck_buf, kp_buf, sem, step, step % DECODE_SLOTS, i, pages):
                cp.start()

    @pl.when(n == 0)
    def _():
        for step in range(DECODE_SLOTS - 1):
            @pl.when(step < pl.num_programs(0))
            def _():
                start_step(step)

    @pl.when(n + DECODE_SLOTS - 1 < pl.num_programs(0))
    def _():
        start_step(n + DECODE_SLOTS - 1)

    @pl.when(j == 0)
    def _():
        m_ref[...] = jnp.full(m_ref.shape, NEG_INF, F32)
        l_ref[...] = jnp.zeros(l_ref.shape, F32)
        acc_ref[...] = jnp.zeros(acc_ref.shape, F32)

    for i in range(pages):
        for cp in _page_copies(pt_ref, ck_hbm, kp_hbm, ck_buf, kp_buf, sem, n, slot, i, pages):
            cp.wait()

    ql = qlat_ref[0]
    qlt = qlt_ref[0]
    qp = qpe_ref[0]
    heads = ql.shape[0]
    cks = []
    scores = []
    for i in range(pages):
        ck = ck_buf[slot, i].astype(BF)
        kp = kp_buf[slot, i].astype(BF)
        cks.append(ck)
        st = jnp.dot(ck, qlt, preferred_element_type=F32)
        st = jnp.where(lax.broadcasted_iota(jnp.int32, st.shape, 1) < heads, st, 0.0)
        scores.append(st.T[:heads] + jnp.dot(qp, kp, preferred_element_type=F32))
    s = jnp.concatenate(scores, axis=-1)
    m_prev = m_ref[...]
    m_new = jnp.maximum(m_prev, jnp.max(s, axis=-1, keepdims=True))
    corr = jnp.exp2(m_prev - m_new)
    p = jnp.exp2(s - m_new)
    l_new = corr * l_ref[...] + jnp.sum(p, axis=-1, keepdims=True)
    pb = p.astype(BF)
    rows = cks[0].shape[0]
    pv = jnp.dot(pb[:, :rows], cks[0], preferred_element_type=F32)
    for i in range(1, pages):
        pv = pv + jnp.dot(pb[:, i * rows:(i + 1) * rows], cks[i], preferred_element_type=F32)
    acc_new = corr * acc_ref[...] + pv
    m_ref[...] = m_new
    l_ref[...] = l_new
    acc_ref[...] = acc_new

    @pl.when(j == steps_per_seq - 1)
    def _():
        cn = cnew_ref[0].astype(BF).astype(F32)
        kn = knew_ref[0].astype(BF).astype(F32)
        s_self = (jnp.sum(ql.astype(F32) * cn, axis=-1, keepdims=True)
                  + jnp.sum(qp.astype(F32) * kn, axis=-1, keepdims=True))
        m_fin = jnp.maximum(m_new, s_self)
        c2 = jnp.exp2(m_new - m_fin)
        p_self = jnp.exp2(s_self - m_fin)
        l_fin = c2 * l_new + p_self
        acc_fin = c2 * acc_new + p_self.astype(BF).astype(F32) * cn
        o_ref[0] = (acc_fin / l_fin).astype(o_ref.dtype)


def _decode_attention(page_table, q_lat, q_lat_t, q_pe, ckv_new, kpe_new, cache_ckv, cache_kpe_t, *, pages):
    Bd, H, C = q_lat.shape
    R = q_pe.shape[2]
    n_pages = page_table.shape[1]
    page_rows = cache_ckv.shape[1]
    assert n_pages % pages == 0
    sps = n_pages // pages
    per_b = lambda *shape: pl.BlockSpec((1,) + shape, lambda n, pt: (n // sps, 0, 0))
    hbm = pl.BlockSpec(memory_space=pl.ANY)
    grid_spec = pltpu.PrefetchScalarGridSpec(
        num_scalar_prefetch=1,
        grid=(Bd * sps,),
        in_specs=[per_b(H, C), per_b(C, 128), per_b(H, R), per_b(1, C), per_b(1, R), hbm, hbm],
        out_specs=per_b(H, C),
        scratch_shapes=[pltpu.VMEM((DECODE_SLOTS, pages, page_rows, C), cache_ckv.dtype),
                        pltpu.VMEM((DECODE_SLOTS, pages, R, page_rows), cache_kpe_t.dtype),
                        pltpu.SemaphoreType.DMA((2, DECODE_SLOTS)),
                        pltpu.VMEM((H, 1), F32), pltpu.VMEM((H, 1), F32), pltpu.VMEM((H, C), F32)],
    )
    return pl.pallas_call(
        functools.partial(_decode_kernel, pages=pages, steps_per_seq=sps),
        out_shape=jax.ShapeDtypeStruct((Bd, H, C), BF),
        grid_spec=grid_spec,
        compiler_params=_params(1),
        name="decode_attention",
    )(page_table.reshape(-1), q_lat, q_lat_t, q_pe, ckv_new, kpe_new, cache_ckv, cache_kpe_t)


def _value_up_kernel(o_hbm_ref, olat_ref, wuv_ref, o_ref):
    del o_hbm_ref
    o_ref[...] = jnp.dot(olat_ref[...], wuv_ref[...], preferred_element_type=F32).astype(o_ref.dtype)


def _value_up_into(o_all, o_lat, wuv, *, heads, c_dim, vdim, row_block, rows):
    return pl.pallas_call(
        _value_up_kernel,
        out_shape=jax.ShapeDtypeStruct(o_all.shape, o_all.dtype),
        grid=(heads,),
        in_specs=[pl.BlockSpec(memory_space=pl.ANY),
                  pl.BlockSpec((rows, c_dim), lambda h: (0, h)),
                  pl.BlockSpec((c_dim, vdim), lambda h: (0, h))],
        out_specs=pl.BlockSpec((rows, vdim), lambda h: (row_block, h)),
        input_output_aliases={0: 0},
        compiler_params=_params(1),
        name="value_up",
    )(o_all, o_lat, wuv)


def _router_kernel(x_ref, mix_ref, g_ref, lng_ref, lnb_ref, sc_ref, sh_ref, wrh_ref, wrl_ref, br_ref,
                   xo_ref, w_ref, i_ref, h_ref, *, n_exp, alpha):
    xn = _layernorm_rows(alpha * x_ref[...] + g_ref[...] * mix_ref[...], lng_ref[...], lnb_ref[...])
    xo_ref[...] = xn
    h = xn * (1.0 + sc_ref[...]) + sh_ref[...]
    h_hi = h.astype(BF)
    h_lo = (h - h_hi.astype(F32)).astype(BF)
    logits = (jnp.dot(h_hi, wrh_ref[...], preferred_element_type=F32)
              + jnp.dot(h_hi, wrl_ref[...], preferred_element_type=F32)
              + jnp.dot(h_lo, wrh_ref[...], preferred_element_type=F32)) + br_ref[...]
    lane = lax.broadcasted_iota(jnp.int32, logits.shape, 1)
    valid = lane < n_exp
    logits = jnp.where(valid, logits, NEG_INF)
    mx = jnp.max(logits, axis=-1, keepdims=True)
    e = jnp.exp(logits - mx)
    probs = e / jnp.sum(e, axis=-1, keepdims=True)
    big = jnp.int32(1 << 30)
    p1 = jnp.max(probs, axis=-1, keepdims=True)
    i1 = jnp.min(jnp.where((probs == p1) & valid, lane, big), axis=-1, keepdims=True)
    rest = jnp.where((lane == i1) | ~valid, -1.0, probs)
    p2 = jnp.max(rest, axis=-1, keepdims=True)
    i2 = jnp.min(jnp.where(rest == p2, lane, big), axis=-1, keepdims=True)
    denom = p1 + p2
    w_ref[...] = jnp.where(lane == 0, p1 / denom, jnp.where(lane == 1, p2 / denom, 0.0))
    i_ref[...] = jnp.where(lane == 0, i1, jnp.where(lane == 1, i2, 0))
    h_ref[...] = h


def _postln_router(x, mix, mod, g_chunk, ln_g, ln_b, sc_chunk, sh_chunk, wr_pad, br_pad, tiles_per_batch, n_exp,
                   *, alpha):
    T, D = x.shape
    row = pl.BlockSpec((ROW_TILE, D), lambda i: (i, 0))
    vec = pl.BlockSpec((1, D), lambda i: (0, 0))
    sel = pl.BlockSpec((ROW_TILE, 128), lambda i: (i, 0))
    wr_hi = wr_pad.astype(BF)
    wr_lo = (wr_pad - wr_hi.astype(F32)).astype(BF)
    wspec = pl.BlockSpec((D, 128), lambda i: (0, 0))
    return pl.pallas_call(
        functools.partial(_router_kernel, n_exp=n_exp, alpha=alpha),
        out_shape=[jax.ShapeDtypeStruct((T, D), F32), jax.ShapeDtypeStruct((T, 128), F32),
                   jax.ShapeDtypeStruct((T, 128), jnp.int32), jax.ShapeDtypeStruct((T, D), F32)],
        grid=(T // ROW_TILE,),
        in_specs=[row, row, _mod_spec(tiles_per_batch, D, g_chunk), vec, vec,
                  _mod_spec(tiles_per_batch, D, sc_chunk), _mod_spec(tiles_per_batch, D, sh_chunk),
                  wspec, wspec, pl.BlockSpec((1, 128), lambda i: (0, 0))],
        out_specs=[row, sel, sel, row],
        compiler_params=_params(1),
        name="postln_router",
    )(x, mix, mod, ln_g.reshape(1, D), ln_b.reshape(1, D), mod, mod, wr_hi, wr_lo, br_pad)


def _route_tables(idx2, n_exp, bm, n_blocks):
    T = idx2.shape[0]
    e_flat = idx2.reshape(-1)
    onehot = (e_flat[:, None] == jnp.arange(n_exp, dtype=jnp.int32)[None, :]).astype(jnp.int32)
    rank = jnp.sum((jnp.cumsum(onehot, axis=0) - onehot) * onehot, axis=1)
    counts = jnp.sum(onehot, axis=0)
    blocks_per = (counts + bm - 1) // bm
    blocks_end = jnp.cumsum(blocks_per)
    dest = (blocks_end - blocks_per)[e_flat] * bm + rank
    n_used = blocks_end[-1]
    b = jnp.minimum(jnp.arange(n_blocks, dtype=jnp.int32), n_used - 1)
    block_expert = jnp.sum((b[:, None] >= blocks_end[None, :]).astype(jnp.int32), axis=1)
    row_token = jnp.full((n_blocks * bm,), -1, jnp.int32).at[dest].set(jnp.arange(2 * T, dtype=jnp.int32) // 2)
    ids = jnp.arange(n_exp, dtype=jnp.int32)
    later_used = (ids[None, :] > ids[:, None]) & (blocks_per[None, :] > 0)
    next_of_expert = jnp.min(jnp.where(later_used, ids[None, :], n_exp), axis=1)
    next_expert = jnp.where(next_of_expert < n_exp, next_of_expert, -1)[block_expert]
    return (row_token, dest.astype(jnp.int32), block_expert.astype(jnp.int32), next_expert.astype(jnp.int32),
            n_used.reshape(1).astype(jnp.int32))


def _row_copy(src_hbm, row, dst, sem):
    return pltpu.make_async_copy(src_hbm.at[pl.ds(row, 1)], dst, sem)


ROW_DMA_UNROLL = 8


def _dispatch_kernel(tok_ref, h_hbm, o_ref, buf, sem, *, rows):
    i = pl.program_id(0)
    slot = i % 2

    def issue(step, s):
        def body(r8, c):
            for k in range(ROW_DMA_UNROLL):
                r = r8 * ROW_DMA_UNROLL + k
                tok = tok_ref[step * rows + r]

                @pl.when(tok >= 0)
                def _():
                    _row_copy(h_hbm, tok, buf.at[s, pl.ds(r, 1)], sem.at[s]).start(priority=k % 2)
            return c
        lax.fori_loop(0, rows // ROW_DMA_UNROLL, body, 0)

    @pl.when(i == 0)
    def _():
        buf[...] = jnp.zeros(buf.shape, buf.dtype)
        issue(0, 0)

    @pl.when(i + 1 < pl.num_programs(0))
    def _():
        issue(i + 1, 1 - slot)

    def wait_body(r, c):
        @pl.when(tok_ref[i * rows + r] >= 0)
        def _():
            _row_copy(h_hbm, 0, buf.at[slot, pl.ds(r, 1)], sem.at[slot]).wait()
        return c
    lax.fori_loop(0, rows, wait_body, 0, unroll=8)
    o_ref[...] = buf[slot].astype(o_ref.dtype)


def _dispatch(h, row_token, *, rows=256):
    T, D = h.shape
    Rn = row_token.shape[0]
    if Rn % rows:
        rows = ROW_TILE
    assert Rn % rows == 0 and rows % ROW_DMA_UNROLL == 0
    grid_spec = pltpu.PrefetchScalarGridSpec(
        num_scalar_prefetch=1,
        grid=(Rn // rows,),
        in_specs=[pl.BlockSpec(memory_space=pl.ANY)],
        out_specs=pl.BlockSpec((rows, D), lambda i, tok: (i, 0)),
        scratch_shapes=[pltpu.VMEM((2, rows, D), h.dtype), pltpu.SemaphoreType.DMA((2,))],
    )
    return pl.pallas_call(
        functools.partial(_dispatch_kernel, rows=rows),
        out_shape=jax.ShapeDtypeStruct((Rn, D), h.dtype),
        grid_spec=grid_spec,
        compiler_params=_params(1),
        name="moe_dispatch",
    )(row_token, h)


def _combine_postln_kernel(dest_ref, x_ref, g_ref, lng_ref, lnb_ref, w_ref, y_hbm, yp_ref, ys_ref, buf, sem,
                           *, alpha, rows, n_prompt_tiles):
    i = pl.program_id(0)
    slot = i % 2

    def issue(step, s):
        def body(r, c):
            a = (step * rows + r) * 2
            _row_copy(y_hbm, dest_ref[a], buf.at[s, 0, pl.ds(r, 1)], sem.at[s]).start(priority=0)
            _row_copy(y_hbm, dest_ref[a + 1], buf.at[s, 1, pl.ds(r, 1)], sem.at[s]).start(priority=1)
            return c
        lax.fori_loop(0, rows, body, 0, unroll=8)

    @pl.when(i == 0)
    def _():
        issue(0, 0)

    @pl.when(i + 1 < pl.num_programs(0))
    def _():
        issue(i + 1, 1 - slot)

    def wait_body(r, c):
        _row_copy(y_hbm, 0, buf.at[slot, 0, pl.ds(r, 1)], sem.at[slot]).wait()
        _row_copy(y_hbm, 0, buf.at[slot, 1, pl.ds(r, 1)], sem.at[slot]).wait()
        return c
    lax.fori_loop(0, rows, wait_body, 0, unroll=8)
    f = w_ref[:, 0:1] * buf[slot, 0] + w_ref[:, 1:2] * buf[slot, 1]
    xn = _layernorm_rows(alpha * x_ref[...] + g_ref[...] * f, lng_ref[...], lnb_ref[...])

    @pl.when(i < n_prompt_tiles)
    def _():
        yp_ref[...] = xn

    @pl.when(i >= n_prompt_tiles)
    def _():
        ys_ref[...] = xn


def _combine_postln(x, y, dest, wts, mod_g, g_chunk, ln_g, ln_b, tiles_per_batch, n_prompt_tiles, *, alpha):
    T, D = x.shape
    rows = ROW_TILE
    assert T == (n_prompt_tiles + 1) * rows
    row = pl.BlockSpec((rows, D), lambda i, d: (i, 0))
    vec = pl.BlockSpec((1, D), lambda i, d: (0, 0))
    grid_spec = pltpu.PrefetchScalarGridSpec(
        num_scalar_prefetch=1,
        grid=(T // rows,),
        in_specs=[row, pl.BlockSpec((rows, D), lambda i, d: (i // tiles_per_batch, g_chunk)), vec, vec,
                  pl.BlockSpec((rows, 128), lambda i, d: (i, 0)), pl.BlockSpec(memory_space=pl.ANY)],
        out_specs=[pl.BlockSpec((rows, D), lambda i, d: (jnp.minimum(i, n_prompt_tiles - 1), 0)),
                   pl.BlockSpec((rows, D), lambda i, d: (0, 0))],
        scratch_shapes=[pltpu.VMEM((2, 2, rows, D), F32), pltpu.SemaphoreType.DMA((2,))],
    )
    return pl.pallas_call(
        functools.partial(_combine_postln_kernel, alpha=alpha, rows=rows, n_prompt_tiles=n_prompt_tiles),
        out_shape=[jax.ShapeDtypeStruct((n_prompt_tiles * rows, D), F32), jax.ShapeDtypeStruct((rows, D), F32)],
        grid_spec=grid_spec,
        compiler_params=_params(1),
        name="moe_combine_postln",
    )(dest, x, mod_g, ln_g.reshape(1, D), ln_b.reshape(1, D), wts, y)


def _pick_bm(T, target):
    best = ROW_TILE
    for m in range(ROW_TILE, target + 1, ROW_TILE):
        if T % m == 0:
            best = m
    return best


def kernel(x_prompt, x_sample, cache_ckv, cache_kpe, page_table, c_prompt, c_sample, w_ada, b_ada, ln_g, ln_b, a_w_in, a_v_g, a_v_b, a_w_s, a_b_s, a_w_out, b_w_dq, b_q_g, b_w_uq, b_w_qr, b_w_o, kv_w_dkv, kv_g, kv_w_kr, kv_w_uk, kv_w_uv, ffn_w_gate, ffn_w_up, ffn_w_down, moe_w_router, moe_b_router, moe_w_gate, moe_w_up, moe_w_down):
    B, S, D = x_prompt.shape
    Bd, Td, _ = x_sample.shape
    depth = w_ada.shape[0]
    n_a = a_w_in.shape[0]
    assert depth == 2 and n_a == 1 and Td == 1 and Bd == ROW_TILE and S % ROW_TILE == 0
    A = a_w_in.shape[2] // 2
    G = a_w_s.shape[1]
    H, nope = b_w_uq.shape[2], b_w_uq.shape[3]
    R = b_w_qr.shape[3]
    C = kv_w_dkv.shape[1]
    vdim = kv_w_uv.shape[2]
    E = moe_w_router.shape[2]
    n_pages, page_rows = page_table.shape[1], cache_ckv.shape[1]
    past_len = n_pages * page_rows
    alpha = (2.0 * depth) ** 0.25
    scale = (nope + R) ** -0.5
    Tp = B * S
    T = Tp + Bd
    tiles_per_batch = S // ROW_TILE
    n_prompt_tiles = Tp // ROW_TILE
    bm = _pick_bm(T, 640)
    bm_mid = _pick_bm(T, 832)
    bm_big = _pick_bm(T, 1664)

    x = (x_prompt.reshape(Tp, D), x_sample.reshape(Bd, D))

    c_tiles = jnp.concatenate([jnp.repeat(c_prompt, ROW_TILE, axis=0), c_sample], axis=0)
    mods = [_wsmm(c_tiles, [w_ada], bm=c_tiles.shape[0], bn=1024, out_dtype=F32, act="silu_in",
                  bias=b_ada[l].reshape(1, -1), w_index=l, name="adaln") for l in range(depth)]
    SH_M, SC_M, G_M, SH_F, SC_F, G_F = range(6)

    half = R // 2
    inv = ROPE_BASE ** (-jnp.arange(half, dtype=F32) / half)
    pos = jnp.concatenate([jnp.tile(jnp.arange(S, dtype=jnp.int32), B),
                           jnp.full((Bd,), past_len, jnp.int32)]).astype(F32)
    ang = pos[:, None] * inv[None, :]
    cos2 = jnp.concatenate([jnp.cos(ang), jnp.cos(ang)], axis=-1)
    sin2 = jnp.concatenate([jnp.sin(ang), jnp.sin(ang)], axis=-1)
    qscale = scale * math.log2(math.e)
    row_pad = jnp.zeros((128 - R, T), F32)
    cos_qt = jnp.concatenate([cos2.T * qscale, row_pad], axis=0)
    sin_qt = jnp.concatenate([sin2.T * qscale, row_pad], axis=0)

    h = _modulate(x, mods[0], SC_M, SH_M, tiles_per_batch)
    u = _wsmm(h, [a_w_in], bm=bm_big, bn=512, out_dtype=BF, n_off=0, n_out=A, act="gelu", name="a_in_u")
    vpre = _wsmm(h, [a_w_in], bm=bm_big, bn=512, out_dtype=F32, n_off=A, n_out=A, act="gelu", name="a_in_v")
    tril = jnp.tril(jnp.ones((ROW_TILE, ROW_TILE), bool))
    ws_prompt = jnp.where(tril, a_w_s[0], 0.0)
    ws_sample = a_w_s[0][:, :1, :1] * jnp.eye(ROW_TILE, dtype=F32)
    ws_tiles = jnp.stack([ws_prompt, ws_sample]).astype(BF)
    gd = A // G
    b_prompt = jnp.repeat(a_b_s[0].T, gd, axis=1)
    b_sample = jnp.broadcast_to(b_prompt[:1], b_prompt.shape)
    b_tiles = jnp.stack([b_prompt, b_sample])
    gated, v_sample = _spatial_gate(vpre, u, ws_tiles, b_tiles, a_v_g[0], a_v_b[0], n_prompt_tiles)
    mix = _wsmm(gated, [a_w_out], bm=bm_mid, bn=512, out_dtype=F32, name="a_out")
    x, h = _postln(x, mix, mods[0], G_M, ln_g[0, 0], ln_b[0, 0], tiles_per_batch, alpha=alpha,
                   mod_next=mods[0], sc_chunk=SC_F, sh_chunk=SH_F)
    act = _wsmm(h, [ffn_w_gate, ffn_w_up], bm=bm_big, bn=512, out_dtype=BF, name="ffn_up")
    f = _wsmm(act, [ffn_w_down], bm=bm, bn=512, out_dtype=F32, name="ffn_down")
    x, h, xb = _postln(x, f, mods[0], G_F, ln_g[0, 1], ln_b[0, 1], tiles_per_batch, alpha=alpha,
                       mod_next=mods[1], sc_chunk=SC_M, sh_chunk=SH_M, emit_bf=True)

    w1, w2 = kv_w_kr[:, :half], kv_w_kr[:, half:]
    zpad = jnp.zeros((D, 128 - R), F32)
    wkv = jnp.concatenate([kv_w_dkv, kv_w_kr, zpad, -w2, w1, zpad], axis=1).astype(BF)
    wuk2 = kv_w_uk.reshape(C, H * nope)
    wuv2 = kv_w_uv.reshape(C, H * vdim)
    ckv, kpe, kfull, v_t = _kv_side(xb, wkv, kv_g.reshape(1, C), cos2, sin2, wuk2.astype(BF), wuv2.T.astype(BF),
                                    bm=bm, c_dim=C, r_dim=R, heads=H, nope=nope)

    wqr = b_w_qr[0]
    qpad = jnp.zeros(wqr.shape[:2] + (128 - R,), F32)
    wpe = jnp.concatenate([wqr, qpad], axis=-1).reshape(wqr.shape[0], H * 128)
    q_t = _q_side(h, b_w_dq[0].T.astype(BF), b_q_g[0].reshape(-1, 1), b_w_uq[0].reshape(-1, H * nope).T.astype(BF),
                  wpe.T.astype(BF), cos_qt, sin_qt, bm=bm, heads=H, nope=nope, scale=qscale, half=half)
    o_all = _flash_attention(kfull, q_t, v_t, batch=B, seq=S, heads=H, vdim=vdim, blk=min(1024, S), hb=min(8, H),
                             total_rows=T)

    q_lat_hc = _absorb_query(q_t, wuk2.astype(BF), heads=H, nope=nope, c_dim=C,
                             col_block=Tp // Bd, cols=Bd).reshape(H, C, Bd)
    q_lat = q_lat_hc.transpose(2, 0, 1)
    q_lat_t = jnp.pad(q_lat_hc.transpose(2, 1, 0), ((0, 0), (0, 0), (0, 128 - H)))
    q_pe_s = q_t[:, Tp:].reshape(H, 256, Bd)[:, nope:nope + R, :].transpose(2, 0, 1)
    o_lat = _decode_attention(page_table, q_lat, q_lat_t, q_pe_s, ckv[Tp:].reshape(Bd, 1, C),
                              kpe[Tp:].reshape(Bd, 1, R), cache_ckv, jnp.swapaxes(cache_kpe, 1, 2),
                              pages=min(32, n_pages))
    o_all = _value_up_into(o_all, o_lat.reshape(Bd, H * C), wuv2.astype(BF), heads=H, c_dim=C, vdim=vdim,
                           row_block=Tp // Bd, rows=Bd)
    mix = _wsmm(o_all, [b_w_o], bm=bm_big, bn=512, out_dtype=F32, name="attn_out")
    wr_pad = jnp.concatenate([moe_w_router[0], jnp.zeros((D, 128 - E), F32)], axis=1)
    br_pad = jnp.concatenate([moe_b_router[0], jnp.zeros((128 - E,), F32)]).reshape(1, 128)
    x, wts, idx, hf = _postln_router(x, mix, mods[1], G_M, ln_g[1, 0], ln_b[1, 0], SC_F, SH_F, wr_pad, br_pad,
                                     tiles_per_batch, E, alpha=alpha)
    bme = 256
    n_blocks = pl.cdiv(2 * T + E * (bme - 1), bme)
    row_token, dest, block_expert, next_expert, n_used = _route_tables(idx[:, :2], E, bme, n_blocks)
    grouped = dict(block_w=block_expert, next_w=next_expert, n_used=n_used)
    hs = _dispatch(hf, row_token)
    per_expert = lambda w: w.reshape((-1,) + w.shape[-2:])
    act = _wsmm(hs, [per_expert(moe_w_gate), per_expert(moe_w_up)], bm=bme, bn=1024, out_dtype=BF,
                name="moe_up", **grouped)
    ys = _wsmm(act, [per_expert(moe_w_down)], bm=bme, bn=512, out_dtype=F32, name="moe_down", **grouped)
    y_prompt, y_sample = _combine_postln(x, ys, dest, wts, mods[1], G_F, ln_g[1, 1], ln_b[1, 1], tiles_per_batch,
                                         n_prompt_tiles, alpha=alpha)
    return (y_prompt.reshape(B, S, D), y_sample.reshape(Bd, Td, D),
            ckv[:Tp].reshape(B, S, C), kpe[:Tp].reshape(B, S, R),
            ckv[Tp:].reshape(Bd, Td, C), kpe[Tp:].reshape(Bd, Td, R),
            v_sample.reshape(n_a, Bd, Td, A))
```

```python
import functools
import math

import jax
import jax.numpy as jnp
from jax import lax
from jax.experimental import pallas as pl
from jax.experimental.pallas import tpu as pltpu

BF = jnp.bfloat16
F32 = jnp.float32

ROW_TILE = 128
LN_EPS = 1e-5
RMS_EPS = 1e-6
ROPE_BASE = 10000.0
NEG_INF = float("-inf")
VMEM_LIMIT = 56 * 1024 * 1024
DECODE_SLOTS = 3

def _params(n_axes, vmem=VMEM_LIMIT):
    return pltpu.CompilerParams(dimension_semantics=("arbitrary",) * n_axes, vmem_limit_bytes=vmem)


def _gelu_tanh(x):
    return 0.5 * x * (1.0 + jnp.tanh(math.sqrt(2.0 / math.pi) * (x + 0.044715 * x * x * x)))


def _silu(x):
    return x * (1.0 / (1.0 + jnp.exp(-x)))


def _layernorm_rows(x, g, b):
    mu = jnp.mean(x, axis=-1, keepdims=True)
    xc = x - mu
    var = jnp.mean(xc * xc, axis=-1, keepdims=True)
    return xc * lax.rsqrt(var + LN_EPS) * g + b


def _group_starts(bw_ref, i):
    return (i == 0) | (bw_ref[i] != bw_ref[jnp.maximum(i - 1, 0)])


def _wsmm_kernel(bw_ref, nxt_ref, nu_ref, x_ref, *refs, n_w, act, has_bias, bn, n_off, row_chunks):
    w_hbm = refs[:n_w]
    k = n_w
    bias_ref = None
    if has_bias:
        bias_ref = refs[k]
        k += 1
    o_ref = refs[k]
    stage = refs[k + 1:k + 1 + n_w]
    wbf = refs[k + 1 + n_w:k + 1 + 2 * n_w]
    sem, cnt = refs[k + 1 + 2 * n_w], refs[k + 2 + 2 * n_w]
    j = pl.program_id(0)
    i = pl.program_id(1)

    def weight_copies(w_idx, jj, slot):
        col = pl.multiple_of(jj * bn + n_off, 128)
        return [pltpu.make_async_copy(w_hbm[t].at[w_idx, :, pl.ds(col, bn)], stage[t].at[slot], sem.at[t, slot])
                for t in range(n_w)]

    @pl.when((j == 0) & (i == 0))
    def _():
        cnt[0] = 0
        for cp in weight_copies(bw_ref[0], 0, 0):
            cp.start()

    @pl.when(_group_starts(bw_ref, i))
    def _():
        slot = cnt[0] % 2
        for cp in weight_copies(bw_ref[i], j, slot):
            cp.wait()
        more = nxt_ref[i] >= 0
        nw = jnp.where(more, nxt_ref[i], bw_ref[0])
        nj = jnp.where(more, j, j + 1)

        @pl.when(nj < pl.num_programs(0))
        def _():
            for cp in weight_copies(nw, nj, 1 - slot):
                cp.start()

        for t in range(n_w):
            wbf[t][...] = stage[t][slot].astype(BF)
        cnt[0] = cnt[0] + 1

    @pl.when(i < nu_ref[0])
    def _():
        rows = x_ref.shape[0] // row_chunks
        for c in range(row_chunks):
            rs = slice(c * rows, (c + 1) * rows)
            x = x_ref[rs, :]
            if act == "silu_in":
                x = _silu(x.astype(F32))
            x = x.astype(BF)
            acc = jnp.dot(x, wbf[0][...], preferred_element_type=F32)
            if n_w == 2:
                acc = _silu(acc) * jnp.dot(x, wbf[1][...], preferred_element_type=F32)
            if has_bias:
                acc = acc + bias_ref[...]
            if act == "gelu":
                acc = _gelu_tanh(acc)
            o_ref[rs, :] = acc.astype(o_ref.dtype)

    @pl.when(i >= nu_ref[0])
    def _():
        o_ref[...] = jnp.zeros(o_ref.shape, o_ref.dtype)


def _wsmm(x, ws, *, bm, bn, out_dtype, block_w=None, next_w=None, n_used=None, w_index=0, n_off=0,
          n_out=None, act=None, bias=None, name="wsmm"):
    M, K = x.shape
    N = ws[0].shape[-1] if n_out is None else n_out
    bn = min(bn, N)
    while N % bn:
        bn //= 2
    assert M % bm == 0 and bn % 128 == 0 and n_off % bn == 0 and all(w.ndim == 3 and w.shape[1] == K for w in ws)
    n_i = M // bm
    if block_w is None:
        block_w = jnp.full((n_i,), w_index, jnp.int32)
        next_w = jnp.full((n_i,), -1, jnp.int32)
        n_used = jnp.full((1,), n_i, jnp.int32)
    n_w = len(ws)
    offb = n_off // bn
    row_chunks = max(c for c in (1, 2, 4) if bm % (16 * c) == 0 and (c == 1 or bm // c >= 384))
    in_specs = [pl.BlockSpec((bm, K), lambda j, i, *_: (i, 0))] + [pl.BlockSpec(memory_space=pl.ANY)] * n_w
    args = [x, *ws]
    if bias is not None:
        in_specs.append(pl.BlockSpec((1, bn), lambda j, i, *_: (0, j + offb)))
        args.append(bias)
    grid_spec = pltpu.PrefetchScalarGridSpec(
        num_scalar_prefetch=3,
        grid=(N // bn, n_i),
        in_specs=in_specs,
        out_specs=pl.BlockSpec((bm, bn), lambda j, i, *_: (i, j)),
        scratch_shapes=[pltpu.VMEM((2, K, bn), ws[0].dtype)] * n_w + [pltpu.VMEM((K, bn), BF)] * n_w
        + [pltpu.SemaphoreType.DMA((n_w, 2)), pltpu.SMEM((1,), jnp.int32)],
    )
    return pl.pallas_call(
        functools.partial(_wsmm_kernel, n_w=n_w, act=act, has_bias=bias is not None, bn=bn, n_off=n_off,
                          row_chunks=row_chunks),
        out_shape=jax.ShapeDtypeStruct((M, N), out_dtype),
        grid_spec=grid_spec,
        compiler_params=_params(2),
        name=name,
    )(block_w, next_w, n_used, *args)


def _mod_spec(tiles_per_batch, D, chunk):
    return pl.BlockSpec((ROW_TILE, D), lambda i: (i // tiles_per_batch, chunk))


def _token_rows(x):
    if not isinstance(x, tuple):
        T, D = x.shape
        return [x], [pl.BlockSpec((ROW_TILE, D), lambda i, *_: (i, 0))], T, D, None
    xp, xs = x
    npt = xp.shape[0] // ROW_TILE
    assert xs.shape[0] == ROW_TILE
    D = xp.shape[1]
    specs = [pl.BlockSpec((ROW_TILE, D), lambda i, *_: (jnp.minimum(i, npt - 1), 0)),
             pl.BlockSpec((ROW_TILE, D), lambda i, *_: (0, 0))]
    return [xp, xs], specs, xp.shape[0] + xs.shape[0], D, npt


def _read_token_tile(x_refs, n_prompt_tiles):
    if len(x_refs) == 1:
        return x_refs[0][...]
    return jnp.where(pl.program_id(0) < n_prompt_tiles, x_refs[0][...], x_refs[1][...])


def _modulate_kernel(*refs, n_x, n_prompt_tiles):
    sc_ref, sh_ref, h_ref = refs[n_x:]
    x = _read_token_tile(refs[:n_x], n_prompt_tiles)
    h_ref[...] = (x * (1.0 + sc_ref[...]) + sh_ref[...]).astype(h_ref.dtype)


def _modulate(x, mod, sc_chunk, sh_chunk, tiles_per_batch):
    xs, x_specs, T, D, npt = _token_rows(x)
    return pl.pallas_call(
        functools.partial(_modulate_kernel, n_x=len(xs), n_prompt_tiles=npt),
        out_shape=jax.ShapeDtypeStruct((T, D), BF),
        grid=(T // ROW_TILE,),
        in_specs=x_specs + [_mod_spec(tiles_per_batch, D, sc_chunk), _mod_spec(tiles_per_batch, D, sh_chunk)],
        out_specs=pl.BlockSpec((ROW_TILE, D), lambda i: (i, 0)),
        compiler_params=_params(1),
        name="modulate",
    )(*xs, mod, mod)


def _postln_kernel(*refs, n_x, n_prompt_tiles, alpha, modulate_next, emit_bf):
    x = _read_token_tile(refs[:n_x], n_prompt_tiles)
    mix_ref, g_ref, lng_ref, lnb_ref = refs[n_x:n_x + 4]
    k = n_x + 4
    if modulate_next:
        sc_ref, sh_ref = refs[k], refs[k + 1]
        k += 2
    xo_ref = refs[k]; k += 1
    xn = _layernorm_rows(alpha * x + g_ref[...] * mix_ref[...], lng_ref[...], lnb_ref[...])
    xo_ref[...] = xn
    if modulate_next:
        refs[k][...] = (xn * (1.0 + sc_ref[...]) + sh_ref[...]).astype(BF)
        k += 1
    if emit_bf:
        refs[k][...] = xn.astype(BF)


def _postln(x, mix, mod_g, g_chunk, ln_g, ln_b, tiles_per_batch, *, alpha, mod_next=None,
            sc_chunk=None, sh_chunk=None, emit_bf=False):
    xs, x_specs, T, D, npt = _token_rows(x)
    row = pl.BlockSpec((ROW_TILE, D), lambda i: (i, 0))
    vec = pl.BlockSpec((1, D), lambda i: (0, 0))
    in_specs = x_specs + [row, _mod_spec(tiles_per_batch, D, g_chunk), vec, vec]
    args = xs + [mix, mod_g, ln_g.reshape(1, D), ln_b.reshape(1, D)]
    out_shape = [jax.ShapeDtypeStruct((T, D), F32)]
    out_specs = [row]
    if mod_next is not None:
        in_specs += [_mod_spec(tiles_per_batch, D, sc_chunk), _mod_spec(tiles_per_batch, D, sh_chunk)]
        args += [mod_next, mod_next]
        out_shape.append(jax.ShapeDtypeStruct((T, D), BF))
        out_specs.append(row)
    if emit_bf:
        out_shape.append(jax.ShapeDtypeStruct((T, D), BF))
        out_specs.append(row)
    return pl.pallas_call(
        functools.partial(_postln_kernel, n_x=len(xs), n_prompt_tiles=npt, alpha=alpha,
                          modulate_next=mod_next is not None, emit_bf=emit_bf),
        out_shape=out_shape,
        grid=(T // ROW_TILE,),
        in_specs=in_specs,
        out_specs=out_specs,
        compiler_params=_params(1),
        name="postln",
    )(*args)


def _spatial_gate_kernel(vpre_ref, u_ref, ws_ref, b_ref, vg_ref, vb_ref, mix_ref, vout_ref, *, groups):
    v = _layernorm_rows(vpre_ref[...], vg_ref[...], vb_ref[...])

    @pl.when(pl.program_id(0) == pl.num_programs(0) - 1)
    def _():
        vout_ref[...] = v

    vb16 = v.astype(BF)
    gd = v.shape[1] // groups
    for g in range(groups):
        cols = slice(g * gd, (g + 1) * gd)
        s = jnp.dot(ws_ref[g], vb16[:, cols], preferred_element_type=F32) + b_ref[:, cols]
        mix_ref[:, cols] = (u_ref[:, cols].astype(F32) * s).astype(BF)


def _spatial_gate(vpre, u, ws_tiles, b_tiles, v_g, v_b, n_prompt_tiles):
    T, A = vpre.shape
    G = ws_tiles.shape[1]
    row = pl.BlockSpec((ROW_TILE, A), lambda i: (i, 0))
    vec = pl.BlockSpec((1, A), lambda i: (0, 0))
    return pl.pallas_call(
        functools.partial(_spatial_gate_kernel, groups=G),
        out_shape=[jax.ShapeDtypeStruct((T, A), BF), jax.ShapeDtypeStruct((ROW_TILE, A), F32)],
        grid=(T // ROW_TILE,),
        in_specs=[row, row,
                  pl.BlockSpec((None, G, ROW_TILE, ROW_TILE), lambda i: (i // n_prompt_tiles, 0, 0, 0)),
                  pl.BlockSpec((None, ROW_TILE, A), lambda i: (i // n_prompt_tiles, 0, 0)),
                  vec, vec],
        out_specs=[row, pl.BlockSpec((ROW_TILE, A), lambda i: (0, 0))],
        compiler_params=_params(1),
        name="spatial_gate",
    )(vpre, u, ws_tiles, b_tiles, v_g.reshape(1, A), v_b.reshape(1, A))


_CONTRACT_LAST = (((1,), (1,)), ((), ()))


def _kv_kernel(x_ref, wkv_ref, kvg_ref, cos_ref, sin_ref, wuk_ref, wuvt_ref,
               ckv_ref, kpe_ref, kfull_ref, vt_ref, *, c_dim, r_dim, heads, nope):
    y = jnp.dot(x_ref[...], wkv_ref[...], preferred_element_type=F32)
    c = y[:, :c_dim]
    ckv = c * lax.rsqrt(jnp.mean(c * c, axis=-1, keepdims=True) + RMS_EPS) * kvg_ref[...]
    ckv_ref[...] = ckv
    kpe = y[:, c_dim:c_dim + r_dim] * cos_ref[...] + y[:, c_dim + 128:c_dim + 128 + r_dim] * sin_ref[...]
    kpe_ref[...] = kpe
    cb = ckv.astype(BF)
    kn = jnp.dot(cb, wuk_ref[...], preferred_element_type=F32)
    vt_ref[...] = lax.dot_general(wuvt_ref[...], cb, _CONTRACT_LAST, preferred_element_type=F32).astype(BF)
    kpe_pad = jnp.concatenate([kpe, jnp.zeros((kpe.shape[0], 128 - r_dim), F32)], axis=-1).astype(BF)
    for h in range(heads):
        kfull_ref[:, h * 256:h * 256 + nope] = kn[:, h * nope:(h + 1) * nope].astype(BF)
        kfull_ref[:, h * 256 + nope:(h + 1) * 256] = kpe_pad


def _kv_side(xb, wkv, kv_g, cos_k, sin_k, wuk, wuv_t, *, bm, c_dim, r_dim, heads, nope):
    T, D = xb.shape
    assert nope == 128 and r_dim <= 128
    row = lambda w: pl.BlockSpec((bm, w), lambda i: (i, 0))
    full = lambda a: pl.BlockSpec(a.shape, lambda i: (0,) * a.ndim)
    hv = wuv_t.shape[0]
    return pl.pallas_call(
        functools.partial(_kv_kernel, c_dim=c_dim, r_dim=r_dim, heads=heads, nope=nope),
        out_shape=[jax.ShapeDtypeStruct((T, c_dim), F32), jax.ShapeDtypeStruct((T, r_dim), F32),
                   jax.ShapeDtypeStruct((T, heads * 256), BF), jax.ShapeDtypeStruct((hv, T), BF)],
        grid=(T // bm,),
        in_specs=[row(D), full(wkv), full(kv_g), row(r_dim), row(r_dim), full(wuk), full(wuv_t)],
        out_specs=[row(c_dim), row(r_dim), row(heads * 256), pl.BlockSpec((hv, bm), lambda i: (0, i))],
        compiler_params=_params(1),
        name="kv_side",
    )(xb, wkv, kv_g, cos_k, sin_k, wuk, wuv_t)


def _q_kernel(h_ref, wdqt_ref, qg_ref, wuqt_ref, wpet_ref, cost_ref, sint_ref, qt_ref,
              *, heads, nope, scale, half):
    ct = lax.dot_general(wdqt_ref[...], h_ref[...], _CONTRACT_LAST, preferred_element_type=F32)
    cqt = (ct * lax.rsqrt(jnp.mean(ct * ct, axis=0, keepdims=True) + RMS_EPS) * qg_ref[...]).astype(BF)
    qn = jnp.dot(wuqt_ref[...], cqt, preferred_element_type=F32) * scale
    qp = jnp.dot(wpet_ref[...], cqt, preferred_element_type=F32)
    cos = cost_ref[...]
    sin = sint_ref[...]
    for h in range(heads):
        qt_ref[h * 256:h * 256 + nope, :] = qn[h * nope:(h + 1) * nope].astype(BF)
        x = qp[h * 128:(h + 1) * 128]
        rot = (jnp.concatenate([x[128 - half:], x[:128 - half]], axis=0)
               - jnp.concatenate([x[half:], x[:half]], axis=0))
        pe = x * cos + rot * sin
        qt_ref[h * 256 + nope:(h + 1) * 256, :] = pe.astype(BF)


def _q_side(hb, wdq_t, q_g_col, wuq_t, wpe_t, cos_t, sin_t, *, bm, heads, nope, scale, half):
    T, D = hb.shape
    assert nope == 128
    full = lambda a: pl.BlockSpec(a.shape, lambda i: (0,) * a.ndim)
    col = lambda r: pl.BlockSpec((r, bm), lambda i: (0, i))
    return pl.pallas_call(
        functools.partial(_q_kernel, heads=heads, nope=nope, scale=scale, half=half),
        out_shape=jax.ShapeDtypeStruct((heads * 256, T), BF),
        grid=(T // bm,),
        in_specs=[pl.BlockSpec((bm, D), lambda i: (i, 0)), full(wdq_t), full(q_g_col), full(wuq_t),
                  full(wpe_t), col(128), col(128)],
        out_specs=col(heads * 256),
        compiler_params=_params(1),
        name="q_side",
    )(hb, wdq_t, q_g_col, wuq_t, wpe_t, cos_t, sin_t)


def _flash_kernel(qi_ref, ki_ref, k_ref, qt_ref, vt_ref, o_init_ref, o_ref, *scratch, hb, vdim):
    del o_init_ref
    m_refs, l_refs, acc_refs = scratch[:hb], scratch[hb:2 * hb], scratch[2 * hb:]
    step = pl.program_id(2)
    qi = qi_ref[step]
    ki = ki_ref[step]

    @pl.when(ki == 0)
    def _():
        for h in range(hb):
            m_refs[h][...] = jnp.full(m_refs[h].shape, NEG_INF, F32)
            l_refs[h][...] = jnp.zeros(l_refs[h].shape, F32)
            acc_refs[h][...] = jnp.zeros(acc_refs[h].shape, F32)

    def update(masked):
        for h in range(hb):
            qr = slice(h * 256, (h + 1) * 256)
            vr = slice(h * vdim, (h + 1) * vdim)
            st = jnp.dot(k_ref[:, qr], qt_ref[qr, :], preferred_element_type=F32)
            if masked:
                key = lax.broadcasted_iota(jnp.int32, st.shape, 0)
                qry = lax.broadcasted_iota(jnp.int32, st.shape, 1)
                st = jnp.where(key <= qry, st, NEG_INF)
            m_prev = m_refs[h][...]
            m_new = jnp.maximum(m_prev, jnp.max(st, axis=0, keepdims=True))
            corr = jnp.exp2(m_prev - m_new)
            pt = jnp.exp2(st - m_new)
            l_refs[h][...] = corr * l_refs[h][...] + jnp.sum(pt, axis=0, keepdims=True)
            acc_refs[h][...] = corr * acc_refs[h][...] + jnp.dot(vt_ref[vr, :], pt.astype(BF),
                                                                 preferred_element_type=F32)
            m_refs[h][...] = m_new

    @pl.when(ki < qi)
    def _():
        update(False)

    @pl.when(ki == qi)
    def _():
        update(True)
        for h in range(hb):
            vr = slice(h * vdim, (h + 1) * vdim)
            o_ref[:, vr] = (acc_refs[h][...] / l_refs[h][...]).T.astype(o_ref.dtype)


def _flash_attention(kfull, q_t, v_t, *, batch, seq, heads, vdim, blk, hb, total_rows):
    assert heads % hb == 0
    nb = seq // blk
    pairs = [(a, b) for a in range(nb) for b in range(a + 1)]
    qi_map = jnp.asarray([p[0] for p in pairs], jnp.int32)
    ki_map = jnp.asarray([p[1] for p in pairs], jnp.int32)
    grid_spec = pltpu.PrefetchScalarGridSpec(
        num_scalar_prefetch=2,
        grid=(batch, heads // hb, len(pairs)),
        in_specs=[
            pl.BlockSpec((blk, hb * 256), lambda b, h, s, qi, ki: (b * nb + ki[s], h)),
            pl.BlockSpec((hb * 256, blk), lambda b, h, s, qi, ki: (h, b * nb + qi[s])),
            pl.BlockSpec((hb * vdim, blk), lambda b, h, s, qi, ki: (h, b * nb + ki[s])),
            pl.BlockSpec(memory_space=pl.ANY),
        ],
        out_specs=pl.BlockSpec((blk, hb * vdim), lambda b, h, s, qi, ki: (b * nb + qi[s], h)),
        scratch_shapes=[pltpu.VMEM((1, blk), F32)] * (2 * hb) + [pltpu.VMEM((vdim, blk), F32)] * hb,
    )
    return pl.pallas_call(
        functools.partial(_flash_kernel, hb=hb, vdim=vdim),
        out_shape=jax.ShapeDtypeStruct((total_rows, heads * vdim), BF),
        grid_spec=grid_spec,
        input_output_aliases={5: 0},
        compiler_params=_params(3),
        name="flash_prompt",
    )(qi_map, ki_map, kfull, q_t, v_t, jnp.zeros((total_rows, heads * vdim), BF))


def _absorb_kernel(wuk_ref, qt_ref, o_ref):
    o_ref[...] = jnp.dot(wuk_ref[...], qt_ref[...], preferred_element_type=F32).astype(o_ref.dtype)


def _absorb_query(q_t, wuk, *, heads, nope, c_dim, col_block, cols):
    return pl.pallas_call(
        _absorb_kernel,
        out_shape=jax.ShapeDtypeStruct((heads * c_dim, cols), BF),
        grid=(heads,),
        in_specs=[pl.BlockSpec((c_dim, nope), lambda h: (0, h)),
                  pl.BlockSpec((nope, cols), lambda h: (2 * h, col_block))],
        out_specs=pl.BlockSpec((c_dim, cols), lambda h: (h, 0)),
        compiler_params=_params(1),
        name="absorb_query",
    )(wuk, q_t)


def _page_copies(pt_ref, ck_hbm, kp_hbm, ck_buf, kp_buf, sem, step, slot, i, pages):
    page = pt_ref[step * pages + i]
    return (pltpu.make_async_copy(ck_hbm.at[page], ck_buf.at[slot, i], sem.at[0, slot]),
            pltpu.make_async_copy(kp_hbm.at[page], kp_buf.at[slot, i], sem.at[1, slot]))


def _decode_kernel(pt_ref, qlat_ref, qlt_ref, qpe_ref, cnew_ref, knew_ref, ck_hbm, kp_hbm, o_ref,
                   ck_buf, kp_buf, sem, m_ref, l_ref, acc_ref, *, pages, steps_per_seq):
    n = pl.program_id(0)
    j = n % steps_per_seq
    slot = n % DECODE_SLOTS

    def start_step(step):
        for i in range(pages):
            for cp in _page_copies(pt_ref, ck_hbm, kp_hbm, ck_buf, kp_buf, sem, step, step % DECODE_SLOTS, i, pages):
                cp.start()

    @pl.when(n == 0)
    def _():
        for step in range(DECODE_SLOTS - 1):
            @pl.when(step < pl.num_programs(0))
            def _():
                start_step(step)

    @pl.when(n + DECODE_SLOTS - 1 < pl.num_programs(0))
    def _():
        start_step(n + DECODE_SLOTS - 1)

    @pl.when(j == 0)
    def _():
        m_ref[...] = jnp.full(m_ref.shape, NEG_INF, F32)
        l_ref[...] = jnp.zeros(l_ref.shape, F32)
        acc_ref[...] = jnp.zeros(acc_ref.shape, F32)

    for i in range(pages):
        for cp in _page_copies(pt_ref, ck_hbm, kp_hbm, ck_buf, kp_buf, sem, n, slot, i, pages):
            cp.wait()

    ql = qlat_ref[0]
    qlt = qlt_ref[0]
    qp = qpe_ref[0]
    heads = ql.shape[0]
    cks = []
    scores = []
    for i in range(pages):
        ck = ck_buf[slot, i].astype(BF)
        kp = kp_buf[slot, i].astype(BF)
        cks.append(ck)
        st = jnp.dot(ck, qlt, preferred_element_type=F32)
        st = jnp.where(lax.broadcasted_iota(jnp.int32, st.shape, 1) < heads, st, 0.0)
        scores.append(st.T[:heads] + jnp.dot(qp, kp, preferred_element_type=F32))
    s = jnp.concatenate(scores, axis=-1)
    m_prev = m_ref[...]
    m_new = jnp.maximum(m_prev, jnp.max(s, axis=-1, keepdims=True))
    corr = jnp.exp2(m_prev - m_new)
    p = jnp.exp2(s - m_new)
    l_new = corr * l_ref[...] + jnp.sum(p, axis=-1, keepdims=True)
    pb = p.astype(BF)
    rows = cks[0].shape[0]
    pv = jnp.dot(pb[:, :rows], cks[0], preferred_element_type=F32)
    for i in range(1, pages):
        pv = pv + jnp.dot(pb[:, i * rows:(i + 1) * rows], cks[i], preferred_element_type=F32)
    acc_new = corr * acc_ref[...] + pv
    m_ref[...] = m_new
    l_ref[...] = l_new
    acc_ref[...] = acc_new

    @pl.when(j == steps_per_seq - 1)
    def _():
        cn = cnew_ref[0].astype(BF).astype(F32)
        kn = knew_ref[0].astype(BF).astype(F32)
        s_self = (jnp.sum(ql.astype(F32) * cn, axis=-1, keepdims=True)
                  + jnp.sum(qp.astype(F32) * kn, axis=-1, keepdims=True))
        m_fin = jnp.maximum(m_new, s_self)
        c2 = jnp.exp2(m_new - m_fin)
        p_self = jnp.exp2(s_self - m_fin)
        l_fin = c2 * l_new + p_self
        acc_fin = c2 * acc_new + p_self.astype(BF).astype(F32) * cn
        o_ref[0] = (acc_fin / l_fin).astype(o_ref.dtype)


def _decode_attention(page_table, q_lat, q_lat_t, q_pe, ckv_new, kpe_new, cache_ckv, cache_kpe_t, *, pages):
    Bd, H, C = q_lat.shape
    R = q_pe.shape[2]
    n_pages = page_table.shape[1]
    page_rows = cache_ckv.shape[1]
    assert n_pages % pages == 0
    sps = n_pages // pages
    per_b = lambda *shape: pl.BlockSpec((1,) + shape, lambda n, pt: (n // sps, 0, 0))
    hbm = pl.BlockSpec(memory_space=pl.ANY)
    grid_spec = pltpu.PrefetchScalarGridSpec(
        num_scalar_prefetch=1,
        grid=(Bd * sps,),
        in_specs=[per_b(H, C), per_b(C, 128), per_b(H, R), per_b(1, C), per_b(1, R), hbm, hbm],
        out_specs=per_b(H, C),
        scratch_shapes=[pltpu.VMEM((DECODE_SLOTS, pages, page_rows, C), cache_ckv.dtype),
                        pltpu.VMEM((DECODE_SLOTS, pages, R, page_rows), cache_kpe_t.dtype),
                        pltpu.SemaphoreType.DMA((2, DECODE_SLOTS)),
                        pltpu.VMEM((H, 1), F32), pltpu.VMEM((H, 1), F32), pltpu.VMEM((H, C), F32)],
    )
    return pl.pallas_call(
        functools.partial(_decode_kernel, pages=pages, steps_per_seq=sps),
        out_shape=jax.ShapeDtypeStruct((Bd, H, C), BF),
        grid_spec=grid_spec,
        compiler_params=_params(1),
        name="decode_attention",
    )(page_table.reshape(-1), q_lat, q_lat_t, q_pe, ckv_new, kpe_new, cache_ckv, cache_kpe_t)


def _value_up_kernel(o_hbm_ref, olat_ref, wuv_ref, o_ref):
    del o_hbm_ref
    o_ref[...] = jnp.dot(olat_ref[...], wuv_ref[...], preferred_element_type=F32).astype(o_ref.dtype)


def _value_up_into(o_all, o_lat, wuv, *, heads, c_dim, vdim, row_block, rows):
    return pl.pallas_call(
        _value_up_kernel,
        out_shape=jax.ShapeDtypeStruct(o_all.shape, o_all.dtype),
        grid=(heads,),
        in_specs=[pl.BlockSpec(memory_space=pl.ANY),
                  pl.BlockSpec((rows, c_dim), lambda h: (0, h)),
                  pl.BlockSpec((c_dim, vdim), lambda h: (0, h))],
        out_specs=pl.BlockSpec((rows, vdim), lambda h: (row_block, h)),
        input_output_aliases={0: 0},
        compiler_params=_params(1),
        name="value_up",
    )(o_all, o_lat, wuv)


def _router_kernel(x_ref, mix_ref, g_ref, lng_ref, lnb_ref, sc_ref, sh_ref, wrh_ref, wrl_ref, br_ref,
                   xo_ref, w_ref, i_ref, h_ref, *, n_exp, alpha):
    xn = _layernorm_rows(alpha * x_ref[...] + g_ref[...] * mix_ref[...], lng_ref[...], lnb_ref[...])
    xo_ref[...] = xn
    h = xn * (1.0 + sc_ref[...]) + sh_ref[...]
    h_hi = h.astype(BF)
    h_lo = (h - h_hi.astype(F32)).astype(BF)
    logits = (jnp.dot(h_hi, wrh_ref[...], preferred_element_type=F32)
              + jnp.dot(h_hi, wrl_ref[...], preferred_element_type=F32)
              + jnp.dot(h_lo, wrh_ref[...], preferred_element_type=F32)) + br_ref[...]
    lane = lax.broadcasted_iota(jnp.int32, logits.shape, 1)
    valid = lane < n_exp
    logits = jnp.where(valid, logits, NEG_INF)
    mx = jnp.max(logits, axis=-1, keepdims=True)
    e = jnp.exp(logits - mx)
    probs = e / jnp.sum(e, axis=-1, keepdims=True)
    big = jnp.int32(1 << 30)
    p1 = jnp.max(probs, axis=-1, keepdims=True)
    i1 = jnp.min(jnp.where((probs == p1) & valid, lane, big), axis=-1, keepdims=True)
    rest = jnp.where((lane == i1) | ~valid, -1.0, probs)
    p2 = jnp.max(rest, axis=-1, keepdims=True)
    i2 = jnp.min(jnp.where(rest == p2, lane, big), axis=-1, keepdims=True)
    denom = p1 + p2
    w_ref[...] = jnp.where(lane == 0, p1 / denom, jnp.where(lane == 1, p2 / denom, 0.0))
    i_ref[...] = jnp.where(lane == 0, i1, jnp.where(lane == 1, i2, 0))
    h_ref[...] = h


def _postln_router(x, mix, mod, g_chunk, ln_g, ln_b, sc_chunk, sh_chunk, wr_pad, br_pad, tiles_per_batch, n_exp,
                   *, alpha):
    T, D = x.shape
    row = pl.BlockSpec((ROW_TILE, D), lambda i: (i, 0))
    vec = pl.BlockSpec((1, D), lambda i: (0, 0))
    sel = pl.BlockSpec((ROW_TILE, 128), lambda i: (i, 0))
    wr_hi = wr_pad.astype(BF)
    wr_lo = (wr_pad - wr_hi.astype(F32)).astype(BF)
    wspec = pl.BlockSpec((D, 128), lambda i: (0, 0))
    return pl.pallas_call(
        functools.partial(_router_kernel, n_exp=n_exp, alpha=alpha),
        out_shape=[jax.ShapeDtypeStruct((T, D), F32), jax.ShapeDtypeStruct((T, 128), F32),
                   jax.ShapeDtypeStruct((T, 128), jnp.int32), jax.ShapeDtypeStruct((T, D), F32)],
        grid=(T // ROW_TILE,),
        in_specs=[row, row, _mod_spec(tiles_per_batch, D, g_chunk), vec, vec,
                  _mod_spec(tiles_per_batch, D, sc_chunk), _mod_spec(tiles_per_batch, D, sh_chunk),
                  wspec, wspec, pl.BlockSpec((1, 128), lambda i: (0, 0))],
        out_specs=[row, sel, sel, row],
        compiler_params=_params(1),
        name="postln_router",
    )(x, mix, mod, ln_g.reshape(1, D), ln_b.reshape(1, D), mod, mod, wr_hi, wr_lo, br_pad)


def _route_tables(idx2, n_exp, bm, n_blocks):
    T = idx2.shape[0]
    e_flat = idx2.reshape(-1)
    onehot = (e_flat[:, None] == jnp.arange(n_exp, dtype=jnp.int32)[None, :]).astype(jnp.int32)
    rank = jnp.sum((jnp.cumsum(onehot, axis=0) - onehot) * onehot, axis=1)
    counts = jnp.sum(onehot, axis=0)
    blocks_per = (counts + bm - 1) // bm
    blocks_end = jnp.cumsum(blocks_per)
    dest = (blocks_end - blocks_per)[e_flat] * bm + rank
    n_used = blocks_end[-1]
    b = jnp.minimum(jnp.arange(n_blocks, dtype=jnp.int32), n_used - 1)
    block_expert = jnp.sum((b[:, None] >= blocks_end[None, :]).astype(jnp.int32), axis=1)
    row_token = jnp.full((n_blocks * bm,), -1, jnp.int32).at[dest].set(jnp.arange(2 * T, dtype=jnp.int32) // 2)
    ids = jnp.arange(n_exp, dtype=jnp.int32)
    later_used = (ids[None, :] > ids[:, None]) & (blocks_per[None, :] > 0)
    next_of_expert = jnp.min(jnp.where(later_used, ids[None, :], n_exp), axis=1)
    next_expert = jnp.where(next_of_expert < n_exp, next_of_expert, -1)[block_expert]
    return (row_token, dest.astype(jnp.int32), block_expert.astype(jnp.int32), next_expert.astype(jnp.int32),
            n_used.reshape(1).astype(jnp.int32))


def _row_copy(src_hbm, row, dst, sem):
    return pltpu.make_async_copy(src_hbm.at[pl.ds(row, 1)], dst, sem)


ROW_DMA_UNROLL = 8


def _dispatch_kernel(tok_ref, h_hbm, o_ref, buf, sem, *, rows):
    i = pl.program_id(0)
    slot = i % 2

    def issue(step, s):
        def body(r8, c):
            for k in range(ROW_DMA_UNROLL):
                r = r8 * ROW_DMA_UNROLL + k
                tok = tok_ref[step * rows + r]

                @pl.when(tok >= 0)
                def _():
                    _row_copy(h_hbm, tok, buf.at[s, pl.ds(r, 1)], sem.at[s]).start(priority=k % 2)
            return c
        lax.fori_loop(0, rows // ROW_DMA_UNROLL, body, 0)

    @pl.when(i == 0)
    def _():
        buf[...] = jnp.zeros(buf.shape, buf.dtype)
        issue(0, 0)

    @pl.when(i + 1 < pl.num_programs(0))
    def _():
        issue(i + 1, 1 - slot)

    def wait_body(r, c):
        @pl.when(tok_ref[i * rows + r] >= 0)
        def _():
            _row_copy(h_hbm, 0, buf.at[slot, pl.ds(r, 1)], sem.at[slot]).wait()
        return c
    lax.fori_loop(0, rows, wait_body, 0, unroll=8)
    o_ref[...] = buf[slot].astype(o_ref.dtype)


def _dispatch(h, row_token, *, rows=256):
    T, D = h.shape
    Rn = row_token.shape[0]
    if Rn % rows:
        rows = ROW_TILE
    assert Rn % rows == 0 and rows % ROW_DMA_UNROLL == 0
    grid_spec = pltpu.PrefetchScalarGridSpec(
        num_scalar_prefetch=1,
        grid=(Rn // rows,),
        in_specs=[pl.BlockSpec(memory_space=pl.ANY)],
        out_specs=pl.BlockSpec((rows, D), lambda i, tok: (i, 0)),
        scratch_shapes=[pltpu.VMEM((2, rows, D), h.dtype), pltpu.SemaphoreType.DMA((2,))],
    )
    return pl.pallas_call(
        functools.partial(_dispatch_kernel, rows=rows),
        out_shape=jax.ShapeDtypeStruct((Rn, D), h.dtype),
        grid_spec=grid_spec,
        compiler_params=_params(1),
        name="moe_dispatch",
    )(row_token, h)


def _combine_postln_kernel(dest_ref, x_ref, g_ref, lng_ref, lnb_ref, w_ref, y_hbm, yp_ref, ys_ref, buf, sem,
                           *, alpha, rows, n_prompt_tiles):
    i = pl.program_id(0)
    slot = i % 2

    def issue(step, s):
        def body(r, c):
            a = (step * rows + r) * 2
            _row_copy(y_hbm, dest_ref[a], buf.at[s, 0, pl.ds(r, 1)], sem.at[s]).start(priority=0)
            _row_copy(y_hbm, dest_ref[a + 1], buf.at[s, 1, pl.ds(r, 1)], sem.at[s]).start(priority=1)
            return c
        lax.fori_loop(0, rows, body, 0, unroll=8)

    @pl.when(i == 0)
    def _():
        issue(0, 0)

    @pl.when(i + 1 < pl.num_programs(0))
    def _():
        issue(i + 1, 1 - slot)

    def wait_body(r, c):
        _row_copy(y_hbm, 0, buf.at[slot, 0, pl.ds(r, 1)], sem.at[slot]).wait()
        _row_copy(y_hbm, 0, buf.at[slot, 1, pl.ds(r, 1)], sem.at[slot]).wait()
        return c
    lax.fori_loop(0, rows, wait_body, 0, unroll=8)
    f = w_ref[:, 0:1] * buf[slot, 0] + w_ref[:, 1:2] * buf[slot, 1]
    xn = _layernorm_rows(alpha * x_ref[...] + g_ref[...] * f, lng_ref[...], lnb_ref[...])

    @pl.when(i < n_prompt_tiles)
    def _():
        yp_ref[...] = xn

    @pl.when(i >= n_prompt_tiles)
    def _():
        ys_ref[...] = xn


def _combine_postln(x, y, dest, wts, mod_g, g_chunk, ln_g, ln_b, tiles_per_batch, n_prompt_tiles, *, alpha):
    T, D = x.shape
    rows = ROW_TILE
    assert T == (n_prompt_tiles + 1) * rows
    row = pl.BlockSpec((rows, D), lambda i, d: (i, 0))
    vec = pl.BlockSpec((1, D), lambda i, d: (0, 0))
    grid_spec = pltpu.PrefetchScalarGridSpec(
        num_scalar_prefetch=1,
        grid=(T // rows,),
        in_specs=[row, pl.BlockSpec((rows, D), lambda i, d: (i // tiles_per_batch, g_chunk)), vec, vec,
                  pl.BlockSpec((rows, 128), lambda i, d: (i, 0)), pl.BlockSpec(memory_space=pl.ANY)],
        out_specs=[pl.BlockSpec((rows, D), lambda i, d: (jnp.minimum(i, n_prompt_tiles - 1), 0)),
                   pl.BlockSpec((rows, D), lambda i, d: (0, 0))],
        scratch_shapes=[pltpu.VMEM((2, 2, rows, D), F32), pltpu.SemaphoreType.DMA((2,))],
    )
    return pl.pallas_call(
        functools.partial(_combine_postln_kernel, alpha=alpha, rows=rows, n_prompt_tiles=n_prompt_tiles),
        out_shape=[jax.ShapeDtypeStruct((n_prompt_tiles * rows, D), F32), jax.ShapeDtypeStruct((rows, D), F32)],
        grid_spec=grid_spec,
        compiler_params=_params(1),
        name="moe_combine_postln",
    )(dest, x, mod_g, ln_g.reshape(1, D), ln_b.reshape(1, D), wts, y)


def _pick_bm(T, target):
    best = ROW_TILE
    for m in range(ROW_TILE, target + 1, ROW_TILE):
        if T % m == 0:
            best = m
    return best


def kernel(x_prompt, x_sample, cache_ckv, cache_kpe, page_table, c_prompt, c_sample, w_ada, b_ada, ln_g, ln_b, a_w_in, a_v_g, a_v_b, a_w_s, a_b_s, a_w_out, b_w_dq, b_q_g, b_w_uq, b_w_qr, b_w_o, kv_w_dkv, kv_g, kv_w_kr, kv_w_uk, kv_w_uv, ffn_w_gate, ffn_w_up, ffn_w_down, moe_w_router, moe_b_router, moe_w_gate, moe_w_up, moe_w_down):
    B, S, D = x_prompt.shape
    Bd, Td, _ = x_sample.shape
    depth = w_ada.shape[0]
    n_a = a_w_in.shape[0]
    assert depth == 2 and n_a == 1 and Td == 1 and Bd == ROW_TILE and S % ROW_TILE == 0
    A = a_w_in.shape[2] // 2
    G = a_w_s.shape[1]
    H, nope = b_w_uq.shape[2], b_w_uq.shape[3]
    R = b_w_qr.shape[3]
    C = kv_w_dkv.shape[1]
    vdim = kv_w_uv.shape[2]
    E = moe_w_router.shape[2]
    n_pages, page_rows = page_table.shape[1], cache_ckv.shape[1]
    past_len = n_pages * page_rows
    alpha = (2.0 * depth) ** 0.25
    scale = (nope + R) ** -0.5
    Tp = B * S
    T = Tp + Bd
    tiles_per_batch = S // ROW_TILE
    n_prompt_tiles = Tp // ROW_TILE
    bm = _pick_bm(T, 640)
    bm_mid = _pick_bm(T, 832)
    bm_big = _pick_bm(T, 1664)

    x = (x_prompt.reshape(Tp, D), x_sample.reshape(Bd, D))

    c_tiles = jnp.concatenate([jnp.repeat(c_prompt, ROW_TILE, axis=0), c_sample], axis=0)
    mods = [_wsmm(c_tiles, [w_ada], bm=c_tiles.shape[0], bn=1024, out_dtype=F32, act="silu_in",
                  bias=b_ada[l].reshape(1, -1), w_index=l, name="adaln") for l in range(depth)]
    SH_M, SC_M, G_M, SH_F, SC_F, G_F = range(6)

    half = R // 2
    inv = ROPE_BASE ** (-jnp.arange(half, dtype=F32) / half)
    pos = jnp.concatenate([jnp.tile(jnp.arange(S, dtype=jnp.int32), B),
                           jnp.full((Bd,), past_len, jnp.int32)]).astype(F32)
    ang = pos[:, None] * inv[None, :]
    cos2 = jnp.concatenate([jnp.cos(ang), jnp.cos(ang)], axis=-1)
    sin2 = jnp.concatenate([jnp.sin(ang), jnp.sin(ang)], axis=-1)
    qscale = scale * math.log2(math.e)
    row_pad = jnp.zeros((128 - R, T), F32)
    cos_qt = jnp.concatenate([cos2.T * qscale, row_pad], axis=0)
    sin_qt = jnp.concatenate([sin2.T * qscale, row_pad], axis=0)

    h = _modulate(x, mods[0], SC_M, SH_M, tiles_per_batch)
    u = _wsmm(h, [a_w_in], bm=bm_big, bn=512, out_dtype=BF, n_off=0, n_out=A, act="gelu", name="a_in_u")
    vpre = _wsmm(h, [a_w_in], bm=bm_big, bn=512, out_dtype=F32, n_off=A, n_out=A, act="gelu", name="a_in_v")
    tril = jnp.tril(jnp.ones((ROW_TILE, ROW_TILE), bool))
    ws_prompt = jnp.where(tril, a_w_s[0], 0.0)
    ws_sample = a_w_s[0][:, :1, :1] * jnp.eye(ROW_TILE, dtype=F32)
    ws_tiles = jnp.stack([ws_prompt, ws_sample]).astype(BF)
    gd = A // G
    b_prompt = jnp.repeat(a_b_s[0].T, gd, axis=1)
    b_sample = jnp.broadcast_to(b_prompt[:1], b_prompt.shape)
    b_tiles = jnp.stack([b_prompt, b_sample])
    gated, v_sample = _spatial_gate(vpre, u, ws_tiles, b_tiles, a_v_g[0], a_v_b[0], n_prompt_tiles)
    mix = _wsmm(gated, [a_w_out], bm=bm_mid, bn=512, out_dtype=F32, name="a_out")
    x, h = _postln(x, mix, mods[0], G_M, ln_g[0, 0], ln_b[0, 0], tiles_per_batch, alpha=alpha,
                   mod_next=mods[0], sc_chunk=SC_F, sh_chunk=SH_F)
    act = _wsmm(h, [ffn_w_gate, ffn_w_up], bm=bm_big, bn=512, out_dtype=BF, name="ffn_up")
    f = _wsmm(act, [ffn_w_down], bm=bm, bn=512, out_dtype=F32, name="ffn_down")
    x, h, xb = _postln(x, f, mods[0], G_F, ln_g[0, 1], ln_b[0, 1], tiles_per_batch, alpha=alpha,
                       mod_next=mods[1], sc_chunk=SC_M, sh_chunk=SH_M, emit_bf=True)

    w1, w2 = kv_w_kr[:, :half], kv_w_kr[:, half:]
    zpad = jnp.zeros((D, 128 - R), F32)
    wkv = jnp.concatenate([kv_w_dkv, kv_w_kr, zpad, -w2, w1, zpad], axis=1).astype(BF)
    wuk2 = kv_w_uk.reshape(C, H * nope)
    wuv2 = kv_w_uv.reshape(C, H * vdim)
    ckv, kpe, kfull, v_t = _kv_side(xb, wkv, kv_g.reshape(1, C), cos2, sin2, wuk2.astype(BF), wuv2.T.astype(BF),
                                    bm=bm, c_dim=C, r_dim=R, heads=H, nope=nope)

    wqr = b_w_qr[0]
    qpad = jnp.zeros(wqr.shape[:2] + (128 - R,), F32)
    wpe = jnp.concatenate([wqr, qpad], axis=-1).reshape(wqr.shape[0], H * 128)
    q_t = _q_side(h, b_w_dq[0].T.astype(BF), b_q_g[0].reshape(-1, 1), b_w_uq[0].reshape(-1, H * nope).T.astype(BF),
                  wpe.T.astype(BF), cos_qt, sin_qt, bm=bm, heads=H, nope=nope, scale=qscale, half=half)
    o_all = _flash_attention(kfull, q_t, v_t, batch=B, seq=S, heads=H, vdim=vdim, blk=min(1024, S), hb=min(8, H),
                             total_rows=T)

    q_lat_hc = _absorb_query(q_t, wuk2.astype(BF), heads=H, nope=nope, c_dim=C,
                             col_block=Tp // Bd, cols=Bd).reshape(H, C, Bd)
    q_lat = q_lat_hc.transpose(2, 0, 1)
    q_lat_t = jnp.pad(q_lat_hc.transpose(2, 1, 0), ((0, 0), (0, 0), (0, 128 - H)))
    q_pe_s = q_t[:, Tp:].reshape(H, 256, Bd)[:, nope:nope + R, :].transpose(2, 0, 1)
    o_lat = _decode_attention(page_table, q_lat, q_lat_t, q_pe_s, ckv[Tp:].reshape(Bd, 1, C),
                              kpe[Tp:].reshape(Bd, 1, R), cache_ckv, jnp.swapaxes(cache_kpe, 1, 2),
                              pages=min(32, n_pages))
    o_all = _value_up_into(o_all, o_lat.reshape(Bd, H * C), wuv2.astype(BF), heads=H, c_dim=C, vdim=vdim,
                           row_block=Tp // Bd, rows=Bd)
    mix = _wsmm(o_all, [b_w_o], bm=bm_big, bn=512, out_dtype=F32, name="attn_out")
    wr_pad = jnp.concatenate([moe_w_router[0], jnp.zeros((D, 128 - E), F32)], axis=1)
    br_pad = jnp.concatenate([moe_b_router[0], jnp.zeros((128 - E,), F32)]).reshape(1, 128)
    x, wts, idx, hf = _postln_router(x, mix, mods[1], G_M, ln_g[1, 0], ln_b[1, 0], SC_F, SH_F, wr_pad, br_pad,
                                     tiles_per_batch, E, alpha=alpha)
    bme = 384
    n_blocks = pl.cdiv(2 * T + E * (bme - 1), bme)
    row_token, dest, block_expert, next_expert, n_used = _route_tables(idx[:, :2], E, bme, n_blocks)
    grouped = dict(block_w=block_expert, next_w=next_expert, n_used=n_used)
    hs = _dispatch(hf, row_token)
    per_expert = lambda w: w.reshape((-1,) + w.shape[-2:])
    act = _wsmm(hs, [per_expert(moe_w_gate), per_expert(moe_w_up)], bm=bme, bn=1024, out_dtype=BF,
                name="moe_up", **grouped)
    ys = _wsmm(act, [per_expert(moe_w_down)], bm=bme, bn=512, out_dtype=F32, name="moe_down", **grouped)
    y_prompt, y_sample = _combine_postln(x, ys, dest, wts, mods[1], G_F, ln_g[1, 1], ln_b[1, 1], tiles_per_batch,
                                         n_prompt_tiles, alpha=alpha)
    return (y_prompt.reshape(B, S, D), y_sample.reshape(Bd, Td, D),
            ckv[:Tp].reshape(B, S, C), kpe[:Tp].reshape(B, S, R),
            ckv[Tp:].reshape(Bd, Td, C), kpe[Tp:].reshape(Bd, Td, R),
            v_sample.reshape(n_a, Bd, Td, A))
```
